```python
import math
import jax, jax.numpy as jnp
from jax import lax
import numpy as np

D_MODEL = 1024
BATCH = 32
SEQ = 2048
DEPTH = 2

DA_HEADS = 8
DA_HEAD_DIM = 64
DA_WIDTH = DA_HEADS * 2 * DA_HEAD_DIM
NA_HEADS = 16
NA_HEAD_DIM = 32
NA_WIDTH = NA_HEADS * NA_HEAD_DIM
GRID_W = 64
WIN_R = 8
WIN_C = 16
COL_BLOCK = WIN_C
HALO = 2 * WIN_C
Q_BLOCK = 128
N_EXPERTS = 256
TOP_K = 8
N_GROUPS = 8
TOPK_GROUPS = 4
EXPERT_DIM = 256
ROUTE_SCALE = 2.5
MOE_BLOCK = 128
ALPHA = (2 * DEPTH) ** 0.25
BETA = (8 * DEPTH) ** -0.25
LN_EPS = 1e-5
F32 = jnp.float32
IN_SIZES = (DA_WIDTH, DA_WIDTH, DA_WIDTH, NA_WIDTH, NA_WIDTH, NA_WIDTH, D_MODEL, D_MODEL)
IN_COLS = sum(IN_SIZES)

kernel_name = 'hybrid_diffattn_natten_moe_deepnorm'


def _layer_norm(x, g, b):
    xf = x.astype(F32)
    mu = xf.mean(-1, keepdims=True)
    var = jnp.square(xf - mu).mean(-1, keepdims=True)
    return ((xf - mu) * lax.rsqrt(var + LN_EPS) * g.astype(F32) + b.astype(F32)).astype(x.dtype)


def _rms_norm(x, g):
    xf = x.astype(F32)
    y = xf * lax.rsqrt(jnp.mean(jnp.square(xf), -1, keepdims=True) + LN_EPS)
    return (y * g.astype(F32)).astype(x.dtype)


def _alibi_slopes(n):
    return jnp.asarray(2.0 ** (-8.0 * np.arange(1, n + 1) / n), F32)


def _diff_attention(q, k, v, lam, subln_g, lam_init):
    B, S = q.shape[:2]
    nqb = S // Q_BLOCK
    slopes = _alibi_slopes(DA_HEADS)
    kpos = jnp.arange(S, dtype=F32)
    q_blocks = q.reshape(B, nqb, Q_BLOCK, DA_HEADS, 2, DA_HEAD_DIM).transpose(1, 0, 2, 3, 4, 5)

    def block(args):
        q_blk, i = args
        s = jnp.einsum('bqhpd,bkhpd->bhpqk', q_blk, k, preferred_element_type=F32)
        qpos = (i * Q_BLOCK + jnp.arange(Q_BLOCK)).astype(F32)
        dist = jnp.abs(qpos[:, None] - kpos[None, :])
        s = s - slopes[:, None, None, None] * dist
        a = jax.nn.softmax(s, axis=-1)
        w = a[:, :, 0] - lam * a[:, :, 1]
        return jnp.einsum('bhqk,bkhe->bqhe', w.astype(v.dtype), v)

    o = lax.map(block, (q_blocks, jnp.arange(nqb)))
    o = o.transpose(1, 0, 2, 3, 4).reshape(B, S, DA_HEADS, 2 * DA_HEAD_DIM)
    o = _rms_norm(o, subln_g) * (1.0 - lam_init)
    return o.reshape(B, S, DA_WIDTH)


def _na_tables():
    ncb = GRID_W // COL_BLOCK
    col_start = np.clip(np.arange(GRID_W) - WIN_C // 2, 0, GRID_W - WIN_C)
    halo_start = np.clip(np.arange(ncb) * COL_BLOCK - WIN_C // 2, 0, GRID_W - HALO)
    halo_cols = halo_start[:, None] + np.arange(HALO)[None, :]
    qcol = np.arange(ncb)[:, None] * COL_BLOCK + np.arange(COL_BLOCK)[None, :]
    kcol = halo_cols[:, None, :]
    cs = col_start[qcol][:, :, None]
    col_mask = (kcol >= cs) & (kcol < cs + WIN_C)
    col_off = np.clip(kcol - qcol[:, :, None] + WIN_C - 1, 0, 2 * WIN_C - 2)
    return halo_cols, col_mask, col_off


def _neighbourhood_attention(q, k, v, rpb):
    B, S = q.shape[:2]
    rows = S // GRID_W
    kh = min(WIN_R, rows)
    ncb = GRID_W // COL_BLOCK
    halo_cols, col_mask, col_off = _na_tables()

    def to_grid(t):
        return t.reshape(B, rows, GRID_W, NA_HEADS, NA_HEAD_DIM).transpose(0, 3, 1, 2, 4)

    qg, kg, vg = to_grid(q), to_grid(k), to_grid(v)
    q_rows = qg.transpose(2, 0, 1, 3, 4).reshape(rows, B, NA_HEADS, ncb, COL_BLOCK, NA_HEAD_DIM)
    col_bias = rpb[:, :, col_off]
    col_mask_b = jnp.asarray(col_mask)[:, :, None, :]

    def row(args):
        q_r, r = args
        rs = jnp.clip(r - WIN_R // 2, 0, rows - kh)
        k_r = lax.dynamic_slice_in_dim(kg, rs, kh, axis=2)[:, :, :, halo_cols]
        v_r = lax.dynamic_slice_in_dim(vg, rs, kh, axis=2)[:, :, :, halo_cols]
        row_idx = rs + jnp.arange(kh) - r + WIN_R - 1
        bias = col_bias[:, row_idx].transpose(0, 2, 3, 1, 4)
        s = jnp.einsum('bhjqd,bhrjkd->bhjqrk', q_r, k_r, preferred_element_type=F32) + bias
        s = jnp.where(col_mask_b, s, -jnp.inf)
        p = jax.nn.softmax(s, axis=(-2, -1))
        o = jnp.einsum('bhjqrk,bhrjke->bhjqe', p.astype(v.dtype), v_r)
        return o.reshape(B, NA_HEADS, GRID_W, NA_HEAD_DIM)

    o = lax.map(row, (q_rows, jnp.arange(rows)))
    return o.transpose(1, 0, 3, 2, 4).reshape(B, S, NA_WIDTH)


def _mixer(x, w_in, b_gate, lam_q1, lam_k1, lam_q2, lam_k2, subln_g, w_proj_a, rpb, w_proj_b, w_out, lam_init):
    B, S, _ = x.shape
    split_pts = [int(c) for c in np.cumsum(IN_SIZES)[:-1]]
    qa, ka, va, qn, kn, vn, ga, gb = jnp.split(x @ w_in, split_pts, axis=-1)
    qa = qa.reshape(B, S, DA_HEADS, 2, DA_HEAD_DIM) * (DA_HEAD_DIM ** -0.5)
    ka = ka.reshape(B, S, DA_HEADS, 2, DA_HEAD_DIM)
    va = va.reshape(B, S, DA_HEADS, 2 * DA_HEAD_DIM)
    lam = (jnp.exp(jnp.sum(lam_q1.astype(F32) * lam_k1.astype(F32)))
           - jnp.exp(jnp.sum(lam_q2.astype(F32) * lam_k2.astype(F32))) + lam_init)
    y_a = _diff_attention(qa, ka, va, lam, subln_g, lam_init) @ w_proj_a
    y_b = _neighbourhood_attention(qn * (NA_HEAD_DIM ** -0.5), kn, vn, rpb) @ w_proj_b
    g_a = jax.nn.sigmoid(ga + b_gate[0])
    g_b = jax.nn.sigmoid(gb + b_gate[1])
    return (g_a * y_a + g_b * y_b) @ w_out


def _moe(x, router_w, router_bias, wg, wu, wd, swg, swu, swd):
    B, S, D = x.shape
    T = B * S
    xt = x.reshape(T, D)
    scores = jax.nn.sigmoid((xt @ router_w).astype(F32))
    biased = scores + router_bias.astype(F32)
    per_group = N_EXPERTS // N_GROUPS
    grp = lax.top_k(biased.reshape(T, N_GROUPS, per_group), 2)[0].sum(-1)
    _, gsel = lax.top_k(grp, TOPK_GROUPS)
    gmask = jax.nn.one_hot(gsel, N_GROUPS, dtype=F32).sum(-2) > 0
    emask = jnp.repeat(gmask, per_group, axis=-1)
    _, eidx = lax.top_k(jnp.where(emask, biased, -jnp.inf), TOP_K)
    gate = jnp.take_along_axis(scores, eidx, axis=-1)
    gate = gate / gate.sum(-1, keepdims=True) * ROUTE_SCALE
    M = T * TOP_K
    flat_e = eidx.reshape(M)
    flat_t = jnp.repeat(jnp.arange(T, dtype=jnp.int32), TOP_K)
    flat_w = gate.reshape(M)
    order = jnp.argsort(flat_e)
    se, st, sw = flat_e[order], flat_t[order], flat_w[order]
    counts = jnp.bincount(flat_e, length=N_EXPERTS)
    padded = (counts + MOE_BLOCK - 1) // MOE_BLOCK * MOE_BLOCK
    pad_end = jnp.cumsum(padded)
    pad_start = pad_end - padded
    start = jnp.cumsum(counts) - counts
    slot = pad_start[se] + jnp.arange(M) - start[se]
    nb = -(-(M + N_EXPERTS * (MOE_BLOCK - 1)) // MOE_BLOCK)
    n_slots = nb * MOE_BLOCK
    slot_tok = jnp.zeros((n_slots,), jnp.int32).at[slot].set(st)
    slot_w = jnp.zeros((n_slots,), F32).at[slot].set(sw)
    blk_exp = jnp.minimum(jnp.searchsorted(pad_end, jnp.arange(nb) * MOE_BLOCK, side='right'), N_EXPERTS - 1)

    def step(acc, inp):
        tok, w, e = inp
        xb = xt[tok]
        h = jax.nn.silu(xb @ wg[e]) * (xb @ wu[e])
        return acc.at[tok].add((h @ wd[e]) * w[:, None].astype(x.dtype)), None

    routed, _ = lax.scan(step, jnp.zeros_like(xt),
                         (slot_tok.reshape(nb, MOE_BLOCK), slot_w.reshape(nb, MOE_BLOCK), blk_exp))
    shared = (jax.nn.silu(xt @ swg) * (xt @ swu)) @ swd
    return (routed + shared).reshape(B, S, D)


def setup_inputs(seed: int = 0) -> dict:
    key = jax.random.key(seed)
    ks = jax.random.split(key, 32)
    L, D = DEPTH, D_MODEL

    def nrm(k, shape, scale):
        return jax.random.normal(k, shape, F32) * scale

    off = np.concatenate([[0], np.cumsum(IN_SIZES)])
    col_scale = np.ones((IN_COLS,), np.float32)
    col_scale[off[2]:off[3]] = BETA
    col_scale[off[5]:off[6]] = BETA
    return {
        'x': nrm(ks[0], (BATCH, SEQ, D), 1.0),
        'emb_ln_g': 1.0 + nrm(ks[1], (D,), 0.02),
        'emb_ln_b': nrm(ks[2], (D,), 0.02),
        'w_in': nrm(ks[3], (L, D, IN_COLS), D ** -0.5) * jnp.asarray(col_scale),
        'b_gate': nrm(ks[4], (L, 2, D), 0.01),
        'lam_q1': nrm(ks[5], (L, DA_HEAD_DIM), 0.1),
        'lam_k1': nrm(ks[6], (L, DA_HEAD_DIM), 0.1),
        'lam_q2': nrm(ks[7], (L, DA_HEAD_DIM), 0.1),
        'lam_k2': nrm(ks[8], (L, DA_HEAD_DIM), 0.1),
        'subln_g': 1.0 + nrm(ks[9], (L, 2 * DA_HEAD_DIM), 0.02),
        'w_proj_a': nrm(ks[10], (L, DA_WIDTH, D), BETA * DA_WIDTH ** -0.5),
        'na_rpb': nrm(ks[11], (L, NA_HEADS, 2 * WIN_R - 1, 2 * WIN_C - 1), 0.1),
        'w_proj_b': nrm(ks[12], (L, NA_WIDTH, D), BETA * NA_WIDTH ** -0.5),
        'w_out': nrm(ks[13], (L, D, D), BETA * D ** -0.5),
        'ln1_g': 1.0 + nrm(ks[14], (L, D), 0.02),
        'ln1_b': nrm(ks[15], (L, D), 0.02),
        'router_w': nrm(ks[16], (L, D, N_EXPERTS), D ** -0.5),
        'router_bias': nrm(ks[17], (L, N_EXPERTS), 0.01),
        'exp_w_gate': nrm(ks[18], (L, N_EXPERTS, D, EXPERT_DIM), D ** -0.5),
        'exp_w_up': nrm(ks[19], (L, N_EXPERTS, D, EXPERT_DIM), D ** -0.5),
        'exp_w_down': nrm(ks[20], (L, N_EXPERTS, EXPERT_DIM, D), BETA * EXPERT_DIM ** -0.5),
        'sh_w_gate': nrm(ks[21], (L, D, EXPERT_DIM), D ** -0.5),
        'sh_w_up': nrm(ks[22], (L, D, EXPERT_DIM), D ** -0.5),
        'sh_w_down': nrm(ks[23], (L, EXPERT_DIM, D), BETA * EXPERT_DIM ** -0.5),
        'ln2_g': 1.0 + nrm(ks[24], (L, D), 0.02),
        'ln2_b': nrm(ks[25], (L, D), 0.02),
    }


def reference(x, emb_ln_g, emb_ln_b, w_in, b_gate, lam_q1, lam_k1, lam_q2, lam_k2, subln_g, w_proj_a, na_rpb,
              w_proj_b, w_out, ln1_g, ln1_b, router_w, router_bias, exp_w_gate, exp_w_up, exp_w_down,
              sh_w_gate, sh_w_up, sh_w_down, ln2_g, ln2_b):
    h = _layer_norm(x, emb_ln_g, emb_ln_b)
    for l in range(DEPTH):
        lam_init = 0.8 - 0.6 * math.exp(-0.3 * l)
        m = _mixer(h, w_in[l], b_gate[l], lam_q1[l], lam_k1[l], lam_q2[l], lam_k2[l], subln_g[l],
                   w_proj_a[l], na_rpb[l], w_proj_b[l], w_out[l], lam_init)
        h = _layer_norm(ALPHA * h + m, ln1_g[l], ln1_b[l])
        f = _moe(h, router_w[l], router_bias[l], exp_w_gate[l], exp_w_up[l], exp_w_down[l],
                 sh_w_gate[l], sh_w_up[l], sh_w_down[l])
        h = _layer_norm(ALPHA * h + f, ln2_g[l], ln2_b[l])
    return h
```

```python
import functools
import math

import numpy as np
import jax
import jax.numpy as jnp
from jax import lax
from jax.experimental import pallas as pl
from jax.experimental.pallas import tpu as pltpu

F32 = jnp.float32
BF16 = jnp.bfloat16
U32 = jnp.uint32
I32 = jnp.int32

DA_HEADS = 8
DA_HEAD_DIM = 64
NA_HEADS = 16
NA_HEAD_DIM = 32
GRID_W = 64
WIN_R = 8
WIN_C = 16
N_EXPERTS = 256
TOP_K = 8
N_GROUPS = 8
TOPK_GROUPS = 4
ROUTE_SCALE = 2.5
DEPTH = 2
ALPHA = (2 * DEPTH) ** 0.25
LN_EPS = 1e-5

LANES = 128
NA_HEADS_PER_BLOCK = LANES // NA_HEAD_DIM
VMEM_LIMIT = 56 * 1024 * 1024

TM_PROJ = 512
TQ_DA = 512
TT_ROUTER = 512
TD_MOE = 256
BLK_MOE = 256
NEG_BIG = -1e30


def _cparams(n_axes):
    return pltpu.CompilerParams(
        dimension_semantics=("arbitrary",) * n_axes, vmem_limit_bytes=VMEM_LIMIT)


def _layer_norm(x, g, b):
    mu = jnp.mean(x, axis=-1, keepdims=True)
    xc = x - mu
    var = jnp.mean(xc * xc, axis=-1, keepdims=True)
    return xc * lax.rsqrt(var + LN_EPS) * g + b


def _pack_bf16_pairs(y):
    w = y.shape[1] // 2
    lo = lax.bitcast_convert_type(y[:, :w].astype(BF16).astype(F32), U32)
    hi = lax.bitcast_convert_type(y[:, w:].astype(BF16).astype(F32), U32)
    return (hi & jnp.uint32(0xFFFF0000)) | (lo >> 16)


def _unpack_bf16_pairs(u):
    lo = lax.bitcast_convert_type(u << 16, F32)
    hi = lax.bitcast_convert_type(u & jnp.uint32(0xFFFF0000), F32)
    return lo, hi


def _inproj_kernel(apply_ln, seg_widths, seg_scales, x_ref, g_ref, b_ref, w_ref, *out_refs):
    x = x_ref[...]
    if apply_ln:
        x = _layer_norm(x, g_ref[...], b_ref[...])
        out_refs[0][...] = x
        out_refs = out_refs[1:]
    xb = x.astype(BF16)
    off = 0
    for ref, width, scale in zip(out_refs, seg_widths, seg_scales):
        y = jnp.dot(xb, w_ref[:, off:off + width], preferred_element_type=F32)
        if scale != 1.0:
            y = y * scale
        ref[...] = y.astype(BF16)
        off += width


def _inproj(x, ln_g, ln_b, w_in_bf16, apply_ln):
    T, D = x.shape
    da_w = DA_HEADS * 2 * DA_HEAD_DIM
    na_w = NA_HEADS * NA_HEAD_DIM
    seg_widths = (da_w, da_w, da_w, na_w, na_w, na_w, D, D)
    seg_scales = (DA_HEAD_DIM ** -0.5, 1.0, 1.0, NA_HEAD_DIM ** -0.5, 1.0, 1.0, 1.0, 1.0)
    tm = min(TM_PROJ, T)
    n_cols = w_in_bf16.shape[1]
    row = lambda i: (i, 0)
    const = lambda i: (0, 0)
    out_shape = [jax.ShapeDtypeStruct((T, w), BF16) for w in seg_widths]
    out_specs = [pl.BlockSpec((tm, w), row) for w in seg_widths]
    if apply_ln:
        out_shape = [jax.ShapeDtypeStruct((T, D), F32)] + out_shape
        out_specs = [pl.BlockSpec((tm, D), row)] + out_specs
    return pl.pallas_call(
        functools.partial(_inproj_kernel, apply_ln, seg_widths, seg_scales),
        grid=(T // tm,),
        in_specs=[
            pl.BlockSpec((tm, D), row),
            pl.BlockSpec((1, D), const),
            pl.BlockSpec((1, D), const),
            pl.BlockSpec((D, n_cols), const, pipeline_mode=pl.Buffered(1)),
        ],
        out_specs=out_specs,
        out_shape=out_shape,
        compiler_params=_cparams(1),
        name="inproj",
    )(x, ln_g.reshape(1, D), ln_b.reshape(1, D), w_in_bf16)


def _diffattn_kernel(lam_init, slopes_ref, q_ref, k_ref, v_ref, lamv_ref, g_ref, o_ref, bias_ref):
    h = pl.program_id(0)
    qi = pl.program_id(1)
    b = pl.program_id(2)
    tq = q_ref.shape[1]
    S = k_ref.shape[1]
    d = DA_HEAD_DIM

    @pl.when(b == 0)
    def _():
        qpos = qi * tq + lax.broadcasted_iota(I32, (tq, S), 0)
        kpos = lax.broadcasted_iota(I32, (tq, S), 1)
        bias_ref[...] = jnp.abs(qpos - kpos).astype(F32) * slopes_ref[h]

    lv = lamv_ref[...]
    lam = (jnp.exp(jnp.sum(lv[0:1] * lv[1:2], axis=-1, keepdims=True))
           - jnp.exp(jnp.sum(lv[2:3] * lv[3:4], axis=-1, keepdims=True)) + lam_init)

    q = q_ref[0]
    lane = lax.broadcasted_iota(I32, q.shape, 1)
    zero = jnp.zeros_like(q)
    qq = jnp.concatenate([jnp.where(lane < d, q, zero), jnp.where(lane >= d, q, zero)], axis=0)
    s = lax.dot_general(qq, k_ref[0], (((1,), (1,)), ((), ())), preferred_element_type=F32)
    s = s.reshape(2, tq, S) - bias_ref[...][None]
    m = jnp.max(s, axis=-1, keepdims=True)
    e = jnp.exp(s - m)
    r = 1.0 / jnp.sum(e, axis=-1, keepdims=True)
    w = e[0] * r[0] - e[1] * (lam * r[1])
    o = jnp.dot(w.astype(BF16), v_ref[0], preferred_element_type=F32)
    ms = jnp.mean(o * o, axis=-1, keepdims=True)
    o = o * lax.rsqrt(ms + LN_EPS) * (g_ref[...] * (1.0 - lam_init))
    o_ref[0] = o.astype(BF16)


def _diffattn(qa, ka, va, lamv, subln_g, lam_init):
    B, S, W = qa.shape
    hw = 2 * DA_HEAD_DIM
    tq = min(TQ_DA, S)
    slopes = jnp.asarray(2.0 ** (-8.0 * np.arange(1, DA_HEADS + 1) / DA_HEADS), F32)
    grid_spec = pltpu.PrefetchScalarGridSpec(
        num_scalar_prefetch=1,
        grid=(DA_HEADS, S // tq, B),
        in_specs=[
            pl.BlockSpec((1, tq, hw), lambda h, qi, b, sl: (b, qi, h)),
            pl.BlockSpec((1, S, hw), lambda h, qi, b, sl: (b, 0, h)),
            pl.BlockSpec((1, S, hw), lambda h, qi, b, sl: (b, 0, h)),
            pl.BlockSpec((4, DA_HEAD_DIM), lambda h, qi, b, sl: (0, 0)),
            pl.BlockSpec((1, hw), lambda h, qi, b, sl: (0, 0)),
        ],
        out_specs=pl.BlockSpec((1, tq, hw), lambda h, qi, b, sl: (b, qi, h)),
        scratch_shapes=[pltpu.VMEM((tq, S), F32)],
    )
    return pl.pallas_call(
        functools.partial(_diffattn_kernel, lam_init),
        grid_spec=grid_spec,
        out_shape=jax.ShapeDtypeStruct((B, S, W), BF16),
        compiler_params=_cparams(3),
        name="diffattn",
    )(slopes, qa, ka, va, lamv, subln_g.reshape(1, hw))


def _na_bias_table(rpb, rows):
    kh = min(WIN_R, rows)
    qcol = np.arange(GRID_W)
    kcol = np.arange(GRID_W)
    cs = np.clip(qcol - WIN_C // 2, 0, GRID_W - WIN_C)
    col_mask = (kcol[None, :] >= cs[:, None]) & (kcol[None, :] < cs[:, None] + WIN_C)
    col_off = np.clip(kcol[None, :] - qcol[:, None] + WIN_C - 1, 0, 2 * WIN_C - 2)
    v = np.arange(kh)
    j = np.arange(kh)
    row_idx = np.clip(j[None, :] - v[:, None] + WIN_R - 1, 0, 2 * WIN_R - 2)
    t = rpb[:, row_idx]
    t = t[:, :, :, col_off]
    t = jnp.where(jnp.asarray(col_mask)[None, None, None], t, NEG_BIG)
    t = t.transpose(1, 0, 3, 2, 4)
    nblk = NA_HEADS // NA_HEADS_PER_BLOCK
    return t.reshape(kh, nblk, NA_HEADS_PER_BLOCK * GRID_W, kh * GRID_W)


def _natten_kernel(rows, kh, q_ref, k_ref, v_ref, bias_ref, o_ref):
    hid = lax.broadcasted_iota(I32, (GRID_W, LANES), 1) // NA_HEAD_DIM

    def row(r, carry):
        rs = jnp.clip(r - WIN_R // 2, 0, rows - kh)
        var = r - rs
        q0 = pl.multiple_of(r * GRID_W, GRID_W)
        k0 = pl.multiple_of(rs * GRID_W, GRID_W)
        qr = q_ref[0, pl.ds(q0, GRID_W), :]
        zero = jnp.zeros_like(qr)
        qq = jnp.concatenate(
            [jnp.where(hid == hh, qr, zero) for hh in range(NA_HEADS_PER_BLOCK)], axis=0)
        kb = k_ref[0, pl.ds(k0, kh * GRID_W), :]
        vb = v_ref[0, pl.ds(k0, kh * GRID_W), :]
        s = lax.dot_general(qq, kb, (((1,), (1,)), ((), ())), preferred_element_type=F32)
        s = s + bias_ref[var, 0]
        m = jnp.max(s, axis=-1, keepdims=True)
        e = jnp.exp(s - m)
        p = e * (1.0 / jnp.sum(e, axis=-1, keepdims=True))
        oo = jnp.dot(p.astype(BF16), vb, preferred_element_type=F32)
        o = jnp.zeros((GRID_W, LANES), F32)
        for hh in range(NA_HEADS_PER_BLOCK):
            o = o + jnp.where(hid == hh, oo[hh * GRID_W:(hh + 1) * GRID_W], 0.0)
        o_ref[0, pl.ds(q0, GRID_W), :] = o.astype(BF16)
        return carry

    lax.fori_loop(0, rows, row, 0)


def _natten(qn, kn, vn, bias_tab):
    B, S, W = qn.shape
    rows = S // GRID_W
    kh = min(WIN_R, rows)
    nblk = W // LANES
    blk = lambda g, b: (b, 0, g)
    return pl.pallas_call(
        functools.partial(_natten_kernel, rows, kh),
        grid=(nblk, B),
        in_specs=[
            pl.BlockSpec((1, S, LANES), blk),
            pl.BlockSpec((1, S, LANES), blk),
            pl.BlockSpec((1, S, LANES), blk),
            pl.BlockSpec((kh, 1, NA_HEADS_PER_BLOCK * GRID_W, kh * GRID_W), lambda g, b: (0, g, 0, 0)),
        ],
        out_specs=pl.BlockSpec((1, S, LANES), blk),
        out_shape=jax.ShapeDtypeStruct((B, S, W), BF16),
        compiler_params=_cparams(2),
        name="natten",
    )(qn, kn, vn, bias_tab)


def _mixout_kernel(oa_ref, ob_ref, ga_ref, gb_ref, h_ref, bg_ref, wa_ref, wb_ref, wo_ref,
                   lg_ref, lb_ref, h1_ref, h1p_ref):
    ya = jnp.dot(oa_ref[...], wa_ref[...], preferred_element_type=F32)
    yb = jnp.dot(ob_ref[...], wb_ref[...], preferred_element_type=F32)
    g_a = jax.nn.sigmoid(ga_ref[...].astype(F32) + bg_ref[0:1, :])
    g_b = jax.nn.sigmoid(gb_ref[...].astype(F32) + bg_ref[1:2, :])
    z = (g_a * ya + g_b * yb).astype(BF16)
    m = jnp.dot(z, wo_ref[...], preferred_element_type=F32)
    y = _layer_norm(ALPHA * h_ref[...] + m, lg_ref[...], lb_ref[...])
    h1_ref[...] = y
    h1p_ref[...] = _pack_bf16_pairs(y)


def _mixout(oa, ob, ga, gb, h, b_gate, wa, wb, wo, ln_g, ln_b):
    T, D = h.shape
    tm = min(TM_PROJ, T)
    row = lambda i: (i, 0)
    const = lambda i: (0, 0)
    res = lambda a: pl.BlockSpec(a.shape, const, pipeline_mode=pl.Buffered(1))
    return pl.pallas_call(
        _mixout_kernel,
        grid=(T // tm,),
        in_specs=[
            pl.BlockSpec((tm, oa.shape[1]), row),
            pl.BlockSpec((tm, ob.shape[1]), row),
            pl.BlockSpec((tm, D), row),
            pl.BlockSpec((tm, D), row),
            pl.BlockSpec((tm, D), row),
            pl.BlockSpec((2, D), const),
            res(wa), res(wb), res(wo),
            pl.BlockSpec((1, D), const),
            pl.BlockSpec((1, D), const),
        ],
        out_specs=[pl.BlockSpec((tm, D), row), pl.BlockSpec((tm, D // 2), row)],
        out_shape=[jax.ShapeDtypeStruct((T, D), F32), jax.ShapeDtypeStruct((T, D // 2), U32)],
        compiler_params=_cparams(1),
        name="mixout",
    )(oa, ob, ga, gb, h, b_gate, wa, wb, wo, ln_g.reshape(1, D), ln_b.reshape(1, D))


def _router_kernel(h_ref, rwt_ref, rb_ref, tri_ref, eidx_ref, gate_ref, rank_ref, cnt_ref, carry_ref):
    i = pl.program_id(0)
    tt = h_ref.shape[0]
    E, G = N_EXPERTS, N_GROUPS
    P = E // G
    neg = -jnp.inf

    @pl.when(i == 0)
    def _():
        carry_ref[...] = jnp.zeros_like(carry_ref)

    hb = h_ref[...].astype(BF16)
    logits = lax.dot_general(rwt_ref[...], hb, (((1,), (1,)), ((), ())), preferred_element_type=F32)
    scores = jax.nn.sigmoid(logits)
    biased = scores + rb_ref[...]
    b3 = biased.reshape(G, P, tt)
    s3 = scores.reshape(G, P, tt)
    pi = lax.broadcasted_iota(I32, (G, P, tt), 1)
    ei = lax.broadcasted_iota(I32, (G, P, tt), 0) * P + pi

    m1 = jnp.max(b3, axis=1, keepdims=True)
    i1 = jnp.min(jnp.where(b3 == m1, pi, P), axis=1, keepdims=True)
    m2 = jnp.max(jnp.where(pi == i1, neg, b3), axis=1, keepdims=True)
    grp = m1 + m2
    gi = lax.broadcasted_iota(I32, (G, 1, tt), 0)
    gsel = jnp.zeros((G, 1, tt), F32)
    for _ in range(TOPK_GROUPS):
        gm = jnp.max(grp, axis=0, keepdims=True)
        gidx = jnp.min(jnp.where(grp == gm, gi, G), axis=0, keepdims=True)
        hit = gi == gidx
        gsel = jnp.where(hit, 1.0, gsel)
        grp = jnp.where(hit, neg, grp)

    cand = jnp.where(gsel > 0.0, b3, neg)
    sel = jnp.zeros((G, P, tt), F32)
    eids, gates = [], []
    for _ in range(TOP_K):
        mk = jnp.max(cand, axis=(0, 1), keepdims=True)
        ik = jnp.min(jnp.where(cand == mk, ei, E), axis=(0, 1), keepdims=True)
        hit = ei == ik
        gates.append(jnp.sum(jnp.where(hit, s3, 0.0), axis=(0, 1), keepdims=True))
        eids.append(ik)
        cand = jnp.where(hit, neg, cand)
        sel = jnp.where(hit, 1.0, sel)

    gsum = gates[0]
    for gk in gates[1:]:
        gsum = gsum + gk
    gscale = ROUTE_SCALE / gsum

    sel2 = sel.reshape(E, tt)
    prefix = jnp.dot(sel2.astype(BF16), tri_ref[...], preferred_element_type=F32)
    base3 = (prefix + carry_ref[...]).reshape(G, P, tt)
    ranks = [jnp.sum(jnp.where(ei == ik, base3, 0.0), axis=(0, 1), keepdims=True) for ik in eids]
    carry_ref[...] = carry_ref[...] + jnp.sum(sel2, axis=1, keepdims=True)
    cnt_ref[...] = carry_ref[...]

    eidx_ref[...] = jnp.concatenate([x.reshape(1, tt) for x in eids], axis=0)
    gate_ref[...] = jnp.concatenate([(g * gscale).reshape(1, tt) for g in gates], axis=0)
    rank_ref[...] = jnp.concatenate([x.reshape(1, tt) for x in ranks], axis=0).astype(I32)


def _router(h1, router_w, router_bias):
    T, D = h1.shape
    E = N_EXPERTS
    tt = min(TT_ROUTER, T)
    rwt = router_w.T.astype(BF16)
    tri = jnp.asarray(np.triu(np.ones((tt, tt), np.float32), k=1), BF16)
    const = lambda i: (0, 0)
    col = lambda i: (0, i)
    return pl.pallas_call(
        _router_kernel,
        grid=(T // tt,),
        in_specs=[
            pl.BlockSpec((tt, D), lambda i: (i, 0)),
            pl.BlockSpec((E, D), const),
            pl.BlockSpec((E, 1), const),
            pl.BlockSpec((tt, tt), const),
        ],
        out_specs=[
            pl.BlockSpec((TOP_K, tt), col),
            pl.BlockSpec((TOP_K, tt), col),
            pl.BlockSpec((TOP_K, tt), col),
            pl.BlockSpec((E, 1), const),
        ],
        out_shape=[
            jax.ShapeDtypeStruct((TOP_K, T), I32),
            jax.ShapeDtypeStruct((TOP_K, T), F32),
            jax.ShapeDtypeStruct((TOP_K, T), I32),
            jax.ShapeDtypeStruct((E, 1), F32),
        ],
        scratch_shapes=[pltpu.VMEM((E, 1), F32)],
        compiler_params=_cparams(1),
        name="router",
    )(h1, rwt, router_bias.reshape(E, 1).astype(F32), tri)


def _dispatch_kernel(slots_hbm, xp_ref, xs_hbm, slot_smem, sem_s, sem_d):
    i = pl.program_id(0)
    td = xp_ref.shape[0]
    cp = pltpu.make_async_copy(slots_hbm.at[i], slot_smem, sem_s)
    cp.start()
    cp.wait()

    def row_copy(t, s):
        return pltpu.make_async_copy(xp_ref.at[pl.ds(t, 1)], xs_hbm.at[pl.ds(s, 1)], sem_d)

    def issue(t, carry):
        for k in range(TOP_K):
            row_copy(t, slot_smem[k * td + t]).start()
        return carry

    lax.fori_loop(0, td, issue, 0)

    def drain(t, carry):
        for k in range(TOP_K):
            row_copy(t, slot_smem[k * td + t]).wait()
        return carry

    lax.fori_loop(0, td, drain, 0)


def _dispatch(slots_tiles, h1p, n_slots):
    T, Wp = h1p.shape
    td = min(TD_MOE, T)
    return pl.pallas_call(
        _dispatch_kernel,
        grid=(T // td,),
        in_specs=[
            pl.BlockSpec(memory_space=pl.ANY),
            pl.BlockSpec((td, Wp), lambda i: (i, 0)),
        ],
        out_specs=pl.BlockSpec(memory_space=pl.ANY),
        out_shape=jax.ShapeDtypeStruct((n_slots, Wp), U32),
        scratch_shapes=[
            pltpu.SMEM((TOP_K * td,), I32),
            pltpu.SemaphoreType.DMA,
            pltpu.SemaphoreType.DMA,
        ],
        compiler_params=pltpu.CompilerParams(
            dimension_semantics=("arbitrary",), vmem_limit_bytes=VMEM_LIMIT, has_side_effects=True),
        name="dispatch",
    )(slots_tiles, h1p)


def _experts_kernel(be_ref, nv_ref, xs_ref, wg_ref, wu_ref, wd_ref, ys_ref, wg_s, wu_s, wd_s):
    i = pl.program_id(0)

    @pl.when(i < nv_ref[0])
    def _():
        prev = be_ref[jnp.maximum(i - 1, 0)]

        @pl.when((i == 0) | (be_ref[i] != prev))
        def _():
            wg_s[...] = wg_ref[0].astype(BF16)
            wu_s[...] = wu_ref[0].astype(BF16)
            wd_s[...] = wd_ref[0].astype(BF16)

        lo, hi = _unpack_bf16_pairs(xs_ref[...])
        x = jnp.concatenate([lo.astype(BF16), hi.astype(BF16)], axis=1)
        g = jnp.dot(x, wg_s[...], preferred_element_type=F32)
        u = jnp.dot(x, wu_s[...], preferred_element_type=F32)
        hmid = (g * jax.nn.sigmoid(g) * u).astype(BF16)
        y = jnp.dot(hmid, wd_s[...], preferred_element_type=F32)
        ys_ref[...] = _pack_bf16_pairs(y)


def _experts(blk_exp, n_valid, xs, wg, wu, wd):
    n_slots, Wp = xs.shape
    E, D, F = wg.shape
    nb = n_slots // BLK_MOE

    def xmap(i, be, nv):
        return (jnp.minimum(i, nv[0] - 1), 0)

    def wmap(i, be, nv):
        return (be[i], 0, 0)

    grid_spec = pltpu.PrefetchScalarGridSpec(
        num_scalar_prefetch=2,
        grid=(nb,),
        in_specs=[
            pl.BlockSpec((BLK_MOE, Wp), xmap),
            pl.BlockSpec((1, D, F), wmap),
            pl.BlockSpec((1, D, F), wmap),
            pl.BlockSpec((1, F, D), wmap),
        ],
        out_specs=pl.BlockSpec((BLK_MOE, Wp), xmap),
        scratch_shapes=[pltpu.VMEM((D, F), BF16), pltpu.VMEM((D, F), BF16), pltpu.VMEM((F, D), BF16)],
    )
    return pl.pallas_call(
        _experts_kernel,
        grid_spec=grid_spec,
        out_shape=jax.ShapeDtypeStruct((n_slots, Wp), U32),
        compiler_params=_cparams(1),
        name="experts",
    )(blk_exp, n_valid, xs, wg, wu, wd)


def _combine_kernel(slots_hbm, ys_hbm, gate_ref, h1_ref, swg_ref, swu_ref, swd_ref, lg_ref, lb_ref,
                    out_ref, slot_smem, buf, sem_s, sem_g):
    i = pl.program_id(0)
    tc = h1_ref.shape[0]
    cp = pltpu.make_async_copy(slots_hbm.at[i], slot_smem, sem_s)
    cp.start()
    cp.wait()

    def row_copy(t, k, s):
        return pltpu.make_async_copy(ys_hbm.at[pl.ds(s, 1)], buf.at[k, pl.ds(t, 1)], sem_g)

    def issue(t, carry):
        for k in range(TOP_K):
            row_copy(t, k, slot_smem[k * tc + t]).start()
        return carry

    lax.fori_loop(0, tc, issue, 0)

    h1 = h1_ref[...]
    xb = h1.astype(BF16)
    g = jnp.dot(xb, swg_ref[...], preferred_element_type=F32)
    u = jnp.dot(xb, swu_ref[...], preferred_element_type=F32)
    hmid = (g * jax.nn.sigmoid(g) * u).astype(BF16)
    shared = jnp.dot(hmid, swd_ref[...], preferred_element_type=F32)

    def drain(t, carry):
        for k in range(TOP_K):
            row_copy(t, k, slot_smem[k * tc + t]).wait()
        return carry

    lax.fori_loop(0, tc, drain, 0)

    gates = gate_ref[...]
    f_lo = jnp.zeros((tc, buf.shape[2]), F32)
    f_hi = jnp.zeros((tc, buf.shape[2]), F32)
    for k in range(TOP_K):
        lo, hi = _unpack_bf16_pairs(buf[k])
        gk = gates[:, k:k + 1]
        f_lo = f_lo + gk * lo
        f_hi = f_hi + gk * hi
    routed = jnp.concatenate([f_lo, f_hi], axis=1)
    out_ref[...] = _layer_norm(ALPHA * h1 + (routed + shared), lg_ref[...], lb_ref[...])


def _combine(slots_tiles, ys, gate_t, h1, swg, swu, swd, ln_g, ln_b):
    T, D = h1.shape
    Wp = ys.shape[1]
    tc = min(TD_MOE, T)
    row = lambda i: (i, 0)
    const = lambda i: (0, 0)
    res = lambda a: pl.BlockSpec(a.shape, const)
    return pl.pallas_call(
        _combine_kernel,
        grid=(T // tc,),
        in_specs=[
            pl.BlockSpec(memory_space=pl.ANY),
            pl.BlockSpec(memory_space=pl.ANY),
            pl.BlockSpec((tc, TOP_K), row),
            pl.BlockSpec((tc, D), row),
            res(swg), res(swu), res(swd),
            pl.BlockSpec((1, D), const),
            pl.BlockSpec((1, D), const),
        ],
        out_specs=pl.BlockSpec((tc, D), row),
        out_shape=jax.ShapeDtypeStruct((T, D), F32),
        scratch_shapes=[
            pltpu.SMEM((TOP_K * tc,), I32),
            pltpu.VMEM((TOP_K, tc, Wp), U32),
            pltpu.SemaphoreType.DMA,
            pltpu.SemaphoreType.DMA,
        ],
        compiler_params=_cparams(1),
        name="combine",
    )(slots_tiles, ys, gate_t, h1, swg, swu, swd, ln_g.reshape(1, D), ln_b.reshape(1, D))


def _moe_layer(h1, h1p, router_w, router_bias, wg, wu, wd, swg, swu, swd, ln_g, ln_b):
    T, D = h1.shape
    E = N_EXPERTS
    M = T * TOP_K
    eidx, gate, rank, cnt = _router(h1, router_w, router_bias)

    counts = cnt[:, 0].astype(I32)
    padded = (counts + BLK_MOE - 1) // BLK_MOE * BLK_MOE
    pad_end = jnp.cumsum(padded)
    pad_start = pad_end - padded
    nb = -(-(M + E * (BLK_MOE - 1)) // BLK_MOE)
    n_slots = nb * BLK_MOE
    slot = pad_start[eidx] + rank
    n_valid = (pad_end[-1] // BLK_MOE).astype(I32).reshape(1)
    blk_exp = jnp.minimum(
        jnp.searchsorted(pad_end, jnp.arange(nb, dtype=I32) * BLK_MOE, side="right"), E - 1).astype(I32)

    td = min(TD_MOE, T)
    slots_tiles = slot.reshape(TOP_K, T // td, td).transpose(1, 0, 2).reshape(T // td, TOP_K * td)

    xs = _dispatch(slots_tiles, h1p, n_slots)
    ys = _experts(blk_exp, n_valid, xs, wg, wu, wd)
    return _combine(slots_tiles, ys, gate.T, h1, swg.astype(BF16), swu.astype(BF16), swd.astype(BF16),
                    ln_g, ln_b)


def kernel(x, emb_ln_g, emb_ln_b, w_in, b_gate, lam_q1, lam_k1, lam_q2, lam_k2, subln_g, w_proj_a, na_rpb, w_proj_b, w_out, ln1_g, ln1_b, router_w, router_bias, exp_w_gate, exp_w_up, exp_w_down, sh_w_gate, sh_w_up, sh_w_down, ln2_g, ln2_b):
    B, S, D = x.shape
    T = B * S
    rows = S // GRID_W
    h = x.reshape(T, D)
    for l in range(DEPTH):
        lam_init = 0.8 - 0.6 * math.exp(-0.3 * l)
        outs = _inproj(h, emb_ln_g, emb_ln_b, w_in[l].astype(BF16), apply_ln=(l == 0))
        if l == 0:
            h, outs = outs[0], outs[1:]
        qa, ka, va, qn, kn, vn, ga, gb = outs
        lamv = jnp.stack([lam_q1[l], lam_k1[l], lam_q2[l], lam_k2[l]]).astype(F32)
        oa = _diffattn(qa.reshape(B, S, -1), ka.reshape(B, S, -1), va.reshape(B, S, -1),
                       lamv, subln_g[l], lam_init)
        ob = _natten(qn.reshape(B, S, -1), kn.reshape(B, S, -1), vn.reshape(B, S, -1),
                     _na_bias_table(na_rpb[l], rows))
        h1, h1p = _mixout(oa.reshape(T, -1), ob.reshape(T, -1), ga, gb, h, b_gate[l],
                          w_proj_a[l].astype(BF16), w_proj_b[l].astype(BF16), w_out[l].astype(BF16),
                          ln1_g[l], ln1_b[l])
        h = _moe_layer(h1, h1p, router_w[l], router_bias[l], exp_w_gate[l], exp_w_up[l], exp_w_down[l],
                       sh_w_gate[l], sh_w_up[l], sh_w_down[l], ln2_g[l], ln2_b[l])
    return h.reshape(B, S, D)
```

```python
import functools
import math

import numpy as np
import jax
import jax.numpy as jnp
from jax import lax
from jax.experimental import pallas as pl
from jax.experimental.pallas import tpu as pltpu
from jax.experimental.pallas import tpu_sc as plsc

F32 = jnp.float32
BF16 = jnp.bfloat16
U32 = jnp.uint32
I32 = jnp.int32

DA_HEADS = 8
DA_HEAD_DIM = 64
NA_HEADS = 16
NA_HEAD_DIM = 32
GRID_W = 64
WIN_R = 8
WIN_C = 16
N_EXPERTS = 256
TOP_K = 8
N_GROUPS = 8
TOPK_GROUPS = 4
ROUTE_SCALE = 2.5
DEPTH = 2
ALPHA = (2 * DEPTH) ** 0.25
LN_EPS = 1e-5
LOG2E = 1.4426950408889634

LANES = 128
NA_HEADS_PER_BLOCK = LANES // NA_HEAD_DIM
VMEM_LIMIT = 56 * 1024 * 1024
SC_CORES = 2
SC_SUBCORES = 16
SC_IDX_CHUNK = 128
SC_ROW_WIN = 64
SC_GATHER_PLANES = 4

TM_PROJ = 512
TQ_DA = 1024
TQ_SUB_DA = 128
NA_ROWS_PER_ITER = 4
TT_ROUTER = 512
TD_MOE = 256
BLK_MOE = 512
SUB_MOE = 256
NEG_BIG = -1e30


def _cparams(n_axes, flags=None):
    return pltpu.CompilerParams(
        dimension_semantics=("arbitrary",) * n_axes, vmem_limit_bytes=VMEM_LIMIT, flags=flags)


def _layer_norm(x, g, b):
    mu = jnp.mean(x, axis=-1, keepdims=True)
    xc = x - mu
    var = jnp.mean(xc * xc, axis=-1, keepdims=True)
    return xc * lax.rsqrt(var + LN_EPS) * g + b


def _pack_bf16_pairs(y):
    w = y.shape[1] // 2
    lo = lax.bitcast_convert_type(y[:, :w].astype(BF16).astype(F32), U32)
    hi = lax.bitcast_convert_type(y[:, w:].astype(BF16).astype(F32), U32)
    return (hi & jnp.uint32(0xFFFF0000)) | (lo >> 16)


def _unpack_bf16_pairs(u):
    lo = lax.bitcast_convert_type(u << 16, F32)
    hi = lax.bitcast_convert_type(u & jnp.uint32(0xFFFF0000), F32)
    return lo, hi


def _inproj_kernel(apply_ln, seg_widths, seg_scales, x_ref, g_ref, b_ref, w_ref, *out_refs):
    x = x_ref[...]
    if apply_ln:
        x = _layer_norm(x, g_ref[...], b_ref[...])
        out_refs[0][...] = x
        out_refs = out_refs[1:]
    xb = x.astype(BF16)
    off = 0
    for ref, width, scale in zip(out_refs, seg_widths, seg_scales):
        y = jnp.dot(xb, w_ref[:, off:off + width], preferred_element_type=F32)
        if scale != 1.0:
            y = y * scale
        ref[...] = y.astype(BF16)
        off += width


def _inproj(x, ln_g, ln_b, w_in_bf16, apply_ln):
    T, D = x.shape
    da_w = DA_HEADS * 2 * DA_HEAD_DIM
    na_w = NA_HEADS * NA_HEAD_DIM
    seg_widths = (da_w, da_w, da_w, na_w, na_w, na_w, D, D)
    seg_scales = (DA_HEAD_DIM ** -0.5 * LOG2E, 1.0, 1.0, NA_HEAD_DIM ** -0.5 * LOG2E, 1.0, 1.0, 1.0, 1.0)
    tm = min(TM_PROJ, T)
    n_cols = w_in_bf16.shape[1]
    row = lambda i: (i, 0)
    const = lambda i: (0, 0)
    out_shape = [jax.ShapeDtypeStruct((T, w), BF16) for w in seg_widths]
    out_specs = [pl.BlockSpec((tm, w), row) for w in seg_widths]
    if apply_ln:
        out_shape = [jax.ShapeDtypeStruct((T, D), F32)] + out_shape
        out_specs = [pl.BlockSpec((tm, D), row)] + out_specs
    return pl.pallas_call(
        functools.partial(_inproj_kernel, apply_ln, seg_widths, seg_scales),
        grid=(T // tm,),
        in_specs=[
            pl.BlockSpec((tm, D), row),
            pl.BlockSpec((1, D), const),
            pl.BlockSpec((1, D), const),
            pl.BlockSpec((D, n_cols), const, pipeline_mode=pl.Buffered(1)),
        ],
        out_specs=out_specs,
        out_shape=out_shape,
        compiler_params=_cparams(1),
        name="inproj",
    )(x, ln_g.reshape(1, D), ln_b.reshape(1, D), w_in_bf16)


def _diffattn_kernel(lam_init, slopes_ref, q_ref, k_ref, v_ref, lamv_ref, g_ref, o_ref, bias_ref):
    h = pl.program_id(0)
    qi = pl.program_id(1)
    b = pl.program_id(2)
    tq = q_ref.shape[1]
    S = k_ref.shape[1]
    d = DA_HEAD_DIM

    @pl.when(b == 0)
    def _():
        qpos = qi * tq + lax.broadcasted_iota(I32, (tq, S), 0)
        kpos = lax.broadcasted_iota(I32, (tq, S), 1)
        bias_ref[...] = jnp.abs(qpos - kpos).astype(F32) * (-slopes_ref[h])

    lv = lamv_ref[...]
    lam = (jnp.exp(jnp.sum(lv[0:1] * lv[1:2], axis=-1, keepdims=True))
           - jnp.exp(jnp.sum(lv[2:3] * lv[3:4], axis=-1, keepdims=True)) + lam_init)

    k = k_ref[0]
    v = v_ref[0]
    gain = g_ref[...] * (1.0 - lam_init)
    nt = (((1,), (1,)), ((), ()))
    ts = min(TQ_SUB_DA, tq)
    lane = lax.broadcasted_iota(I32, (ts, 2 * d), 1)

    def scores(j):
        q = q_ref[0, j * ts:(j + 1) * ts, :]
        nb = bias_ref[j * ts:(j + 1) * ts, :]
        zero = jnp.zeros_like(q)
        s1 = lax.dot_general(jnp.where(lane < d, q, zero), k, nt, preferred_element_type=F32) + nb
        s2 = lax.dot_general(jnp.where(lane >= d, q, zero), k, nt, preferred_element_type=F32) + nb
        return s1, s2

    def finish(j, s1, s2):
        e1 = jnp.exp2(s1 - jnp.max(s1, axis=-1, keepdims=True))
        e2 = jnp.exp2(s2 - jnp.max(s2, axis=-1, keepdims=True))
        r1 = 1.0 / jnp.sum(e1, axis=-1, keepdims=True)
        r2 = lam / jnp.sum(e2, axis=-1, keepdims=True)
        w = e1 * r1 - e2 * r2
        o = jnp.dot(w.astype(BF16), v, preferred_element_type=F32)
        ms = jnp.mean(o * o, axis=-1, keepdims=True)
        o = o * lax.rsqrt(ms + LN_EPS) * gain
        o_ref[0, j * ts:(j + 1) * ts, :] = o.astype(BF16)

    n_sub = tq // ts
    pending = scores(0)
    for j in range(n_sub):
        nxt = scores(j + 1) if j + 1 < n_sub else None
        finish(j, *pending)
        pending = nxt


def _diffattn(qa, ka, va, lamv, subln_g, lam_init):
    B, S, W = qa.shape
    hw = 2 * DA_HEAD_DIM
    tq = min(TQ_DA, S)
    slopes = jnp.asarray(2.0 ** (-8.0 * np.arange(1, DA_HEADS + 1) / DA_HEADS) * LOG2E, F32)
    grid_spec = pltpu.PrefetchScalarGridSpec(
        num_scalar_prefetch=1,
        grid=(DA_HEADS, S // tq, B),
        in_specs=[
            pl.BlockSpec((1, tq, hw), lambda h, qi, b, sl: (b, qi, h)),
            pl.BlockSpec((1, S, hw), lambda h, qi, b, sl: (b, 0, h)),
            pl.BlockSpec((1, S, hw), lambda h, qi, b, sl: (b, 0, h)),
            pl.BlockSpec((4, DA_HEAD_DIM), lambda h, qi, b, sl: (0, 0)),
            pl.BlockSpec((1, hw), lambda h, qi, b, sl: (0, 0)),
        ],
        out_specs=pl.BlockSpec((1, tq, hw), lambda h, qi, b, sl: (b, qi, h)),
        scratch_shapes=[pltpu.VMEM((tq, S), F32)],
    )
    return pl.pallas_call(
        functools.partial(_diffattn_kernel, lam_init),
        grid_spec=grid_spec,
        out_shape=jax.ShapeDtypeStruct((B, S, W), BF16),
        compiler_params=_cparams(3),
        name="diffattn",
    )(slopes, qa, ka, va, lamv, subln_g.reshape(1, hw))


def _na_bias_table(rpb, rows):
    kh = min(WIN_R, rows)
    qcol = np.arange(GRID_W)
    kcol = np.arange(GRID_W)
    cs = np.clip(qcol - WIN_C // 2, 0, GRID_W - WIN_C)
    col_mask = (kcol[None, :] >= cs[:, None]) & (kcol[None, :] < cs[:, None] + WIN_C)
    col_off = np.clip(kcol[None, :] - qcol[:, None] + WIN_C - 1, 0, 2 * WIN_C - 2)
    v = np.arange(kh)
    j = np.arange(kh)
    row_idx = np.clip(j[None, :] - v[:, None] + WIN_R - 1, 0, 2 * WIN_R - 2)
    t = (rpb * LOG2E)[:, row_idx]
    t = t[:, :, :, col_off]
    t = jnp.where(jnp.asarray(col_mask)[None, None, None], t, NEG_BIG)
    t = t.transpose(1, 0, 3, 2, 4)
    nblk = NA_HEADS // NA_HEADS_PER_BLOCK
    return t.reshape(kh, nblk, NA_HEADS_PER_BLOCK * GRID_W, kh * GRID_W)


def _natten_kernel(rows, kh, q_ref, k_ref, v_ref, bias_ref, o_ref):
    hid = lax.broadcasted_iota(I32, (GRID_W, LANES), 1) // NA_HEAD_DIM

    def scores(r):
        rs = jnp.clip(r - WIN_R // 2, 0, rows - kh)
        q0 = pl.multiple_of(r * GRID_W, GRID_W)
        k0 = pl.multiple_of(rs * GRID_W, GRID_W)
        qr = q_ref[0, pl.ds(q0, GRID_W), :]
        zero = jnp.zeros_like(qr)
        qq = jnp.concatenate(
            [jnp.where(hid == hh, qr, zero) for hh in range(NA_HEADS_PER_BLOCK)], axis=0)
        kb = k_ref[0, pl.ds(k0, kh * GRID_W), :]
        s = lax.dot_general(qq, kb, (((1,), (1,)), ((), ())), preferred_element_type=F32)
        return s + bias_ref[r - rs, 0]

    def weights(s):
        e = jnp.exp2(s - jnp.max(s, axis=-1, keepdims=True))
        return e.astype(BF16), 1.0 / jnp.sum(e, axis=-1, keepdims=True)

    def finish(r, e, rl):
        rs = jnp.clip(r - WIN_R // 2, 0, rows - kh)
        q0 = pl.multiple_of(r * GRID_W, GRID_W)
        k0 = pl.multiple_of(rs * GRID_W, GRID_W)
        vb = v_ref[0, pl.ds(k0, kh * GRID_W), :]
        oo = jnp.dot(e, vb, preferred_element_type=F32) * rl
        o = jnp.zeros((GRID_W, LANES), F32)
        for hh in range(NA_HEADS_PER_BLOCK):
            o = o + jnp.where(hid == hh, oo[hh * GRID_W:(hh + 1) * GRID_W], 0.0)
        o_ref[0, pl.ds(q0, GRID_W), :] = o.astype(BF16)

    def row_group(g, carry):
        rr = [g * NA_ROWS_PER_ITER + i for i in range(NA_ROWS_PER_ITER)]
        ss = [scores(r) for r in rr]
        ws = [weights(s) for s in ss]
        for r, (e, rl) in zip(rr, ws):
            finish(r, e, rl)
        return carry

    lax.fori_loop(0, rows // NA_ROWS_PER_ITER, row_group, 0)


def _natten(qn, kn, vn, bias_tab):
    B, S, W = qn.shape
    rows = S // GRID_W
    kh = min(WIN_R, rows)
    nblk = W // LANES
    blk = lambda g, b: (b, 0, g)
    return pl.pallas_call(
        functools.partial(_natten_kernel, rows, kh),
        grid=(nblk, B),
        in_specs=[
            pl.BlockSpec((1, S, LANES), blk),
            pl.BlockSpec((1, S, LANES), blk),
            pl.BlockSpec((1, S, LANES), blk),
            pl.BlockSpec((kh, 1, NA_HEADS_PER_BLOCK * GRID_W, kh * GRID_W), lambda g, b: (0, g, 0, 0)),
        ],
        out_specs=pl.BlockSpec((1, S, LANES), blk),
        out_shape=jax.ShapeDtypeStruct((B, S, W), BF16),
        compiler_params=_cparams(2),
        name="natten",
    )(qn, kn, vn, bias_tab)


def _mixout_kernel(oa_ref, ob_ref, ga_ref, gb_ref, h_ref, bg_ref, wa_ref, wb_ref, wo_ref,
                   lg_ref, lb_ref, h1_ref, h1p_ref):
    ya = jnp.dot(oa_ref[...], wa_ref[...], preferred_element_type=F32)
    yb = jnp.dot(ob_ref[...], wb_ref[...], preferred_element_type=F32)
    g_a = jax.nn.sigmoid(ga_ref[...].astype(F32) + bg_ref[0:1, :])
    g_b = jax.nn.sigmoid(gb_ref[...].astype(F32) + bg_ref[1:2, :])
    z = (g_a * ya + g_b * yb).astype(BF16)
    m = jnp.dot(z, wo_ref[...], preferred_element_type=F32)
    y = _layer_norm(ALPHA * h_ref[...] + m, lg_ref[...], lb_ref[...])
    h1_ref[...] = y
    h1p_ref[...] = _pack_bf16_pairs(y)


def _mixout(oa, ob, ga, gb, h, b_gate, wa, wb, wo, ln_g, ln_b):
    T, D = h.shape
    tm = min(TM_PROJ, T)
    row = lambda i: (i, 0)
    const = lambda i: (0, 0)
    res = lambda a: pl.BlockSpec(a.shape, const, pipeline_mode=pl.Buffered(1))
    return pl.pallas_call(
        _mixout_kernel,
        grid=(T // tm,),
        in_specs=[
            pl.BlockSpec((tm, oa.shape[1]), row),
            pl.BlockSpec((tm, ob.shape[1]), row),
            pl.BlockSpec((tm, D), row),
            pl.BlockSpec((tm, D), row),
            pl.BlockSpec((tm, D), row),
            pl.BlockSpec((2, D), const),
            res(wa), res(wb), res(wo),
            pl.BlockSpec((1, D), const),
            pl.BlockSpec((1, D), const),
        ],
        out_specs=[pl.BlockSpec((tm, D), row), pl.BlockSpec((tm, D // 2), row)],
        out_shape=[jax.ShapeDtypeStruct((T, D), F32), jax.ShapeDtypeStruct((T, D // 2), U32)],
        compiler_params=_cparams(1),
        name="mixout",
    )(oa, ob, ga, gb, h, b_gate, wa, wb, wo, ln_g.reshape(1, D), ln_b.reshape(1, D))


def _router_kernel(h_ref, rwt_ref, rb_ref, tri_ref, eidx_ref, gate_ref, rank_ref, cnt_ref, carry_ref):
    i = pl.program_id(0)
    tt = h_ref.shape[0]
    E, G = N_EXPERTS, N_GROUPS
    P = E // G
    neg = -jnp.inf

    @pl.when(i == 0)
    def _():
        carry_ref[...] = jnp.zeros_like(carry_ref)

    hb = h_ref[...].astype(BF16)
    logits = lax.dot_general(rwt_ref[...], hb, (((1,), (1,)), ((), ())), preferred_element_type=F32)
    scores = jax.nn.sigmoid(logits)
    biased = scores + rb_ref[...]
    b3 = biased.reshape(G, P, tt)
    s3 = scores.reshape(G, P, tt)
    pi = lax.broadcasted_iota(I32, (G, P, tt), 1)
    ei = lax.broadcasted_iota(I32, (G, P, tt), 0) * P + pi

    m1 = jnp.max(b3, axis=1, keepdims=True)
    i1 = jnp.min(jnp.where(b3 == m1, pi, P), axis=1, keepdims=True)
    m2 = jnp.max(jnp.where(pi == i1, neg, b3), axis=1, keepdims=True)
    grp = m1 + m2
    gi = lax.broadcasted_iota(I32, (G, 1, tt), 0)
    gsel = jnp.zeros((G, 1, tt), F32)
    for _ in range(TOPK_GROUPS):
        gm = jnp.max(grp, axis=0, keepdims=True)
        gidx = jnp.min(jnp.where(grp == gm, gi, G), axis=0, keepdims=True)
        hit = gi == gidx
        gsel = jnp.where(hit, 1.0, gsel)
        grp = jnp.where(hit, neg, grp)

    cand = jnp.where(gsel > 0.0, b3, neg)
    sel = jnp.zeros((G, P, tt), F32)
    eids, gates = [], []
    for _ in range(TOP_K):
        mk = jnp.max(cand, axis=(0, 1), keepdims=True)
        ik = jnp.min(jnp.where(cand == mk, ei, E), axis=(0, 1), keepdims=True)
        hit = ei == ik
        gates.append(jnp.sum(jnp.where(hit, s3, 0.0), axis=(0, 1), keepdims=True))
        eids.append(ik)
        cand = jnp.where(hit, neg, cand)
        sel = jnp.where(hit, 1.0, sel)

    gsum = gates[0]
    for gk in gates[1:]:
        gsum = gsum + gk
    gscale = ROUTE_SCALE / gsum

    sel2 = sel.reshape(E, tt)
    prefix = jnp.dot(sel2.astype(BF16), tri_ref[...], preferred_element_type=F32)
    base3 = (prefix + carry_ref[...]).reshape(G, P, tt)
    ranks = [jnp.sum(jnp.where(ei == ik, base3, 0.0), axis=(0, 1), keepdims=True) for ik in eids]
    carry_ref[...] = carry_ref[...] + jnp.sum(sel2, axis=1, keepdims=True)
    cnt_ref[...] = carry_ref[...]

    eidx_ref[...] = jnp.concatenate([x.reshape(1, tt) for x in eids], axis=0)
    gate_ref[...] = jnp.concatenate([(g * gscale).reshape(1, tt) for g in gates], axis=0)
    rank_ref[...] = jnp.concatenate([x.reshape(1, tt) for x in ranks], axis=0).astype(I32)


def _router(h1, router_w, router_bias):
    T, D = h1.shape
    E = N_EXPERTS
    tt = min(TT_ROUTER, T)
    rwt = router_w.T.astype(BF16)
    tri = jnp.asarray(np.triu(np.ones((tt, tt), np.float32), k=1), BF16)
    const = lambda i: (0, 0)
    col = lambda i: (0, i)
    return pl.pallas_call(
        _router_kernel,
        grid=(T // tt,),
        in_specs=[
            pl.BlockSpec((tt, D), lambda i: (i, 0)),
            pl.BlockSpec((E, D), const),
            pl.BlockSpec((E, 1), const),
            pl.BlockSpec((tt, tt), const),
        ],
        out_specs=[
            pl.BlockSpec((TOP_K, tt), col),
            pl.BlockSpec((TOP_K, tt), col),
            pl.BlockSpec((TOP_K, tt), col),
            pl.BlockSpec((E, 1), const),
        ],
        out_shape=[
            jax.ShapeDtypeStruct((TOP_K, T), I32),
            jax.ShapeDtypeStruct((TOP_K, T), F32),
            jax.ShapeDtypeStruct((TOP_K, T), I32),
            jax.ShapeDtypeStruct((E, 1), F32),
        ],
        scratch_shapes=[pltpu.VMEM((E, 1), F32)],
        compiler_params=_cparams(1),
        name="router",
    )(h1, rwt, router_bias.reshape(E, 1).astype(F32), tri)


def _slots_kernel(pstart_ref, eidx_ref, rank_ref, slot_ref):
    eidx = eidx_ref[...]

    def add_expert(e, acc):
        return acc + jnp.where(eidx == e, pstart_ref[e], 0)

    slot_ref[...] = lax.fori_loop(0, N_EXPERTS, add_expert, rank_ref[...], unroll=8)


def _slots(pad_start, eidx, rank):
    K, T = eidx.shape
    tt = min(2048, T)
    col = lambda i, ps: (0, i)
    grid_spec = pltpu.PrefetchScalarGridSpec(
        num_scalar_prefetch=1,
        grid=(T // tt,),
        in_specs=[pl.BlockSpec((K, tt), col), pl.BlockSpec((K, tt), col)],
        out_specs=pl.BlockSpec((K, tt), col),
    )
    return pl.pallas_call(
        _slots_kernel,
        grid_spec=grid_spec,
        out_shape=jax.ShapeDtypeStruct((K, T), I32),
        compiler_params=_cparams(1),
        name="slots",
    )(pad_start, eidx, rank)


def _dispatch(slot, h1p, n_slots):
    T, Wp = h1p.shape
    n_workers = SC_CORES * SC_SUBCORES
    per = T // n_workers
    assert per % SC_IDX_CHUNK == 0, (T, n_workers)
    mesh = plsc.VectorSubcoreMesh(core_axis_name="core", subcore_axis_name="subcore")

    @pl.kernel(
        out_type=jax.ShapeDtypeStruct((n_slots, Wp), U32), mesh=mesh,
        scratch_types=[pltpu.VMEM((TOP_K, SC_IDX_CHUNK), I32), pltpu.VMEM((SC_ROW_WIN, Wp), U32),
                       pltpu.SemaphoreType.DMA])
    def dispatch(x_hbm, s_hbm, o_hbm, idx_v, x_v, sem):
        base = (lax.axis_index("core") * SC_SUBCORES + lax.axis_index("subcore")) * per

        @pl.loop(0, per // SC_IDX_CHUNK)
        def _(c):
            t0 = base + c * SC_IDX_CHUNK
            pltpu.sync_copy(s_hbm.at[:, pl.ds(t0, SC_IDX_CHUNK)], idx_v)
            for j in range(SC_IDX_CHUNK // SC_ROW_WIN):
                pltpu.sync_copy(x_hbm.at[pl.ds(t0 + j * SC_ROW_WIN, SC_ROW_WIN)], x_v)
                copies = [
                    pltpu.async_copy(x_v, o_hbm.at[idx_v.at[k, pl.ds(j * SC_ROW_WIN, SC_ROW_WIN)]], sem)
                    for k in range(TOP_K)]
                for cp in copies:
                    cp.wait()

    return dispatch(h1p, slot)


def _experts_kernel(be_ref, nv_ref, xs_ref, wg_ref, wu_ref, wd_ref, ys_ref, wg_s, wu_s, wd_s):
    i = pl.program_id(0)

    @pl.when(i < nv_ref[0])
    def _():
        prev = be_ref[jnp.maximum(i - 1, 0)]

        @pl.when((i == 0) | (be_ref[i] != prev))
        def _():
            wg_s[...] = wg_ref[0].astype(BF16)
            wu_s[...] = wu_ref[0].astype(BF16)
            wd_s[...] = wd_ref[0].astype(BF16)

        sub = min(SUB_MOE, xs_ref.shape[0])

        def up(j):
            lo, hi = _unpack_bf16_pairs(xs_ref[j * sub:(j + 1) * sub, :])
            x = jnp.concatenate([lo.astype(BF16), hi.astype(BF16)], axis=1)
            return (jnp.dot(x, wg_s[...], preferred_element_type=F32),
                    jnp.dot(x, wu_s[...], preferred_element_type=F32))

        def down(j, g, u):
            hmid = (g * jax.nn.sigmoid(g) * u).astype(BF16)
            y = jnp.dot(hmid, wd_s[...], preferred_element_type=F32)
            ys_ref[j * sub:(j + 1) * sub, :] = _pack_bf16_pairs(y)

        n_sub = xs_ref.shape[0] // sub
        pending = up(0)
        for j in range(n_sub):
            nxt = up(j + 1) if j + 1 < n_sub else None
            down(j, *pending)
            pending = nxt


def _experts(layer, blk_exp, n_valid, xs, wg, wu, wd):
    n_slots, Wp = xs.shape
    _, E, D, F = wg.shape
    nb = n_slots // BLK_MOE

    def xmap(i, be, nv):
        return (jnp.minimum(i, nv[0] - 1), 0)

    def wmap(i, be, nv):
        return (layer, be[i], 0, 0)

    grid_spec = pltpu.PrefetchScalarGridSpec(
        num_scalar_prefetch=2,
        grid=(nb,),
        in_specs=[
            pl.BlockSpec((BLK_MOE, Wp), xmap),
            pl.BlockSpec((None, 1, D, F), wmap),
            pl.BlockSpec((None, 1, D, F), wmap),
            pl.BlockSpec((None, 1, F, D), wmap),
        ],
        out_specs=pl.BlockSpec((BLK_MOE, Wp), xmap),
        scratch_shapes=[pltpu.VMEM((D, F), BF16), pltpu.VMEM((D, F), BF16), pltpu.VMEM((F, D), BF16)],
    )
    return pl.pallas_call(
        _experts_kernel,
        grid_spec=grid_spec,
        out_shape=jax.ShapeDtypeStruct((n_slots, Wp), U32),
        compiler_params=_cparams(1),
        name="experts",
    )(blk_exp, n_valid, xs, wg, wu, wd)


def _gather_rows(slot, ys):
    K, T = slot.shape
    Wp = ys.shape[1]
    n_workers = SC_CORES * SC_SUBCORES
    per = T // n_workers
    assert per % SC_IDX_CHUNK == 0 and K % SC_GATHER_PLANES == 0, (T, K)
    win = SC_ROW_WIN // 2
    mesh = plsc.VectorSubcoreMesh(core_axis_name="core", subcore_axis_name="subcore")

    @pl.kernel(
        out_type=jax.ShapeDtypeStruct((K, T, Wp), U32), mesh=mesh,
        scratch_types=[pltpu.VMEM((K, SC_IDX_CHUNK), I32), pltpu.VMEM((SC_GATHER_PLANES, win, Wp), U32),
                       pltpu.SemaphoreType.DMA, pltpu.SemaphoreType.DMA])
    def gather(y_hbm, s_hbm, g_hbm, idx_v, buf, sem_g, sem_w):
        base = (lax.axis_index("core") * SC_SUBCORES + lax.axis_index("subcore")) * per

        @pl.loop(0, per // SC_IDX_CHUNK)
        def _(c):
            t0 = base + c * SC_IDX_CHUNK
            pltpu.sync_copy(s_hbm.at[:, pl.ds(t0, SC_IDX_CHUNK)], idx_v)
            for j in range(SC_IDX_CHUNK // win):
                for k0 in range(0, K, SC_GATHER_PLANES):
                    reads = [
                        pltpu.async_copy(y_hbm.at[idx_v.at[k0 + i, pl.ds(j * win, win)]], buf.at[i], sem_g)
                        for i in range(SC_GATHER_PLANES)]
                    for cp in reads:
                        cp.wait()
                    writes = [
                        pltpu.async_copy(buf.at[i], g_hbm.at[k0 + i, pl.ds(t0 + j * win, win)], sem_w)
                        for i in range(SC_GATHER_PLANES)]
                    for cp in writes:
                        cp.wait()

    return gather(ys, slot)


def _combine_kernel(yg_ref, gate_ref, h1_ref, swg_ref, swu_ref, swd_ref, lg_ref, lb_ref, out_ref):
    h1 = h1_ref[...]
    xb = h1.astype(BF16)
    g = jnp.dot(xb, swg_ref[...], preferred_element_type=F32)
    u = jnp.dot(xb, swu_ref[...], preferred_element_type=F32)
    hmid = (g * jax.nn.sigmoid(g) * u).astype(BF16)
    shared = jnp.dot(hmid, swd_ref[...], preferred_element_type=F32)

    gates = gate_ref[...]
    tc, wp = yg_ref.shape[1], yg_ref.shape[2]
    f_lo = jnp.zeros((tc, wp), F32)
    f_hi = jnp.zeros((tc, wp), F32)
    for k in range(TOP_K):
        lo, hi = _unpack_bf16_pairs(yg_ref[k])
        gk = gates[:, k:k + 1]
        f_lo = f_lo + gk * lo
        f_hi = f_hi + gk * hi
    routed = jnp.concatenate([f_lo, f_hi], axis=1)
    out_ref[...] = _layer_norm(ALPHA * h1 + (routed + shared), lg_ref[...], lb_ref[...])


def _combine(yg, gate_t, h1, swg, swu, swd, ln_g, ln_b):
    T, D = h1.shape
    K, _, Wp = yg.shape
    tc = min(TD_MOE, T)
    row = lambda i: (i, 0)
    const = lambda i: (0, 0)
    res = lambda a: pl.BlockSpec(a.shape, const)
    return pl.pallas_call(
        _combine_kernel,
        grid=(T // tc,),
        in_specs=[
            pl.BlockSpec((K, tc, Wp), lambda i: (0, i, 0)),
            pl.BlockSpec((tc, TOP_K), row),
            pl.BlockSpec((tc, D), row),
            res(swg), res(swu), res(swd),
            pl.BlockSpec((1, D), const),
            pl.BlockSpec((1, D), const),
        ],
        out_specs=pl.BlockSpec((tc, D), row),
        out_shape=jax.ShapeDtypeStruct((T, D), F32),
        compiler_params=_cparams(1),
        name="combine",
    )(yg, gate_t, h1, swg, swu, swd, ln_g.reshape(1, D), ln_b.reshape(1, D))


def _moe_layer(layer, h1, h1p, router_w, router_bias, wg, wu, wd, swg, swu, swd, ln_g, ln_b):
    T, D = h1.shape
    E = N_EXPERTS
    M = T * TOP_K
    eidx, gate, rank, cnt = _router(h1, router_w, router_bias)

    counts = cnt[:, 0].astype(I32)
    padded = (counts + BLK_MOE - 1) // BLK_MOE * BLK_MOE
    pad_end = jnp.cumsum(padded)
    pad_start = pad_end - padded
    nb = -(-(M + E * (BLK_MOE - 1)) // BLK_MOE)
    n_slots = nb * BLK_MOE
    slot = _slots(pad_start.astype(I32), eidx, rank)
    n_valid = (pad_end[-1] // BLK_MOE).astype(I32).reshape(1)
    blk_exp = jnp.minimum(
        jnp.searchsorted(pad_end, jnp.arange(nb, dtype=I32) * BLK_MOE, side="right"), E - 1).astype(I32)

    xs = _dispatch(slot, h1p, n_slots)
    ys = _experts(layer, blk_exp, n_valid, xs, wg, wu, wd)
    yg = _gather_rows(slot, ys)
    return _combine(yg, gate.T, h1, swg.astype(BF16), swu.astype(BF16), swd.astype(BF16), ln_g, ln_b)


def kernel(x, emb_ln_g, emb_ln_b, w_in, b_gate, lam_q1, lam_k1, lam_q2, lam_k2, subln_g, w_proj_a, na_rpb, w_proj_b, w_out, ln1_g, ln1_b, router_w, router_bias, exp_w_gate, exp_w_up, exp_w_down, sh_w_gate, sh_w_up, sh_w_down, ln2_g, ln2_b):
    B, S, D = x.shape
    T = B * S
    rows = S // GRID_W
    h = x.reshape(T, D)
    for l in range(DEPTH):
        lam_init = 0.8 - 0.6 * math.exp(-0.3 * l)
        outs = _inproj(h, emb_ln_g, emb_ln_b, w_in[l].astype(BF16), apply_ln=(l == 0))
        if l == 0:
            h, outs = outs[0], outs[1:]
        qa, ka, va, qn, kn, vn, ga, gb = outs
        lamv = jnp.stack([lam_q1[l], lam_k1[l], lam_q2[l], lam_k2[l]]).astype(F32)
        oa = _diffattn(qa.reshape(B, S, -1), ka.reshape(B, S, -1), va.reshape(B, S, -1),
                       lamv, subln_g[l], lam_init)
        ob = _natten(qn.reshape(B, S, -1), kn.reshape(B, S, -1), vn.reshape(B, S, -1),
                     _na_bias_table(na_rpb[l], rows))
        h1, h1p = _mixout(oa.reshape(T, -1), ob.reshape(T, -1), ga, gb, h, b_gate[l],
                          w_proj_a[l].astype(BF16), w_proj_b[l].astype(BF16), w_out[l].astype(BF16),
                          ln1_g[l], ln1_b[l])
        h = _moe_layer(l, h1, h1p, router_w[l], router_bias[l], exp_w_gate, exp_w_up, exp_w_down,
                       sh_w_gate[l], sh_w_up[l], sh_w_down[l], ln2_g[l], ln2_b[l])
    return h.reshape(B, S, D)
```

```python
import functools
import math

import numpy as np
import jax
import jax.numpy as jnp
from jax import lax
from jax.experimental import pallas as pl
from jax.experimental.pallas import tpu as pltpu
from jax.experimental.pallas import tpu_sc as plsc

F32 = jnp.float32
BF16 = jnp.bfloat16
U32 = jnp.uint32
I32 = jnp.int32

DA_HEADS = 8
DA_HEAD_DIM = 64
NA_HEADS = 16
NA_HEAD_DIM = 32
GRID_W = 64
WIN_R = 8
WIN_C = 16
N_EXPERTS = 256
TOP_K = 8
N_GROUPS = 8
TOPK_GROUPS = 4
ROUTE_SCALE = 2.5
DEPTH = 2
ALPHA = (2 * DEPTH) ** 0.25
LN_EPS = 1e-5
LOG2E = 1.4426950408889634

LANES = 128
NA_HEADS_PER_BLOCK = LANES // NA_HEAD_DIM
VMEM_LIMIT = 56 * 1024 * 1024
SC_CORES = 2
SC_SUBCORES = 16
SC_IDX_CHUNK = 128
SC_ROW_WIN = 64
SC_GATHER_PLANES = 4

TM_PROJ = 512
TQ_DA = 1024
TQ_SUB_DA = 128
NA_ROWS_PER_ITER = 4
TT_ROUTER = 512
TD_MOE = 256
BLK_MOE = 512
SUB_MOE = 256
NEG_BIG = -1e30


def _cparams(n_axes, flags=None):
    return pltpu.CompilerParams(
        dimension_semantics=("arbitrary",) * n_axes, vmem_limit_bytes=VMEM_LIMIT, flags=flags)


def _layer_norm(x, g, b):
    mu = jnp.mean(x, axis=-1, keepdims=True)
    xc = x - mu
    var = jnp.mean(xc * xc, axis=-1, keepdims=True)
    return xc * lax.rsqrt(var + LN_EPS) * g + b


def _pack_bf16_pairs(y):
    w = y.shape[1] // 2
    lo = lax.bitcast_convert_type(y[:, :w].astype(BF16).astype(F32), U32)
    hi = lax.bitcast_convert_type(y[:, w:].astype(BF16).astype(F32), U32)
    return (hi & jnp.uint32(0xFFFF0000)) | (lo >> 16)


def _unpack_bf16_pairs(u):
    lo = lax.bitcast_convert_type(u << 16, F32)
    hi = lax.bitcast_convert_type(u & jnp.uint32(0xFFFF0000), F32)
    return lo, hi


def _inproj_kernel(apply_ln, seg_widths, seg_scales, x_ref, g_ref, b_ref, w_ref, *out_refs):
    x = x_ref[...]
    if apply_ln:
        x = _layer_norm(x, g_ref[...], b_ref[...])
        out_refs[0][...] = x
        out_refs = out_refs[1:]
    xb = x.astype(BF16)
    off = 0
    for ref, width, scale in zip(out_refs, seg_widths, seg_scales):
        y = jnp.dot(xb, w_ref[:, off:off + width], preferred_element_type=F32)
        if scale != 1.0:
            y = y * scale
        ref[...] = y.astype(BF16)
        off += width


def _inproj(x, ln_g, ln_b, w_in_bf16, apply_ln):
    T, D = x.shape
    da_w = DA_HEADS * 2 * DA_HEAD_DIM
    na_w = NA_HEADS * NA_HEAD_DIM
    seg_widths = (da_w, da_w, da_w, na_w, na_w, na_w, D, D)
    seg_scales = (DA_HEAD_DIM ** -0.5 * LOG2E, 1.0, 1.0, NA_HEAD_DIM ** -0.5 * LOG2E, 1.0, 1.0, 1.0, 1.0)
    tm = min(TM_PROJ, T)
    n_cols = w_in_bf16.shape[1]
    row = lambda i: (i, 0)
    const = lambda i: (0, 0)
    out_shape = [jax.ShapeDtypeStruct((T, w), BF16) for w in seg_widths]
    out_specs = [pl.BlockSpec((tm, w), row) for w in seg_widths]
    if apply_ln:
        out_shape = [jax.ShapeDtypeStruct((T, D), F32)] + out_shape
        out_specs = [pl.BlockSpec((tm, D), row)] + out_specs
    return pl.pallas_call(
        functools.partial(_inproj_kernel, apply_ln, seg_widths, seg_scales),
        grid=(T // tm,),
        in_specs=[
            pl.BlockSpec((tm, D), row),
            pl.BlockSpec((1, D), const),
            pl.BlockSpec((1, D), const),
            pl.BlockSpec((D, n_cols), const, pipeline_mode=pl.Buffered(1)),
        ],
        out_specs=out_specs,
        out_shape=out_shape,
        compiler_params=_cparams(1),
        name="inproj",
    )(x, ln_g.reshape(1, D), ln_b.reshape(1, D), w_in_bf16)


def _diffattn_kernel(lam_init, slopes_ref, q_ref, k_ref, v_ref, lamv_ref, g_ref, o_ref, bias_ref):
    h = pl.program_id(0)
    qi = pl.program_id(1)
    b = pl.program_id(2)
    tq = q_ref.shape[1]
    S = k_ref.shape[1]
    d = DA_HEAD_DIM

    @pl.when(b == 0)
    def _():
        qpos = qi * tq + lax.broadcasted_iota(I32, (tq, S), 0)
        kpos = lax.broadcasted_iota(I32, (tq, S), 1)
        bias_ref[...] = jnp.abs(qpos - kpos).astype(F32) * (-slopes_ref[h])

    lv = lamv_ref[...]
    lam = (jnp.exp(jnp.sum(lv[0:1] * lv[1:2], axis=-1, keepdims=True))
           - jnp.exp(jnp.sum(lv[2:3] * lv[3:4], axis=-1, keepdims=True)) + lam_init)

    k = k_ref[0]
    v1 = jnp.concatenate([v_ref[0], jnp.ones((S, 2 * d), BF16)], axis=1)
    gain = g_ref[...] * (1.0 - lam_init)
    nt = (((1,), (1,)), ((), ()))
    ts = min(TQ_SUB_DA, tq)
    lane = lax.broadcasted_iota(I32, (ts, 2 * d), 1)

    def scores(j):
        q = q_ref[0, j * ts:(j + 1) * ts, :]
        nb = bias_ref[j * ts:(j + 1) * ts, :]
        zero = jnp.zeros_like(q)
        s1 = lax.dot_general(jnp.where(lane < d, q, zero), k, nt, preferred_element_type=F32) + nb
        s2 = lax.dot_general(jnp.where(lane >= d, q, zero), k, nt, preferred_element_type=F32) + nb
        return s1, s2

    def weights(s1, s2):
        return (jnp.exp2((s1 - jnp.max(s1, axis=-1, keepdims=True)).astype(BF16)),
                jnp.exp2((s2 - jnp.max(s2, axis=-1, keepdims=True)).astype(BF16)))

    def finish(j, e1, e2):
        p1 = jnp.dot(e1, v1, preferred_element_type=F32)
        p2 = jnp.dot(e2, v1, preferred_element_type=F32)
        o = p1[:, :2 * d] / p1[:, 2 * d:] - lam * (p2[:, :2 * d] / p2[:, 2 * d:])
        ms = jnp.mean(o * o, axis=-1, keepdims=True)
        o = o * lax.rsqrt(ms + LN_EPS) * gain
        o_ref[0, j * ts:(j + 1) * ts, :] = o.astype(BF16)

    n_sub = tq // ts
    s_next = scores(0)
    e_cur = weights(*s_next)
    s_next = scores(1) if n_sub > 1 else None
    for j in range(n_sub):
        s_after = scores(j + 2) if j + 2 < n_sub else None
        finish(j, *e_cur)
        if s_next is not None:
            e_cur = weights(*s_next)
        s_next = s_after


def _diffattn(qa, ka, va, lamv, subln_g, lam_init):
    B, S, W = qa.shape
    hw = 2 * DA_HEAD_DIM
    tq = min(TQ_DA, S)
    slopes = jnp.asarray(2.0 ** (-8.0 * np.arange(1, DA_HEADS + 1) / DA_HEADS) * LOG2E, F32)
    grid_spec = pltpu.PrefetchScalarGridSpec(
        num_scalar_prefetch=1,
        grid=(DA_HEADS, S // tq, B),
        in_specs=[
            pl.BlockSpec((1, tq, hw), lambda h, qi, b, sl: (b, qi, h)),
            pl.BlockSpec((1, S, hw), lambda h, qi, b, sl: (b, 0, h)),
            pl.BlockSpec((1, S, hw), lambda h, qi, b, sl: (b, 0, h)),
            pl.BlockSpec((4, DA_HEAD_DIM), lambda h, qi, b, sl: (0, 0)),
            pl.BlockSpec((1, hw), lambda h, qi, b, sl: (0, 0)),
        ],
        out_specs=pl.BlockSpec((1, tq, hw), lambda h, qi, b, sl: (b, qi, h)),
        scratch_shapes=[pltpu.VMEM((tq, S), F32)],
    )
    return pl.pallas_call(
        functools.partial(_diffattn_kernel, lam_init),
        grid_spec=grid_spec,
        out_shape=jax.ShapeDtypeStruct((B, S, W), BF16),
        compiler_params=_cparams(3),
        name="diffattn",
    )(slopes, qa, ka, va, lamv, subln_g.reshape(1, hw))


def _na_bias_table(rpb, rows):
    kh = min(WIN_R, rows)
    qcol = np.arange(GRID_W)
    kcol = np.arange(GRID_W)
    cs = np.clip(qcol - WIN_C // 2, 0, GRID_W - WIN_C)
    col_mask = (kcol[None, :] >= cs[:, None]) & (kcol[None, :] < cs[:, None] + WIN_C)
    col_off = np.clip(kcol[None, :] - qcol[:, None] + WIN_C - 1, 0, 2 * WIN_C - 2)
    v = np.arange(kh)
    j = np.arange(kh)
    row_idx = np.clip(j[None, :] - v[:, None] + WIN_R - 1, 0, 2 * WIN_R - 2)
    t = (rpb * LOG2E)[:, row_idx]
    t = t[:, :, :, col_off]
    t = jnp.where(jnp.asarray(col_mask)[None, None, None], t, NEG_BIG)
    t = t.transpose(1, 0, 3, 2, 4)
    nblk = NA_HEADS // NA_HEADS_PER_BLOCK
    return t.reshape(kh, nblk, NA_HEADS_PER_BLOCK * GRID_W, kh * GRID_W)


def _natten_kernel(rows, kh, q_ref, k_ref, v_ref, bias_ref, o_ref):
    hid = lax.broadcasted_iota(I32, (GRID_W, LANES), 1) // NA_HEAD_DIM

    def scores(r):
        rs = jnp.clip(r - WIN_R // 2, 0, rows - kh)
        q0 = pl.multiple_of(r * GRID_W, GRID_W)
        k0 = pl.multiple_of(rs * GRID_W, GRID_W)
        qr = q_ref[0, pl.ds(q0, GRID_W), :]
        zero = jnp.zeros_like(qr)
        qq = jnp.concatenate(
            [jnp.where(hid == hh, qr, zero) for hh in range(NA_HEADS_PER_BLOCK)], axis=0)
        kb = k_ref[0, pl.ds(k0, kh * GRID_W), :]
        s = lax.dot_general(qq, kb, (((1,), (1,)), ((), ())), preferred_element_type=F32)
        return s + bias_ref[r - rs, 0]

    def weights(s):
        e = jnp.exp2(s - jnp.max(s, axis=-1, keepdims=True))
        return e.astype(BF16), 1.0 / jnp.sum(e, axis=-1, keepdims=True)

    def finish(r, e, rl):
        rs = jnp.clip(r - WIN_R // 2, 0, rows - kh)
        q0 = pl.multiple_of(r * GRID_W, GRID_W)
        k0 = pl.multiple_of(rs * GRID_W, GRID_W)
        vb = v_ref[0, pl.ds(k0, kh * GRID_W), :]
        oo = jnp.dot(e, vb, preferred_element_type=F32) * rl
        o = jnp.zeros((GRID_W, LANES), F32)
        for hh in range(NA_HEADS_PER_BLOCK):
            o = o + jnp.where(hid == hh, oo[hh * GRID_W:(hh + 1) * GRID_W], 0.0)
        o_ref[0, pl.ds(q0, GRID_W), :] = o.astype(BF16)

    def row_group(g, carry):
        rr = [g * NA_ROWS_PER_ITER + i for i in range(NA_ROWS_PER_ITER)]
        ss = [scores(r) for r in rr]
        ws = [weights(s) for s in ss]
        for r, (e, rl) in zip(rr, ws):
            finish(r, e, rl)
        return carry

    lax.fori_loop(0, rows // NA_ROWS_PER_ITER, row_group, 0)


def _natten(qn, kn, vn, bias_tab):
    B, S, W = qn.shape
    rows = S // GRID_W
    kh = min(WIN_R, rows)
    nblk = W // LANES
    blk = lambda g, b: (b, 0, g)
    return pl.pallas_call(
        functools.partial(_natten_kernel, rows, kh),
        grid=(nblk, B),
        in_specs=[
            pl.BlockSpec((1, S, LANES), blk),
            pl.BlockSpec((1, S, LANES), blk),
            pl.BlockSpec((1, S, LANES), blk),
            pl.BlockSpec((kh, 1, NA_HEADS_PER_BLOCK * GRID_W, kh * GRID_W), lambda g, b: (0, g, 0, 0)),
        ],
        out_specs=pl.BlockSpec((1, S, LANES), blk),
        out_shape=jax.ShapeDtypeStruct((B, S, W), BF16),
        compiler_params=_cparams(2),
        name="natten",
    )(qn, kn, vn, bias_tab)


def _mixout_kernel(oa_ref, ob_ref, ga_ref, gb_ref, h_ref, bg_ref, wa_ref, wb_ref, wo_ref,
                   lg_ref, lb_ref, h1_ref, h1p_ref):
    ya = jnp.dot(oa_ref[...], wa_ref[...], preferred_element_type=F32)
    yb = jnp.dot(ob_ref[...], wb_ref[...], preferred_element_type=F32)
    g_a = jax.nn.sigmoid(ga_ref[...].astype(F32) + bg_ref[0:1, :])
    g_b = jax.nn.sigmoid(gb_ref[...].astype(F32) + bg_ref[1:2, :])
    z = (g_a * ya + g_b * yb).astype(BF16)
    m = jnp.dot(z, wo_ref[...], preferred_element_type=F32)
    y = _layer_norm(ALPHA * h_ref[...] + m, lg_ref[...], lb_ref[...])
    h1_ref[...] = y
    h1p_ref[...] = _pack_bf16_pairs(y)


def _mixout(oa, ob, ga, gb, h, b_gate, wa, wb, wo, ln_g, ln_b):
    T, D = h.shape
    tm = min(TM_PROJ, T)
    row = lambda i: (i, 0)
    const = lambda i: (0, 0)
    res = lambda a: pl.BlockSpec(a.shape, const, pipeline_mode=pl.Buffered(1))
    return pl.pallas_call(
        _mixout_kernel,
        grid=(T // tm,),
        in_specs=[
            pl.BlockSpec((tm, oa.shape[1]), row),
            pl.BlockSpec((tm, ob.shape[1]), row),
            pl.BlockSpec((tm, D), row),
            pl.BlockSpec((tm, D), row),
            pl.BlockSpec((tm, D), row),
            pl.BlockSpec((2, D), const),
            res(wa), res(wb), res(wo),
            pl.BlockSpec((1, D), const),
            pl.BlockSpec((1, D), const),
        ],
        out_specs=[pl.BlockSpec((tm, D), row), pl.BlockSpec((tm, D // 2), row)],
        out_shape=[jax.ShapeDtypeStruct((T, D), F32), jax.ShapeDtypeStruct((T, D // 2), U32)],
        compiler_params=_cparams(1),
        name="mixout",
    )(oa, ob, ga, gb, h, b_gate, wa, wb, wo, ln_g.reshape(1, D), ln_b.reshape(1, D))


def _router_kernel(h_ref, rwt_ref, rb_ref, tri_ref, eidx_ref, gate_ref, rank_ref, cnt_ref, carry_ref):
    i = pl.program_id(0)
    tt = h_ref.shape[0]
    E, G = N_EXPERTS, N_GROUPS
    P = E // G
    neg = -jnp.inf

    @pl.when(i == 0)
    def _():
        carry_ref[...] = jnp.zeros_like(carry_ref)

    hb = h_ref[...].astype(BF16)
    logits = lax.dot_general(rwt_ref[...], hb, (((1,), (1,)), ((), ())), preferred_element_type=F32)
    scores = jax.nn.sigmoid(logits)
    biased = scores + rb_ref[...]
    b3 = biased.reshape(G, P, tt)
    s3 = scores.reshape(G, P, tt)
    pi = lax.broadcasted_iota(I32, (G, P, tt), 1)
    ei = lax.broadcasted_iota(I32, (G, P, tt), 0) * P + pi

    m1 = jnp.max(b3, axis=1, keepdims=True)
    i1 = jnp.min(jnp.where(b3 == m1, pi, P), axis=1, keepdims=True)
    m2 = jnp.max(jnp.where(pi == i1, neg, b3), axis=1, keepdims=True)
    grp = m1 + m2
    gi = lax.broadcasted_iota(I32, (G, 1, tt), 0)
    gsel = jnp.zeros((G, 1, tt), F32)
    for _ in range(TOPK_GROUPS):
        gm = jnp.max(grp, axis=0, keepdims=True)
        gidx = jnp.min(jnp.where(grp == gm, gi, G), axis=0, keepdims=True)
        hit = gi == gidx
        gsel = jnp.where(hit, 1.0, gsel)
        grp = jnp.where(hit, neg, grp)

    cand = jnp.where(gsel > 0.0, b3, neg)
    sel = jnp.zeros((G, P, tt), F32)
    eids, gates = [], []
    for _ in range(TOP_K):
        mk = jnp.max(cand, axis=(0, 1), keepdims=True)
        ik = jnp.min(jnp.where(cand == mk, ei, E), axis=(0, 1), keepdims=True)
        hit = ei == ik
        gates.append(jnp.sum(jnp.where(hit, s3, 0.0), axis=(0, 1), keepdims=True))
        eids.append(ik)
        cand = jnp.where(hit, neg, cand)
        sel = jnp.where(hit, 1.0, sel)

    gsum = gates[0]
    for gk in gates[1:]:
        gsum = gsum + gk
    gscale = ROUTE_SCALE / gsum

    sel2 = sel.reshape(E, tt)
    prefix = jnp.dot(sel2.astype(BF16), tri_ref[...], preferred_element_type=F32)
    base3 = (prefix + carry_ref[...]).reshape(G, P, tt)
    ranks = [jnp.sum(jnp.where(ei == ik, base3, 0.0), axis=(0, 1), keepdims=True) for ik in eids]
    carry_ref[...] = carry_ref[...] + jnp.sum(sel2, axis=1, keepdims=True)
    cnt_ref[...] = carry_ref[...]

    eidx_ref[...] = jnp.concatenate([x.reshape(1, tt) for x in eids], axis=0)
    gate_ref[...] = jnp.concatenate([(g * gscale).reshape(1, tt) for g in gates], axis=0)
    rank_ref[...] = jnp.concatenate([x.reshape(1, tt) for x in ranks], axis=0).astype(I32)


def _router(h1, router_w, router_bias):
    T, D = h1.shape
    E = N_EXPERTS
    tt = min(TT_ROUTER, T)
    rwt = router_w.T.astype(BF16)
    tri = jnp.asarray(np.triu(np.ones((tt, tt), np.float32), k=1), BF16)
    const = lambda i: (0, 0)
    col = lambda i: (0, i)
    return pl.pallas_call(
        _router_kernel,
        grid=(T // tt,),
        in_specs=[
            pl.BlockSpec((tt, D), lambda i: (i, 0)),
            pl.BlockSpec((E, D), const),
            pl.BlockSpec((E, 1), const),
            pl.BlockSpec((tt, tt), const),
        ],
        out_specs=[
            pl.BlockSpec((TOP_K, tt), col),
            pl.BlockSpec((TOP_K, tt), col),
            pl.BlockSpec((TOP_K, tt), col),
            pl.BlockSpec((E, 1), const),
        ],
        out_shape=[
            jax.ShapeDtypeStruct((TOP_K, T), I32),
            jax.ShapeDtypeStruct((TOP_K, T), F32),
            jax.ShapeDtypeStruct((TOP_K, T), I32),
            jax.ShapeDtypeStruct((E, 1), F32),
        ],
        scratch_shapes=[pltpu.VMEM((E, 1), F32)],
        compiler_params=_cparams(1),
        name="router",
    )(h1, rwt, router_bias.reshape(E, 1).astype(F32), tri)


def _slots_kernel(pstart_ref, eidx_ref, rank_ref, slot_ref):
    eidx = eidx_ref[...]

    def add_expert(e, acc):
        return acc + jnp.where(eidx == e, pstart_ref[e], 0)

    slot_ref[...] = lax.fori_loop(0, N_EXPERTS, add_expert, rank_ref[...], unroll=8)


def _slots(pad_start, eidx, rank):
    K, T = eidx.shape
    tt = min(2048, T)
    col = lambda i, ps: (0, i)
    grid_spec = pltpu.PrefetchScalarGridSpec(
        num_scalar_prefetch=1,
        grid=(T // tt,),
        in_specs=[pl.BlockSpec((K, tt), col), pl.BlockSpec((K, tt), col)],
        out_specs=pl.BlockSpec((K, tt), col),
    )
    return pl.pallas_call(
        _slots_kernel,
        grid_spec=grid_spec,
        out_shape=jax.ShapeDtypeStruct((K, T), I32),
        compiler_params=_cparams(1),
        name="slots",
    )(pad_start, eidx, rank)


def _dispatch(slot, h1p, n_slots):
    T, Wp = h1p.shape
    n_workers = SC_CORES * SC_SUBCORES
    per = T // n_workers
    assert per % SC_IDX_CHUNK == 0, (T, n_workers)
    mesh = plsc.VectorSubcoreMesh(core_axis_name="core", subcore_axis_name="subcore")

    @pl.kernel(
        out_type=jax.ShapeDtypeStruct((n_slots, Wp), U32), mesh=mesh,
        scratch_types=[pltpu.VMEM((TOP_K, SC_IDX_CHUNK), I32),
                       pltpu.VMEM((SC_IDX_CHUNK // SC_ROW_WIN, SC_ROW_WIN, Wp), U32),
                       pltpu.SemaphoreType.DMA((SC_IDX_CHUNK // SC_ROW_WIN,)), pltpu.SemaphoreType.DMA])
    def dispatch(x_hbm, s_hbm, o_hbm, idx_v, x_v, sem_r, sem_w):
        base = (lax.axis_index("core") * SC_SUBCORES + lax.axis_index("subcore")) * per
        n_win = SC_IDX_CHUNK // SC_ROW_WIN

        @pl.loop(0, per // SC_IDX_CHUNK)
        def _(c):
            t0 = base + c * SC_IDX_CHUNK
            reads = [pltpu.async_copy(x_hbm.at[pl.ds(t0 + j * SC_ROW_WIN, SC_ROW_WIN)], x_v.at[j], sem_r.at[j])
                     for j in range(n_win)]
            pltpu.sync_copy(s_hbm.at[:, pl.ds(t0, SC_IDX_CHUNK)], idx_v)
            copies = []
            for j in range(n_win):
                reads[j].wait()
                copies += [
                    pltpu.async_copy(x_v.at[j], o_hbm.at[idx_v.at[k, pl.ds(j * SC_ROW_WIN, SC_ROW_WIN)]], sem_w)
                    for k in range(TOP_K)]
            for cp in copies:
                cp.wait()

    return dispatch(h1p, slot)


def _experts_kernel(be_ref, nv_ref, xs_ref, wg_ref, wu_ref, wd_ref, ys_ref, wg_s, wu_s, wd_s):
    i = pl.program_id(0)

    @pl.when(i < nv_ref[0])
    def _():
        prev = be_ref[jnp.maximum(i - 1, 0)]

        @pl.when((i == 0) | (be_ref[i] != prev))
        def _():
            wg_s[...] = wg_ref[0].astype(BF16)
            wu_s[...] = wu_ref[0].astype(BF16)
            wd_s[...] = wd_ref[0].astype(BF16)

        sub = min(SUB_MOE, xs_ref.shape[0])

        def up(j):
            lo, hi = _unpack_bf16_pairs(xs_ref[j * sub:(j + 1) * sub, :])
            x = jnp.concatenate([lo.astype(BF16), hi.astype(BF16)], axis=1)
            return (jnp.dot(x, wg_s[...], preferred_element_type=F32),
                    jnp.dot(x, wu_s[...], preferred_element_type=F32))

        def down(j, g, u):
            hmid = (g * jax.nn.sigmoid(g) * u).astype(BF16)
            y = jnp.dot(hmid, wd_s[...], preferred_element_type=F32)
            ys_ref[j * sub:(j + 1) * sub, :] = _pack_bf16_pairs(y)

        n_sub = xs_ref.shape[0] // sub
        pending = up(0)
        for j in range(n_sub):
            nxt = up(j + 1) if j + 1 < n_sub else None
            down(j, *pending)
            pending = nxt


def _experts(layer, blk_exp, n_valid, xs, wg, wu, wd):
    n_slots, Wp = xs.shape
    _, E, D, F = wg.shape
    nb = n_slots // BLK_MOE

    def xmap(i, be, nv):
        return (jnp.minimum(i, nv[0] - 1), 0)

    def wmap(i, be, nv):
        return (layer, be[i], 0, 0)

    grid_spec = pltpu.PrefetchScalarGridSpec(
        num_scalar_prefetch=2,
        grid=(nb,),
        in_specs=[
            pl.BlockSpec((BLK_MOE, Wp), xmap),
            pl.BlockSpec((None, 1, D, F), wmap),
            pl.BlockSpec((None, 1, D, F), wmap),
            pl.BlockSpec((None, 1, F, D), wmap),
        ],
        out_specs=pl.BlockSpec((BLK_MOE, Wp), xmap),
        scratch_shapes=[pltpu.VMEM((D, F), BF16), pltpu.VMEM((D, F), BF16), pltpu.VMEM((F, D), BF16)],
    )
    return pl.pallas_call(
        _experts_kernel,
        grid_spec=grid_spec,
        out_shape=jax.ShapeDtypeStruct((n_slots, Wp), U32),
        compiler_params=_cparams(1),
        name="experts",
    )(blk_exp, n_valid, xs, wg, wu, wd)


def _gather_rows(slot, ys):
    K, T = slot.shape
    Wp = ys.shape[1]
    n_workers = SC_CORES * SC_SUBCORES
    per = T // n_workers
    assert per % SC_IDX_CHUNK == 0 and K % SC_GATHER_PLANES == 0, (T, K)
    win = SC_ROW_WIN // 4
    mesh = plsc.VectorSubcoreMesh(core_axis_name="core", subcore_axis_name="subcore")
    groups = [(j, k0) for j in range(SC_IDX_CHUNK // win) for k0 in range(0, K, SC_GATHER_PLANES)]

    @pl.kernel(
        out_type=jax.ShapeDtypeStruct((K, T, Wp), U32), mesh=mesh,
        scratch_types=[pltpu.VMEM((K, SC_IDX_CHUNK), I32),
                       pltpu.VMEM((2, SC_GATHER_PLANES, win, Wp), U32),
                       pltpu.SemaphoreType.DMA((2,)), pltpu.SemaphoreType.DMA((2,))])
    def gather(y_hbm, s_hbm, g_hbm, idx_v, buf, sem_g, sem_w):
        base = (lax.axis_index("core") * SC_SUBCORES + lax.axis_index("subcore")) * per

        @pl.loop(0, per // SC_IDX_CHUNK)
        def _(c):
            t0 = base + c * SC_IDX_CHUNK
            pltpu.sync_copy(s_hbm.at[:, pl.ds(t0, SC_IDX_CHUNK)], idx_v)

            def start_reads(g):
                j, k0 = groups[g]
                return [pltpu.async_copy(y_hbm.at[idx_v.at[k0 + i, pl.ds(j * win, win)]],
                                         buf.at[g % 2, i], sem_g.at[g % 2])
                        for i in range(SC_GATHER_PLANES)]

            def start_writes(g):
                j, k0 = groups[g]
                return [pltpu.async_copy(buf.at[g % 2, i], g_hbm.at[k0 + i, pl.ds(t0 + j * win, win)],
                                         sem_w.at[g % 2])
                        for i in range(SC_GATHER_PLANES)]

            reads = start_reads(0)
            writes_prev = []
            for g in range(len(groups)):
                for cp in reads:
                    cp.wait()
                writes = start_writes(g)
                for cp in writes_prev:
                    cp.wait()
                if g + 1 < len(groups):
                    reads = start_reads(g + 1)
                writes_prev = writes
            for cp in writes_prev:
                cp.wait()

    return gather(ys, slot)


def _combine_kernel(yg_ref, gate_ref, h1_ref, swg_ref, swu_ref, swd_ref, lg_ref, lb_ref, out_ref):
    h1 = h1_ref[...]
    xb = h1.astype(BF16)
    g = jnp.dot(xb, swg_ref[...], preferred_element_type=F32)
    u = jnp.dot(xb, swu_ref[...], preferred_element_type=F32)
    hmid = (g * jax.nn.sigmoid(g) * u).astype(BF16)
    shared = jnp.dot(hmid, swd_ref[...], preferred_element_type=F32)

    gates = gate_ref[...]
    tc, wp = yg_ref.shape[1], yg_ref.shape[2]
    f_lo = jnp.zeros((tc, wp), F32)
    f_hi = jnp.zeros((tc, wp), F32)
    for k in range(TOP_K):
        lo, hi = _unpack_bf16_pairs(yg_ref[k])
        gk = gates[:, k:k + 1]
        f_lo = f_lo + gk * lo
        f_hi = f_hi + gk * hi
    routed = jnp.concatenate([f_lo, f_hi], axis=1)
    out_ref[...] = _layer_norm(ALPHA * h1 + (routed + shared), lg_ref[...], lb_ref[...])


def _combine(yg, gate_t, h1, swg, swu, swd, ln_g, ln_b):
    T, D = h1.shape
    K, _, Wp = yg.shape
    tc = min(TD_MOE, T)
    row = lambda i: (i, 0)
    const = lambda i: (0, 0)
    res = lambda a: pl.BlockSpec(a.shape, const)
    return pl.pallas_call(
        _combine_kernel,
        grid=(T // tc,),
        in_specs=[
            pl.BlockSpec((K, tc, Wp), lambda i: (0, i, 0)),
            pl.BlockSpec((tc, TOP_K), row),
            pl.BlockSpec((tc, D), row),
            res(swg), res(swu), res(swd),
            pl.BlockSpec((1, D), const),
            pl.BlockSpec((1, D), const),
        ],
        out_specs=pl.BlockSpec((tc, D), row),
        out_shape=jax.ShapeDtypeStruct((T, D), F32),
        compiler_params=_cparams(1),
        name="combine",
    )(yg, gate_t, h1, swg, swu, swd, ln_g.reshape(1, D), ln_b.reshape(1, D))


def _moe_layer(layer, h1, h1p, router_w, router_bias, wg, wu, wd, swg, swu, swd, ln_g, ln_b):
    T, D = h1.shape
    E = N_EXPERTS
    M = T * TOP_K
    eidx, gate, rank, cnt = _router(h1, router_w, router_bias)

    counts = cnt[:, 0].astype(I32)
    padded = (counts + BLK_MOE - 1) // BLK_MOE * BLK_MOE
    pad_end = jnp.cumsum(padded)
    pad_start = pad_end - padded
    nb = -(-(M + E * (BLK_MOE - 1)) // BLK_MOE)
    n_slots = nb * BLK_MOE
    slot = _slots(pad_start.astype(I32), eidx, rank)
    n_valid = (pad_end[-1] // BLK_MOE).astype(I32).reshape(1)
    blk_exp = jnp.minimum(
        jnp.searchsorted(pad_end, jnp.arange(nb, dtype=I32) * BLK_MOE, side="right"), E - 1).astype(I32)

    xs = _dispatch(slot, h1p, n_slots)
    ys = _experts(layer, blk_exp, n_valid, xs, wg, wu, wd)
    yg = _gather_rows(slot, ys)
    return _combine(yg, gate.T, h1, swg.astype(BF16), swu.astype(BF16), swd.astype(BF16), ln_g, ln_b)


def kernel(x, emb_ln_g, emb_ln_b, w_in, b_gate, lam_q1, lam_k1, lam_q2, lam_k2, subln_g, w_proj_a, na_rpb, w_proj_b, w_out, ln1_g, ln1_b, router_w, router_bias, exp_w_gate, exp_w_up, exp_w_down, sh_w_gate, sh_w_up, sh_w_down, ln2_g, ln2_b):
    B, S, D = x.shape
    T = B * S
    rows = S // GRID_W
    h = x.reshape(T, D)
    for l in range(DEPTH):
        lam_init = 0.8 - 0.6 * math.exp(-0.3 * l)
        outs = _inproj(h, emb_ln_g, emb_ln_b, w_in[l].astype(BF16), apply_ln=(l == 0))
        if l == 0:
            h, outs = outs[0], outs[1:]
        qa, ka, va, qn, kn, vn, ga, gb = outs
        lamv = jnp.stack([lam_q1[l], lam_k1[l], lam_q2[l], lam_k2[l]]).astype(F32)
        oa = _diffattn(qa.reshape(B, S, -1), ka.reshape(B, S, -1), va.reshape(B, S, -1),
                       lamv, subln_g[l], lam_init)
        ob = _natten(qn.reshape(B, S, -1), kn.reshape(B, S, -1), vn.reshape(B, S, -1),
                     _na_bias_table(na_rpb[l], rows))
        h1, h1p = _mixout(oa.reshape(T, -1), ob.reshape(T, -1), ga, gb, h, b_gate[l],
                          w_proj_a[l].astype(BF16), w_proj_b[l].astype(BF16), w_out[l].astype(BF16),
                          ln1_g[l], ln1_b[l])
        h = _moe_layer(l, h1, h1p, router_w[l], router_bias[l], exp_w_gate, exp_w_up, exp_w_down,
                       sh_w_gate[l], sh_w_up[l], sh_w_down[l], ln2_g[l], ln2_b[l])
    return h.reshape(B, S, D)
```

```python
import functools
import math

import numpy as np
import jax
import jax.numpy as jnp
from jax import lax
from jax.experimental import pallas as pl
from jax.experimental.pallas import tpu as pltpu
from jax.experimental.pallas import tpu_sc as plsc

F32 = jnp.float32
BF16 = jnp.bfloat16
U32 = jnp.uint32
I32 = jnp.int32

DA_HEADS = 8
DA_HEAD_DIM = 64
NA_HEADS = 16
NA_HEAD_DIM = 32
GRID_W = 64
WIN_R = 8
WIN_C = 16
N_EXPERTS = 256
TOP_K = 8
N_GROUPS = 8
TOPK_GROUPS = 4
ROUTE_SCALE = 2.5
DEPTH = 2
ALPHA = (2 * DEPTH) ** 0.25
LN_EPS = 1e-5
LOG2E = 1.4426950408889634

LANES = 128
NA_HEADS_PER_BLOCK = LANES // NA_HEAD_DIM
VMEM_LIMIT = 56 * 1024 * 1024
SC_CORES = 2
SC_SUBCORES = 16
SC_IDX_CHUNK = 128
SC_ROW_WIN = 64
SC_GATHER_PLANES = 4

TM_PROJ = 512
TQ_DA = 1024
TQ_SUB_DA = 128
NA_ROWS_PER_ITER = 4
TT_ROUTER = 512
TD_MOE = 256
BLK_MOE = 512
SUB_MOE = 256
MOE_COMBINE_PARTS = 2
NEG_BIG = -1e30


def _cparams(n_axes, flags=None):
    return pltpu.CompilerParams(
        dimension_semantics=("arbitrary",) * n_axes, vmem_limit_bytes=VMEM_LIMIT, flags=flags)


def _layer_norm(x, g, b):
    mu = jnp.mean(x, axis=-1, keepdims=True)
    xc = x - mu
    var = jnp.mean(xc * xc, axis=-1, keepdims=True)
    return xc * lax.rsqrt(var + LN_EPS) * g + b


def _pack_bf16_pairs(y):
    w = y.shape[1] // 2
    lo = lax.bitcast_convert_type(y[:, :w].astype(BF16).astype(F32), U32)
    hi = lax.bitcast_convert_type(y[:, w:].astype(BF16).astype(F32), U32)
    return (hi & jnp.uint32(0xFFFF0000)) | (lo >> 16)


def _unpack_bf16_pairs(u):
    lo = lax.bitcast_convert_type(u << 16, F32)
    hi = lax.bitcast_convert_type(u & jnp.uint32(0xFFFF0000), F32)
    return lo, hi


def _inproj_kernel(apply_ln, seg_widths, seg_scales, x_ref, g_ref, b_ref, w_ref, *out_refs):
    x = x_ref[...]
    if apply_ln:
        x = _layer_norm(x, g_ref[...], b_ref[...])
        out_refs[0][...] = x
        out_refs = out_refs[1:]
    xb = x.astype(BF16)
    off = 0
    for ref, width, scale in zip(out_refs, seg_widths, seg_scales):
        y = jnp.dot(xb, w_ref[:, off:off + width], preferred_element_type=F32)
        if scale != 1.0:
            y = y * scale
        ref[...] = y.astype(BF16)
        off += width


def _inproj(x, ln_g, ln_b, w_in_bf16, apply_ln):
    T, D = x.shape
    da_w = DA_HEADS * 2 * DA_HEAD_DIM
    na_w = NA_HEADS * NA_HEAD_DIM
    seg_widths = (da_w, da_w, da_w, na_w, na_w, na_w, D, D)
    seg_scales = (DA_HEAD_DIM ** -0.5 * LOG2E, 1.0, 1.0, NA_HEAD_DIM ** -0.5 * LOG2E, 1.0, 1.0, 1.0, 1.0)
    tm = min(TM_PROJ, T)
    n_cols = w_in_bf16.shape[1]
    row = lambda i: (i, 0)
    const = lambda i: (0, 0)
    out_shape = [jax.ShapeDtypeStruct((T, w), BF16) for w in seg_widths]
    out_specs = [pl.BlockSpec((tm, w), row) for w in seg_widths]
    if apply_ln:
        out_shape = [jax.ShapeDtypeStruct((T, D), F32)] + out_shape
        out_specs = [pl.BlockSpec((tm, D), row)] + out_specs
    return pl.pallas_call(
        functools.partial(_inproj_kernel, apply_ln, seg_widths, seg_scales),
        grid=(T // tm,),
        in_specs=[
            pl.BlockSpec((tm, D), row),
            pl.BlockSpec((1, D), const),
            pl.BlockSpec((1, D), const),
            pl.BlockSpec((D, n_cols), const, pipeline_mode=pl.Buffered(1)),
        ],
        out_specs=out_specs,
        out_shape=out_shape,
        compiler_params=_cparams(1),
        name="inproj",
    )(x, ln_g.reshape(1, D), ln_b.reshape(1, D), w_in_bf16)


def _diffattn_kernel(lam_init, slopes_ref, q_ref, k_ref, v_ref, lamv_ref, g_ref, o_ref, bias_ref):
    h = pl.program_id(0)
    qi = pl.program_id(1)
    b = pl.program_id(2)
    tq = q_ref.shape[1]
    S = k_ref.shape[1]
    d = DA_HEAD_DIM

    @pl.when(b == 0)
    def _():
        qpos = qi * tq + lax.broadcasted_iota(I32, (tq, S), 0)
        kpos = lax.broadcasted_iota(I32, (tq, S), 1)
        bias_ref[...] = jnp.abs(qpos - kpos).astype(F32) * (-slopes_ref[h])

    lv = lamv_ref[...]
    lam = (jnp.exp(jnp.sum(lv[0:1] * lv[1:2], axis=-1, keepdims=True))
           - jnp.exp(jnp.sum(lv[2:3] * lv[3:4], axis=-1, keepdims=True)) + lam_init)

    k = k_ref[0]
    v1 = jnp.concatenate([v_ref[0], jnp.ones((S, 2 * d), BF16)], axis=1)
    gain = g_ref[...] * (1.0 - lam_init)
    nt = (((1,), (1,)), ((), ()))
    ts = min(TQ_SUB_DA, tq)
    lane = lax.broadcasted_iota(I32, (ts, 2 * d), 1)

    def scores(j):
        q = q_ref[0, j * ts:(j + 1) * ts, :]
        nb = bias_ref[j * ts:(j + 1) * ts, :]
        zero = jnp.zeros_like(q)
        s1 = lax.dot_general(jnp.where(lane < d, q, zero), k, nt, preferred_element_type=F32) + nb
        s2 = lax.dot_general(jnp.where(lane >= d, q, zero), k, nt, preferred_element_type=F32) + nb
        return s1, s2

    def weights(s1, s2):
        return (jnp.exp2((s1 - jnp.max(s1, axis=-1, keepdims=True)).astype(BF16)),
                jnp.exp2((s2 - jnp.max(s2, axis=-1, keepdims=True)).astype(BF16)))

    def finish(j, e1, e2):
        p1 = jnp.dot(e1, v1, preferred_element_type=F32)
        p2 = jnp.dot(e2, v1, preferred_element_type=F32)
        o = p1[:, :2 * d] / p1[:, 2 * d:] - lam * (p2[:, :2 * d] / p2[:, 2 * d:])
        ms = jnp.mean(o * o, axis=-1, keepdims=True)
        o = o * lax.rsqrt(ms + LN_EPS) * gain
        o_ref[0, j * ts:(j + 1) * ts, :] = o.astype(BF16)

    n_sub = tq // ts
    s_next = scores(0)
    e_cur = weights(*s_next)
    s_next = scores(1) if n_sub > 1 else None
    for j in range(n_sub):
        s_after = scores(j + 2) if j + 2 < n_sub else None
        finish(j, *e_cur)
        if s_next is not None:
            e_cur = weights(*s_next)
        s_next = s_after


def _diffattn(qa, ka, va, lamv, subln_g, lam_init):
    B, S, W = qa.shape
    hw = 2 * DA_HEAD_DIM
    tq = min(TQ_DA, S)
    slopes = jnp.asarray(2.0 ** (-8.0 * np.arange(1, DA_HEADS + 1) / DA_HEADS) * LOG2E, F32)
    grid_spec = pltpu.PrefetchScalarGridSpec(
        num_scalar_prefetch=1,
        grid=(DA_HEADS, S // tq, B),
        in_specs=[
            pl.BlockSpec((1, tq, hw), lambda h, qi, b, sl: (b, qi, h)),
            pl.BlockSpec((1, S, hw), lambda h, qi, b, sl: (b, 0, h)),
            pl.BlockSpec((1, S, hw), lambda h, qi, b, sl: (b, 0, h)),
            pl.BlockSpec((4, DA_HEAD_DIM), lambda h, qi, b, sl: (0, 0)),
            pl.BlockSpec((1, hw), lambda h, qi, b, sl: (0, 0)),
        ],
        out_specs=pl.BlockSpec((1, tq, hw), lambda h, qi, b, sl: (b, qi, h)),
        scratch_shapes=[pltpu.VMEM((tq, S), F32)],
    )
    return pl.pallas_call(
        functools.partial(_diffattn_kernel, lam_init),
        grid_spec=grid_spec,
        out_shape=jax.ShapeDtypeStruct((B, S, W), BF16),
        compiler_params=_cparams(3),
        name="diffattn",
    )(slopes, qa, ka, va, lamv, subln_g.reshape(1, hw))


def _na_bias_table(rpb, rows):
    kh = min(WIN_R, rows)
    qcol = np.arange(GRID_W)
    kcol = np.arange(GRID_W)
    cs = np.clip(qcol - WIN_C // 2, 0, GRID_W - WIN_C)
    col_mask = (kcol[None, :] >= cs[:, None]) & (kcol[None, :] < cs[:, None] + WIN_C)
    col_off = np.clip(kcol[None, :] - qcol[:, None] + WIN_C - 1, 0, 2 * WIN_C - 2)
    v = np.arange(kh)
    j = np.arange(kh)
    row_idx = np.clip(j[None, :] - v[:, None] + WIN_R - 1, 0, 2 * WIN_R - 2)
    sel_r = np.zeros((2 * WIN_R - 1, kh * kh), np.float32)
    sel_r[row_idx.reshape(-1), np.arange(kh * kh)] = 1.0
    sel_c = np.zeros((2 * WIN_C - 1, GRID_W * GRID_W), np.float32)
    sel_c[col_off.reshape(-1), np.arange(GRID_W * GRID_W)] = 1.0
    hi = lax.Precision.HIGHEST
    t = jnp.einsum("hrc,rp->hpc", rpb * LOG2E, jnp.asarray(sel_r), precision=hi)
    t = jnp.einsum("hpc,cq->hpq", t, jnp.asarray(sel_c), precision=hi)
    t = t.reshape(NA_HEADS, kh, kh, GRID_W, GRID_W)
    t = jnp.where(jnp.asarray(col_mask)[None, None, None], t, NEG_BIG)
    t = t.transpose(1, 0, 3, 2, 4)
    nblk = NA_HEADS // NA_HEADS_PER_BLOCK
    return t.reshape(kh, nblk, NA_HEADS_PER_BLOCK * GRID_W, kh * GRID_W)


def _natten_kernel(rows, kh, q_ref, k_ref, v_ref, bias_ref, o_ref):
    hid = lax.broadcasted_iota(I32, (GRID_W, LANES), 1) // NA_HEAD_DIM

    def scores(r):
        rs = jnp.clip(r - WIN_R // 2, 0, rows - kh)
        q0 = pl.multiple_of(r * GRID_W, GRID_W)
        k0 = pl.multiple_of(rs * GRID_W, GRID_W)
        qr = q_ref[0, pl.ds(q0, GRID_W), :]
        zero = jnp.zeros_like(qr)
        qq = jnp.concatenate(
            [jnp.where(hid == hh, qr, zero) for hh in range(NA_HEADS_PER_BLOCK)], axis=0)
        kb = k_ref[0, pl.ds(k0, kh * GRID_W), :]
        s = lax.dot_general(qq, kb, (((1,), (1,)), ((), ())), preferred_element_type=F32)
        return s + bias_ref[r - rs, 0]

    def weights(s):
        e = jnp.exp2(s - jnp.max(s, axis=-1, keepdims=True))
        return e.astype(BF16), 1.0 / jnp.sum(e, axis=-1, keepdims=True)

    def finish(r, e, rl):
        rs = jnp.clip(r - WIN_R // 2, 0, rows - kh)
        q0 = pl.multiple_of(r * GRID_W, GRID_W)
        k0 = pl.multiple_of(rs * GRID_W, GRID_W)
        vb = v_ref[0, pl.ds(k0, kh * GRID_W), :]
        oo = jnp.dot(e, vb, preferred_element_type=F32) * rl
        o = jnp.zeros((GRID_W, LANES), F32)
        for hh in range(NA_HEADS_PER_BLOCK):
            o = o + jnp.where(hid == hh, oo[hh * GRID_W:(hh + 1) * GRID_W], 0.0)
        o_ref[0, pl.ds(q0, GRID_W), :] = o.astype(BF16)

    def row_group(g, carry):
        rr = [g * NA_ROWS_PER_ITER + i for i in range(NA_ROWS_PER_ITER)]
        ss = [scores(r) for r in rr]
        ws = [weights(s) for s in ss]
        for r, (e, rl) in zip(rr, ws):
            finish(r, e, rl)
        return carry

    lax.fori_loop(0, rows // NA_ROWS_PER_ITER, row_group, 0)


def _natten(qn, kn, vn, bias_tab):
    B, S, W = qn.shape
    rows = S // GRID_W
    kh = min(WIN_R, rows)
    nblk = W // LANES
    blk = lambda g, b: (b, 0, g)
    return pl.pallas_call(
        functools.partial(_natten_kernel, rows, kh),
        grid=(nblk, B),
        in_specs=[
            pl.BlockSpec((1, S, LANES), blk),
            pl.BlockSpec((1, S, LANES), blk),
            pl.BlockSpec((1, S, LANES), blk),
            pl.BlockSpec((kh, 1, NA_HEADS_PER_BLOCK * GRID_W, kh * GRID_W), lambda g, b: (0, g, 0, 0)),
        ],
        out_specs=pl.BlockSpec((1, S, LANES), blk),
        out_shape=jax.ShapeDtypeStruct((B, S, W), BF16),
        compiler_params=_cparams(2),
        name="natten",
    )(qn, kn, vn, bias_tab)


def _mixout_kernel(oa_ref, ob_ref, ga_ref, gb_ref, h_ref, bg_ref, wa_ref, wb_ref, wo_ref,
                   lg_ref, lb_ref, h1_ref, h1p_ref):
    ya = jnp.dot(oa_ref[...], wa_ref[...], preferred_element_type=F32)
    yb = jnp.dot(ob_ref[...], wb_ref[...], preferred_element_type=F32)
    g_a = jax.nn.sigmoid(ga_ref[...].astype(F32) + bg_ref[0:1, :])
    g_b = jax.nn.sigmoid(gb_ref[...].astype(F32) + bg_ref[1:2, :])
    z = (g_a * ya + g_b * yb).astype(BF16)
    m = jnp.dot(z, wo_ref[...], preferred_element_type=F32)
    y = _layer_norm(ALPHA * h_ref[...] + m, lg_ref[...], lb_ref[...])
    h1_ref[...] = y
    h1p_ref[...] = _pack_bf16_pairs(y)


def _mixout(oa, ob, ga, gb, h, b_gate, wa, wb, wo, ln_g, ln_b):
    T, D = h.shape
    tm = min(TM_PROJ, T)
    row = lambda i: (i, 0)
    const = lambda i: (0, 0)
    res = lambda a: pl.BlockSpec(a.shape, const, pipeline_mode=pl.Buffered(1))
    return pl.pallas_call(
        _mixout_kernel,
        grid=(T // tm,),
        in_specs=[
            pl.BlockSpec((tm, oa.shape[1]), row),
            pl.BlockSpec((tm, ob.shape[1]), row),
            pl.BlockSpec((tm, D), row),
            pl.BlockSpec((tm, D), row),
            pl.BlockSpec((tm, D), row),
            pl.BlockSpec((2, D), const),
            res(wa), res(wb), res(wo),
            pl.BlockSpec((1, D), const),
            pl.BlockSpec((1, D), const),
        ],
        out_specs=[pl.BlockSpec((tm, D), row), pl.BlockSpec((tm, D // 2), row)],
        out_shape=[jax.ShapeDtypeStruct((T, D), F32), jax.ShapeDtypeStruct((T, D // 2), U32)],
        compiler_params=_cparams(1),
        name="mixout",
    )(oa, ob, ga, gb, h, b_gate, wa, wb, wo, ln_g.reshape(1, D), ln_b.reshape(1, D))


def _router_kernel(h_ref, rwt_ref, rb_ref, tri_ref, eidx_ref, gate_ref, rank_ref, cnt_ref, carry_ref):
    i = pl.program_id(0)
    tt = h_ref.shape[0]
    E, G = N_EXPERTS, N_GROUPS
    P = E // G
    neg = -jnp.inf

    @pl.when(i == 0)
    def _():
        carry_ref[...] = jnp.zeros_like(carry_ref)

    hb = h_ref[...].astype(BF16)
    logits = lax.dot_general(rwt_ref[...], hb, (((1,), (1,)), ((), ())), preferred_element_type=F32)
    scores = jax.nn.sigmoid(logits)
    biased = scores + rb_ref[...]
    b3 = biased.reshape(G, P, tt)
    s3 = scores.reshape(G, P, tt)
    pi = lax.broadcasted_iota(I32, (G, P, tt), 1)
    ei = lax.broadcasted_iota(I32, (G, P, tt), 0) * P + pi

    m1 = jnp.max(b3, axis=1, keepdims=True)
    i1 = jnp.min(jnp.where(b3 == m1, pi, P), axis=1, keepdims=True)
    m2 = jnp.max(jnp.where(pi == i1, neg, b3), axis=1, keepdims=True)
    grp = m1 + m2
    gi = lax.broadcasted_iota(I32, (G, 1, tt), 0)
    gsel = jnp.zeros((G, 1, tt), F32)
    for _ in range(TOPK_GROUPS):
        gm = jnp.max(grp, axis=0, keepdims=True)
        gidx = jnp.min(jnp.where(grp == gm, gi, G), axis=0, keepdims=True)
        hit = gi == gidx
        gsel = jnp.where(hit, 1.0, gsel)
        grp = jnp.where(hit, neg, grp)

    cand = jnp.where(gsel > 0.0, b3, neg)
    sel = jnp.zeros((G, P, tt), F32)
    eids, gates = [], []
    for _ in range(TOP_K):
        mk = jnp.max(cand, axis=(0, 1), keepdims=True)
        ik = jnp.min(jnp.where(cand == mk, ei, E), axis=(0, 1), keepdims=True)
        hit = ei == ik
        gates.append(jnp.sum(jnp.where(hit, s3, 0.0), axis=(0, 1), keepdims=True))
        eids.append(ik)
        cand = jnp.where(hit, neg, cand)
        sel = jnp.where(hit, 1.0, sel)

    gsum = gates[0]
    for gk in gates[1:]:
        gsum = gsum + gk
    gscale = ROUTE_SCALE / gsum

    sel2 = sel.reshape(E, tt)
    prefix = jnp.dot(sel2.astype(BF16), tri_ref[...], preferred_element_type=F32)
    base3 = (prefix + carry_ref[...]).reshape(G, P, tt)
    ranks = [jnp.sum(jnp.where(ei == ik, base3, 0.0), axis=(0, 1), keepdims=True) for ik in eids]
    carry_ref[...] = carry_ref[...] + jnp.sum(sel2, axis=1, keepdims=True)
    cnt_ref[...] = carry_ref[...]

    eidx_ref[...] = jnp.concatenate([x.reshape(1, tt) for x in eids], axis=0)
    gate_ref[...] = jnp.concatenate([(g * gscale).reshape(1, tt) for g in gates], axis=0)
    rank_ref[...] = jnp.concatenate([x.reshape(1, tt) for x in ranks], axis=0).astype(I32)


def _router(h1, router_w, router_bias):
    T, D = h1.shape
    E = N_EXPERTS
    tt = min(TT_ROUTER, T)
    rwt = router_w.T.astype(BF16)
    tri = jnp.asarray(np.triu(np.ones((tt, tt), np.float32), k=1), BF16)
    const = lambda i: (0, 0)
    col = lambda i: (0, i)
    return pl.pallas_call(
        _router_kernel,
        grid=(T // tt,),
        in_specs=[
            pl.BlockSpec((tt, D), lambda i: (i, 0)),
            pl.BlockSpec((E, D), const),
            pl.BlockSpec((E, 1), const),
            pl.BlockSpec((tt, tt), const),
        ],
        out_specs=[
            pl.BlockSpec((TOP_K, tt), col),
            pl.BlockSpec((TOP_K, tt), col),
            pl.BlockSpec((TOP_K, tt), col),
            pl.BlockSpec((E, 1), const),
        ],
        out_shape=[
            jax.ShapeDtypeStruct((TOP_K, T), I32),
            jax.ShapeDtypeStruct((TOP_K, T), F32),
            jax.ShapeDtypeStruct((TOP_K, T), I32),
            jax.ShapeDtypeStruct((E, 1), F32),
        ],
        scratch_shapes=[pltpu.VMEM((E, 1), F32)],
        compiler_params=_cparams(1),
        name="router",
    )(h1, rwt, router_bias.reshape(E, 1).astype(F32), tri)


def _slots_kernel(pstart_ref, eidx_ref, rank_ref, slot_ref):
    eidx = eidx_ref[...]

    def add_expert(e, acc):
        return acc + jnp.where(eidx == e, pstart_ref[e], 0)

    slot_ref[...] = lax.fori_loop(0, N_EXPERTS, add_expert, rank_ref[...], unroll=8)


def _slots(pad_start, eidx, rank):
    K, T = eidx.shape
    tt = min(2048, T)
    col = lambda i, ps: (0, i)
    grid_spec = pltpu.PrefetchScalarGridSpec(
        num_scalar_prefetch=1,
        grid=(T // tt,),
        in_specs=[pl.BlockSpec((K, tt), col), pl.BlockSpec((K, tt), col)],
        out_specs=pl.BlockSpec((K, tt), col),
    )
    return pl.pallas_call(
        _slots_kernel,
        grid_spec=grid_spec,
        out_shape=jax.ShapeDtypeStruct((K, T), I32),
        compiler_params=_cparams(1),
        name="slots",
    )(pad_start, eidx, rank)


def _dispatch(slot, h1p, n_slots):
    T, Wp = h1p.shape
    n_workers = SC_CORES * SC_SUBCORES
    per = T // n_workers
    assert per % SC_IDX_CHUNK == 0, (T, n_workers)
    mesh = plsc.VectorSubcoreMesh(core_axis_name="core", subcore_axis_name="subcore")

    @pl.kernel(
        out_type=jax.ShapeDtypeStruct((n_slots, Wp), U32), mesh=mesh,
        scratch_types=[pltpu.VMEM((TOP_K, SC_IDX_CHUNK), I32),
                       pltpu.VMEM((SC_IDX_CHUNK // SC_ROW_WIN, SC_ROW_WIN, Wp), U32),
                       pltpu.SemaphoreType.DMA((SC_IDX_CHUNK // SC_ROW_WIN,)), pltpu.SemaphoreType.DMA])
    def dispatch(x_hbm, s_hbm, o_hbm, idx_v, x_v, sem_r, sem_w):
        base = (lax.axis_index("core") * SC_SUBCORES + lax.axis_index("subcore")) * per
        n_win = SC_IDX_CHUNK // SC_ROW_WIN

        @pl.loop(0, per // SC_IDX_CHUNK)
        def _(c):
            t0 = base + c * SC_IDX_CHUNK
            reads = [pltpu.async_copy(x_hbm.at[pl.ds(t0 + j * SC_ROW_WIN, SC_ROW_WIN)], x_v.at[j], sem_r.at[j])
                     for j in range(n_win)]
            pltpu.sync_copy(s_hbm.at[:, pl.ds(t0, SC_IDX_CHUNK)], idx_v)
            copies = []
            for j in range(n_win):
                reads[j].wait()
                copies += [
                    pltpu.async_copy(x_v.at[j], o_hbm.at[idx_v.at[k, pl.ds(j * SC_ROW_WIN, SC_ROW_WIN)]], sem_w)
                    for k in range(TOP_K)]
            for cp in copies:
                cp.wait()

    return dispatch(h1p, slot)


def _experts_kernel(be_ref, nv_ref, xs_ref, wg_ref, wu_ref, wd_ref, ys_ref, wg_s, wu_s, wd_s):
    i = pl.program_id(0)

    @pl.when(i < nv_ref[0])
    def _():
        prev = be_ref[jnp.maximum(i - 1, 0)]

        @pl.when((i == 0) | (be_ref[i] != prev))
        def _():
            wg_s[...] = wg_ref[0].astype(BF16)
            wu_s[...] = wu_ref[0].astype(BF16)
            wd_s[...] = wd_ref[0].astype(BF16)

        sub = min(SUB_MOE, xs_ref.shape[0])

        def up(j):
            lo, hi = _unpack_bf16_pairs(xs_ref[j * sub:(j + 1) * sub, :])
            x = jnp.concatenate([lo.astype(BF16), hi.astype(BF16)], axis=1)
            return (jnp.dot(x, wg_s[...], preferred_element_type=F32),
                    jnp.dot(x, wu_s[...], preferred_element_type=F32))

        def down(j, g, u):
            hmid = (g * jax.nn.sigmoid(g) * u).astype(BF16)
            y = jnp.dot(hmid, wd_s[...], preferred_element_type=F32)
            ys_ref[j * sub:(j + 1) * sub, :] = _pack_bf16_pairs(y)

        n_sub = xs_ref.shape[0] // sub
        pending = up(0)
        for j in range(n_sub):
            nxt = up(j + 1) if j + 1 < n_sub else None
            down(j, *pending)
            pending = nxt


def _experts(layer, blk_exp, n_valid, xs, wg, wu, wd):
    n_slots, Wp = xs.shape
    _, E, D, F = wg.shape
    nb = n_slots // BLK_MOE

    def xmap(i, be, nv):
        return (jnp.minimum(i, nv[0] - 1), 0)

    def wmap(i, be, nv):
        return (layer, be[i], 0, 0)

    grid_spec = pltpu.PrefetchScalarGridSpec(
        num_scalar_prefetch=2,
        grid=(nb,),
        in_specs=[
            pl.BlockSpec((BLK_MOE, Wp), xmap),
            pl.BlockSpec((None, 1, D, F), wmap),
            pl.BlockSpec((None, 1, D, F), wmap),
            pl.BlockSpec((None, 1, F, D), wmap),
        ],
        out_specs=pl.BlockSpec((BLK_MOE, Wp), xmap),
        scratch_shapes=[pltpu.VMEM((D, F), BF16), pltpu.VMEM((D, F), BF16), pltpu.VMEM((F, D), BF16)],
    )
    return pl.pallas_call(
        _experts_kernel,
        grid_spec=grid_spec,
        out_shape=jax.ShapeDtypeStruct((n_slots, Wp), U32),
        compiler_params=_cparams(1),
        name="experts",
    )(blk_exp, n_valid, xs, wg, wu, wd)


def _gather_rows(slot, ys):
    K, T = slot.shape
    Wp = ys.shape[1]
    n_workers = SC_CORES * SC_SUBCORES
    per = T // n_workers
    assert per % SC_IDX_CHUNK == 0 and K % SC_GATHER_PLANES == 0, (T, K)
    win = SC_ROW_WIN // 4
    mesh = plsc.VectorSubcoreMesh(core_axis_name="core", subcore_axis_name="subcore")
    groups = [(j, k0) for j in range(SC_IDX_CHUNK // win) for k0 in range(0, K, SC_GATHER_PLANES)]

    @pl.kernel(
        out_type=jax.ShapeDtypeStruct((K, T, Wp), U32), mesh=mesh,
        scratch_types=[pltpu.VMEM((K, SC_IDX_CHUNK), I32),
                       pltpu.VMEM((2, SC_GATHER_PLANES, win, Wp), U32),
                       pltpu.SemaphoreType.DMA((2,)), pltpu.SemaphoreType.DMA((2,))])
    def gather(y_hbm, s_hbm, g_hbm, idx_v, buf, sem_g, sem_w):
        base = (lax.axis_index("core") * SC_SUBCORES + lax.axis_index("subcore")) * per

        @pl.loop(0, per // SC_IDX_CHUNK)
        def _(c):
            t0 = base + c * SC_IDX_CHUNK
            pltpu.sync_copy(s_hbm.at[:, pl.ds(t0, SC_IDX_CHUNK)], idx_v)

            def start_reads(g):
                j, k0 = groups[g]
                return [pltpu.async_copy(y_hbm.at[idx_v.at[k0 + i, pl.ds(j * win, win)]],
                                         buf.at[g % 2, i], sem_g.at[g % 2])
                        for i in range(SC_GATHER_PLANES)]

            def start_writes(g):
                j, k0 = groups[g]
                return [pltpu.async_copy(buf.at[g % 2, i], g_hbm.at[k0 + i, pl.ds(t0 + j * win, win)],
                                         sem_w.at[g % 2])
                        for i in range(SC_GATHER_PLANES)]

            reads = start_reads(0)
            writes_prev = []
            for g in range(len(groups)):
                for cp in reads:
                    cp.wait()
                writes = start_writes(g)
                for cp in writes_prev:
                    cp.wait()
                if g + 1 < len(groups):
                    reads = start_reads(g + 1)
                writes_prev = writes
            for cp in writes_prev:
                cp.wait()

    return gather(ys, slot)


def _combine_kernel(yg_ref, gate_ref, h1_ref, swg_ref, swu_ref, swd_ref, lg_ref, lb_ref, *rest):
    out_ref = rest[-1]
    h1 = h1_ref[...]
    xb = h1.astype(BF16)
    g = jnp.dot(xb, swg_ref[...], preferred_element_type=F32)
    u = jnp.dot(xb, swu_ref[...], preferred_element_type=F32)
    hmid = (g * jax.nn.sigmoid(g) * u).astype(BF16)
    shared = jnp.dot(hmid, swd_ref[...], preferred_element_type=F32)

    gates = gate_ref[...]
    tc, wp = yg_ref.shape[1], yg_ref.shape[2]
    f_lo = jnp.zeros((tc, wp), F32)
    f_hi = jnp.zeros((tc, wp), F32)
    for k in range(TOP_K):
        lo, hi = _unpack_bf16_pairs(yg_ref[k])
        gk = gates[:, k:k + 1]
        f_lo = f_lo + gk * lo
        f_hi = f_hi + gk * hi
    routed = jnp.concatenate([f_lo, f_hi], axis=1)
    out_ref[...] = _layer_norm(ALPHA * h1 + (routed + shared), lg_ref[...], lb_ref[...])


def _combine(yg, gate_t, h1, swg, swu, swd, ln_g, ln_b, part, prev_out):
    T, D = h1.shape
    K, Tp, Wp = yg.shape
    tc = min(TD_MOE, Tp)
    off = part * (Tp // tc)
    row = lambda i: (i + off, 0)
    const = lambda i: (0, 0)
    res = lambda a: pl.BlockSpec(a.shape, const)
    in_specs = [
        pl.BlockSpec((K, tc, Wp), lambda i: (0, i, 0)),
        pl.BlockSpec((tc, TOP_K), row),
        pl.BlockSpec((tc, D), row),
        res(swg), res(swu), res(swd),
        pl.BlockSpec((1, D), const),
        pl.BlockSpec((1, D), const),
    ]
    args = [yg, gate_t, h1, swg, swu, swd, ln_g.reshape(1, D), ln_b.reshape(1, D)]
    aliases = {}
    if prev_out is not None:
        in_specs.append(pl.BlockSpec(memory_space=pl.ANY))
        args.append(prev_out)
        aliases = {len(args) - 1: 0}
    return pl.pallas_call(
        _combine_kernel,
        grid=(Tp // tc,),
        in_specs=in_specs,
        out_specs=pl.BlockSpec((tc, D), row),
        out_shape=jax.ShapeDtypeStruct((T, D), F32),
        input_output_aliases=aliases,
        compiler_params=_cparams(1),
        name="combine",
    )(*args)


def _moe_layer(layer, h1, h1p, router_w, router_bias, wg, wu, wd, swg, swu, swd, ln_g, ln_b):
    T, D = h1.shape
    E = N_EXPERTS
    M = T * TOP_K
    eidx, gate, rank, cnt = _router(h1, router_w, router_bias)

    counts = cnt[:, 0].astype(I32)
    padded = (counts + BLK_MOE - 1) // BLK_MOE * BLK_MOE
    pad_end = jnp.cumsum(padded)
    pad_start = pad_end - padded
    nb = -(-(M + E * (BLK_MOE - 1)) // BLK_MOE)
    n_slots = nb * BLK_MOE
    slot = _slots(pad_start.astype(I32), eidx, rank)
    n_valid = (pad_end[-1] // BLK_MOE).astype(I32).reshape(1)
    blk_start = jnp.arange(nb, dtype=I32) * BLK_MOE
    blk_exp = jnp.minimum(jnp.sum((pad_end[None, :] <= blk_start[:, None]).astype(I32), axis=1), E - 1)

    xs = _dispatch(slot, h1p, n_slots)
    ys = _experts(layer, blk_exp, n_valid, xs, wg, wu, wd)
    n_parts = MOE_COMBINE_PARTS if T % (MOE_COMBINE_PARTS * SC_CORES * SC_SUBCORES * SC_IDX_CHUNK) == 0 else 1
    tp = T // n_parts
    gate_t = gate.T
    shared_w = (swg.astype(BF16), swu.astype(BF16), swd.astype(BF16))
    parts = [_gather_rows(slot[:, p * tp:(p + 1) * tp], ys) for p in range(n_parts)]
    out = None
    for p in range(n_parts):
        out = _combine(parts[p], gate_t, h1, *shared_w, ln_g, ln_b, p, out)
    return out


def kernel(x, emb_ln_g, emb_ln_b, w_in, b_gate, lam_q1, lam_k1, lam_q2, lam_k2, subln_g, w_proj_a, na_rpb, w_proj_b, w_out, ln1_g, ln1_b, router_w, router_bias, exp_w_gate, exp_w_up, exp_w_down, sh_w_gate, sh_w_up, sh_w_down, ln2_g, ln2_b):
    B, S, D = x.shape
    T = B * S
    rows = S // GRID_W
    h = x.reshape(T, D)
    for l in range(DEPTH):
        lam_init = 0.8 - 0.6 * math.exp(-0.3 * l)
        outs = _inproj(h, emb_ln_g, emb_ln_b, w_in[l].astype(BF16), apply_ln=(l == 0))
        if l == 0:
            h, outs = outs[0], outs[1:]
        qa, ka, va, qn, kn, vn, ga, gb = outs
        lamv = jnp.stack([lam_q1[l], lam_k1[l], lam_q2[l], lam_k2[l]]).astype(F32)
        oa = _diffattn(qa.reshape(B, S, -1), ka.reshape(B, S, -1), va.reshape(B, S, -1),
                       lamv, subln_g[l], lam_init)
        ob = _natten(qn.reshape(B, S, -1), kn.reshape(B, S, -1), vn.reshape(B, S, -1),
                     _na_bias_table(na_rpb[l], rows))
        h1, h1p = _mixout(oa.reshape(T, -1), ob.reshape(T, -1), ga, gb, h, b_gate[l],
                          w_proj_a[l].astype(BF16), w_proj_b[l].astype(BF16), w_out[l].astype(BF16),
                          ln1_g[l], ln1_b[l])
        h = _moe_layer(l, h1, h1p, router_w[l], router_bias[l], exp_w_gate, exp_w_up, exp_w_down,
                       sh_w_gate[l], sh_w_up[l], sh_w_down[l], ln2_g[l], ln2_b[l])
    return h.reshape(B, S, D)
```

```python
import functools
import math

import numpy as np
import jax
import jax.numpy as jnp
from jax import lax
from jax.experimental import pallas as pl
from jax.experimental.pallas import tpu as pltpu
from jax.experimental.pallas import tpu_sc as plsc

F32 = jnp.float32
BF16 = jnp.bfloat16
U32 = jnp.uint32
I32 = jnp.int32

DA_HEADS = 8
DA_HEAD_DIM = 64
NA_HEADS = 16
NA_HEAD_DIM = 32
GRID_W = 64
WIN_R = 8
WIN_C = 16
N_EXPERTS = 256
TOP_K = 8
N_GROUPS = 8
TOPK_GROUPS = 4
ROUTE_SCALE = 2.5
DEPTH = 2
ALPHA = (2 * DEPTH) ** 0.25
LN_EPS = 1e-5
LOG2E = 1.4426950408889634

LANES = 128
NA_HEADS_PER_BLOCK = LANES // NA_HEAD_DIM
VMEM_LIMIT = 56 * 1024 * 1024
SC_CORES = 2
SC_SUBCORES = 16
SC_IDX_CHUNK = 128
SC_ROW_WIN = 64
SC_GATHER_PLANES = 4

TM_PROJ = 512
TQ_DA = 2048
TQ_SUB_DA = 128
NA_ROWS_PER_ITER = 4
TT_ROUTER = 512
TD_MOE = 256
BLK_MOE = 512
SUB_MOE = 256
MOE_COMBINE_PARTS = 4
NEG_BIG = -1e30


def _cparams(n_axes, flags=None):
    return pltpu.CompilerParams(
        dimension_semantics=("arbitrary",) * n_axes, vmem_limit_bytes=VMEM_LIMIT, flags=flags)


def _layer_norm(x, g, b):
    mu = jnp.mean(x, axis=-1, keepdims=True)
    xc = x - mu
    var = jnp.mean(xc * xc, axis=-1, keepdims=True)
    return xc * lax.rsqrt(var + LN_EPS) * g + b


def _pack_bf16_pairs(y):
    w = y.shape[1] // 2
    lo = lax.bitcast_convert_type(y[:, :w].astype(BF16).astype(F32), U32)
    hi = lax.bitcast_convert_type(y[:, w:].astype(BF16).astype(F32), U32)
    return (hi & jnp.uint32(0xFFFF0000)) | (lo >> 16)


def _unpack_bf16_pairs(u):
    lo = lax.bitcast_convert_type(u << 16, F32)
    hi = lax.bitcast_convert_type(u & jnp.uint32(0xFFFF0000), F32)
    return lo, hi


def _inproj_kernel(apply_ln, seg_widths, seg_scales, x_ref, g_ref, b_ref, w_ref, *out_refs):
    x = x_ref[...]
    if apply_ln:
        x = _layer_norm(x, g_ref[...], b_ref[...])
        out_refs[0][...] = x
        out_refs = out_refs[1:]
    xb = x.astype(BF16)
    off = 0
    for ref, width, scale in zip(out_refs, seg_widths, seg_scales):
        y = jnp.dot(xb, w_ref[:, off:off + width], preferred_element_type=F32)
        if scale != 1.0:
            y = y * scale
        ref[...] = y.astype(BF16)
        off += width


def _inproj(x, ln_g, ln_b, w_in_bf16, apply_ln):
    T, D = x.shape
    da_w = DA_HEADS * 2 * DA_HEAD_DIM
    na_w = NA_HEADS * NA_HEAD_DIM
    seg_widths = (da_w, da_w, da_w, na_w, na_w, na_w, D, D)
    seg_scales = (DA_HEAD_DIM ** -0.5 * LOG2E, 1.0, 1.0, NA_HEAD_DIM ** -0.5 * LOG2E, 1.0, 1.0, 1.0, 1.0)
    tm = min(TM_PROJ, T)
    n_cols = w_in_bf16.shape[1]
    row = lambda i: (i, 0)
    const = lambda i: (0, 0)
    out_shape = [jax.ShapeDtypeStruct((T, w), BF16) for w in seg_widths]
    out_specs = [pl.BlockSpec((tm, w), row) for w in seg_widths]
    if apply_ln:
        out_shape = [jax.ShapeDtypeStruct((T, D), F32)] + out_shape
        out_specs = [pl.BlockSpec((tm, D), row)] + out_specs
    return pl.pallas_call(
        functools.partial(_inproj_kernel, apply_ln, seg_widths, seg_scales),
        grid=(T // tm,),
        in_specs=[
            pl.BlockSpec((tm, D), row),
            pl.BlockSpec((1, D), const),
            pl.BlockSpec((1, D), const),
            pl.BlockSpec((D, n_cols), const, pipeline_mode=pl.Buffered(1)),
        ],
        out_specs=out_specs,
        out_shape=out_shape,
        compiler_params=_cparams(1),
        name="inproj",
    )(x, ln_g.reshape(1, D), ln_b.reshape(1, D), w_in_bf16)


def _diffattn_kernel(lam_init, slopes_ref, q_ref, k_ref, v_ref, lamv_ref, g_ref, o_ref, bias_ref):
    h = pl.program_id(0)
    qi = pl.program_id(1)
    b = pl.program_id(2)
    tq = q_ref.shape[1]
    S = k_ref.shape[1]
    d = DA_HEAD_DIM

    @pl.when(b == 0)
    def _():
        qpos = qi * tq + lax.broadcasted_iota(I32, (tq, S), 0)
        kpos = lax.broadcasted_iota(I32, (tq, S), 1)
        bias_ref[...] = jnp.abs(qpos - kpos).astype(F32) * (-slopes_ref[h])

    lv = lamv_ref[...]
    lam = (jnp.exp(jnp.sum(lv[0:1] * lv[1:2], axis=-1, keepdims=True))
           - jnp.exp(jnp.sum(lv[2:3] * lv[3:4], axis=-1, keepdims=True)) + lam_init)

    k = k_ref[0]
    v1 = jnp.concatenate([v_ref[0], jnp.ones((S, 2 * d), BF16)], axis=1)
    gain = g_ref[...] * (1.0 - lam_init)
    nt = (((1,), (1,)), ((), ()))
    ts = min(TQ_SUB_DA, tq)
    lane = lax.broadcasted_iota(I32, (ts, 2 * d), 1)

    def scores(j):
        q = q_ref[0, j * ts:(j + 1) * ts, :]
        nb = bias_ref[j * ts:(j + 1) * ts, :]
        zero = jnp.zeros_like(q)
        s1 = lax.dot_general(jnp.where(lane < d, q, zero), k, nt, preferred_element_type=F32) + nb
        s2 = lax.dot_general(jnp.where(lane >= d, q, zero), k, nt, preferred_element_type=F32) + nb
        return s1, s2

    def weights(s1, s2):
        return (jnp.exp2((s1 - jnp.max(s1, axis=-1, keepdims=True)).astype(BF16)),
                jnp.exp2((s2 - jnp.max(s2, axis=-1, keepdims=True)).astype(BF16)))

    def finish(j, e1, e2):
        p = jnp.dot(jnp.concatenate([e1, e2], axis=0), v1, preferred_element_type=F32)
        p1, p2 = p[:ts], p[ts:]
        o = p1[:, :2 * d] / p1[:, 2 * d:] - lam * (p2[:, :2 * d] / p2[:, 2 * d:])
        ms = jnp.mean(o * o, axis=-1, keepdims=True)
        o = o * lax.rsqrt(ms + LN_EPS) * gain
        o_ref[0, j * ts:(j + 1) * ts, :] = o.astype(BF16)

    n_sub = tq // ts
    s_next = scores(0)
    e_cur = weights(*s_next)
    s_next = scores(1) if n_sub > 1 else None
    for j in range(n_sub):
        s_after = scores(j + 2) if j + 2 < n_sub else None
        finish(j, *e_cur)
        if s_next is not None:
            e_cur = weights(*s_next)
        s_next = s_after


def _diffattn(qa, ka, va, lamv, subln_g, lam_init):
    B, S, W = qa.shape
    hw = 2 * DA_HEAD_DIM
    tq = min(TQ_DA, S)
    slopes = jnp.asarray(2.0 ** (-8.0 * np.arange(1, DA_HEADS + 1) / DA_HEADS) * LOG2E, F32)
    grid_spec = pltpu.PrefetchScalarGridSpec(
        num_scalar_prefetch=1,
        grid=(DA_HEADS, S // tq, B),
        in_specs=[
            pl.BlockSpec((1, tq, hw), lambda h, qi, b, sl: (b, qi, h)),
            pl.BlockSpec((1, S, hw), lambda h, qi, b, sl: (b, 0, h)),
            pl.BlockSpec((1, S, hw), lambda h, qi, b, sl: (b, 0, h)),
            pl.BlockSpec((4, DA_HEAD_DIM), lambda h, qi, b, sl: (0, 0)),
            pl.BlockSpec((1, hw), lambda h, qi, b, sl: (0, 0)),
        ],
        out_specs=pl.BlockSpec((1, tq, hw), lambda h, qi, b, sl: (b, qi, h)),
        scratch_shapes=[pltpu.VMEM((tq, S), F32)],
    )
    return pl.pallas_call(
        functools.partial(_diffattn_kernel, lam_init),
        grid_spec=grid_spec,
        out_shape=jax.ShapeDtypeStruct((B, S, W), BF16),
        compiler_params=_cparams(3),
        name="diffattn",
    )(slopes, qa, ka, va, lamv, subln_g.reshape(1, hw))


def _na_bias_table(rpb, rows):
    kh = min(WIN_R, rows)
    qcol = np.arange(GRID_W)
    kcol = np.arange(GRID_W)
    cs = np.clip(qcol - WIN_C // 2, 0, GRID_W - WIN_C)
    col_mask = (kcol[None, :] >= cs[:, None]) & (kcol[None, :] < cs[:, None] + WIN_C)
    col_off = np.clip(kcol[None, :] - qcol[:, None] + WIN_C - 1, 0, 2 * WIN_C - 2)
    v = np.arange(kh)
    j = np.arange(kh)
    row_idx = np.clip(j[None, :] - v[:, None] + WIN_R - 1, 0, 2 * WIN_R - 2)
    sel_r = np.zeros((2 * WIN_R - 1, kh * kh), np.float32)
    sel_r[row_idx.reshape(-1), np.arange(kh * kh)] = 1.0
    sel_c = np.zeros((2 * WIN_C - 1, GRID_W * GRID_W), np.float32)
    sel_c[col_off.reshape(-1), np.arange(GRID_W * GRID_W)] = 1.0
    hi = lax.Precision.HIGHEST
    t = jnp.einsum("hrc,rp->hpc", rpb * LOG2E, jnp.asarray(sel_r), precision=hi)
    t = jnp.einsum("hpc,cq->hpq", t, jnp.asarray(sel_c), precision=hi)
    t = t.reshape(NA_HEADS, kh, kh, GRID_W, GRID_W)
    t = jnp.where(jnp.asarray(col_mask)[None, None, None], t, NEG_BIG)
    t = t.transpose(1, 0, 3, 2, 4)
    nblk = NA_HEADS // NA_HEADS_PER_BLOCK
    return t.reshape(kh, nblk, NA_HEADS_PER_BLOCK * GRID_W, kh * GRID_W)


def _natten_kernel(rows, kh, q_ref, k_ref, v_ref, bias_ref, o_ref):
    hid = lax.broadcasted_iota(I32, (GRID_W, LANES), 1) // NA_HEAD_DIM

    def scores(r):
        rs = jnp.clip(r - WIN_R // 2, 0, rows - kh)
        q0 = pl.multiple_of(r * GRID_W, GRID_W)
        k0 = pl.multiple_of(rs * GRID_W, GRID_W)
        qr = q_ref[0, pl.ds(q0, GRID_W), :]
        zero = jnp.zeros_like(qr)
        qq = jnp.concatenate(
            [jnp.where(hid == hh, qr, zero) for hh in range(NA_HEADS_PER_BLOCK)], axis=0)
        kb = k_ref[0, pl.ds(k0, kh * GRID_W), :]
        s = lax.dot_general(qq, kb, (((1,), (1,)), ((), ())), preferred_element_type=F32)
        return s + bias_ref[r - rs, 0]

    def weights(s):
        e = jnp.exp2(s - jnp.max(s, axis=-1, keepdims=True))
        return e.astype(BF16), 1.0 / jnp.sum(e, axis=-1, keepdims=True)

    def finish(r, e, rl):
        rs = jnp.clip(r - WIN_R // 2, 0, rows - kh)
        q0 = pl.multiple_of(r * GRID_W, GRID_W)
        k0 = pl.multiple_of(rs * GRID_W, GRID_W)
        vb = v_ref[0, pl.ds(k0, kh * GRID_W), :]
        oo = jnp.dot(e, vb, preferred_element_type=F32) * rl
        o = jnp.zeros((GRID_W, LANES), F32)
        for hh in range(NA_HEADS_PER_BLOCK):
            o = o + jnp.where(hid == hh, oo[hh * GRID_W:(hh + 1) * GRID_W], 0.0)
        o_ref[0, pl.ds(q0, GRID_W), :] = o.astype(BF16)

    def row_group(g, carry):
        rr = [g * NA_ROWS_PER_ITER + i for i in range(NA_ROWS_PER_ITER)]
        ss = [scores(r) for r in rr]
        ws = [weights(s) for s in ss]
        for r, (e, rl) in zip(rr, ws):
            finish(r, e, rl)
        return carry

    lax.fori_loop(0, rows // NA_ROWS_PER_ITER, row_group, 0)


def _natten(qn, kn, vn, bias_tab):
    B, S, W = qn.shape
    rows = S // GRID_W
    kh = min(WIN_R, rows)
    nblk = W // LANES
    blk = lambda g, b: (b, 0, g)
    return pl.pallas_call(
        functools.partial(_natten_kernel, rows, kh),
        grid=(nblk, B),
        in_specs=[
            pl.BlockSpec((1, S, LANES), blk),
            pl.BlockSpec((1, S, LANES), blk),
            pl.BlockSpec((1, S, LANES), blk),
            pl.BlockSpec((kh, 1, NA_HEADS_PER_BLOCK * GRID_W, kh * GRID_W), lambda g, b: (0, g, 0, 0)),
        ],
        out_specs=pl.BlockSpec((1, S, LANES), blk),
        out_shape=jax.ShapeDtypeStruct((B, S, W), BF16),
        compiler_params=_cparams(2),
        name="natten",
    )(qn, kn, vn, bias_tab)


def _mixout_kernel(oa_ref, ob_ref, ga_ref, gb_ref, h_ref, bg_ref, wa_ref, wb_ref, wo_ref,
                   lg_ref, lb_ref, h1_ref, h1p_ref):
    ya = jnp.dot(oa_ref[...], wa_ref[...], preferred_element_type=F32)
    yb = jnp.dot(ob_ref[...], wb_ref[...], preferred_element_type=F32)
    g_a = jax.nn.sigmoid(ga_ref[...].astype(F32) + bg_ref[0:1, :])
    g_b = jax.nn.sigmoid(gb_ref[...].astype(F32) + bg_ref[1:2, :])
    z = (g_a * ya + g_b * yb).astype(BF16)
    m = jnp.dot(z, wo_ref[...], preferred_element_type=F32)
    y = _layer_norm(ALPHA * h_ref[...] + m, lg_ref[...], lb_ref[...])
    h1_ref[...] = y
    h1p_ref[...] = _pack_bf16_pairs(y)


def _mixout(oa, ob, ga, gb, h, b_gate, wa, wb, wo, ln_g, ln_b):
    T, D = h.shape
    tm = min(TM_PROJ, T)
    row = lambda i: (i, 0)
    const = lambda i: (0, 0)
    res = lambda a: pl.BlockSpec(a.shape, const, pipeline_mode=pl.Buffered(1))
    return pl.pallas_call(
        _mixout_kernel,
        grid=(T // tm,),
        in_specs=[
            pl.BlockSpec((tm, oa.shape[1]), row),
            pl.BlockSpec((tm, ob.shape[1]), row),
            pl.BlockSpec((tm, D), row),
            pl.BlockSpec((tm, D), row),
            pl.BlockSpec((tm, D), row),
            pl.BlockSpec((2, D), const),
            res(wa), res(wb), res(wo),
            pl.BlockSpec((1, D), const),
            pl.BlockSpec((1, D), const),
        ],
        out_specs=[pl.BlockSpec((tm, D), row), pl.BlockSpec((tm, D // 2), row)],
        out_shape=[jax.ShapeDtypeStruct((T, D), F32), jax.ShapeDtypeStruct((T, D // 2), U32)],
        compiler_params=_cparams(1),
        name="mixout",
    )(oa, ob, ga, gb, h, b_gate, wa, wb, wo, ln_g.reshape(1, D), ln_b.reshape(1, D))


def _router_kernel(h_ref, rwt_ref, rb_ref, tri_ref, eidx_ref, gate_ref, rank_ref, cnt_ref, carry_ref):
    i = pl.program_id(0)
    tt = h_ref.shape[0]
    E, G = N_EXPERTS, N_GROUPS
    P = E // G
    neg = -jnp.inf

    @pl.when(i == 0)
    def _():
        carry_ref[...] = jnp.zeros_like(carry_ref)

    hb = h_ref[...].astype(BF16)
    logits = lax.dot_general(rwt_ref[...], hb, (((1,), (1,)), ((), ())), preferred_element_type=F32)
    scores = jax.nn.sigmoid(logits)
    biased = scores + rb_ref[...]
    b3 = biased.reshape(G, P, tt)
    s3 = scores.reshape(G, P, tt)
    pi = lax.broadcasted_iota(I32, (G, P, tt), 1)
    ei = lax.broadcasted_iota(I32, (G, P, tt), 0) * P + pi

    m1 = jnp.max(b3, axis=1, keepdims=True)
    i1 = jnp.min(jnp.where(b3 == m1, pi, P), axis=1, keepdims=True)
    m2 = jnp.max(jnp.where(pi == i1, neg, b3), axis=1, keepdims=True)
    grp = m1 + m2
    gi = lax.broadcasted_iota(I32, (G, 1, tt), 0)
    gsel = jnp.zeros((G, 1, tt), F32)
    for _ in range(TOPK_GROUPS):
        gm = jnp.max(grp, axis=0, keepdims=True)
        gidx = jnp.min(jnp.where(grp == gm, gi, G), axis=0, keepdims=True)
        hit = gi == gidx
        gsel = jnp.where(hit, 1.0, gsel)
        grp = jnp.where(hit, neg, grp)

    cand = jnp.where(gsel > 0.0, b3, neg)
    sel = jnp.zeros((G, P, tt), F32)
    eids, gates = [], []
    for _ in range(TOP_K):
        mk = jnp.max(cand, axis=(0, 1), keepdims=True)
        ik = jnp.min(jnp.where(cand == mk, ei, E), axis=(0, 1), keepdims=True)
        hit = ei == ik
        gates.append(jnp.sum(jnp.where(hit, s3, 0.0), axis=(0, 1), keepdims=True))
        eids.append(ik)
        cand = jnp.where(hit, neg, cand)
        sel = jnp.where(hit, 1.0, sel)

    gsum = gates[0]
    for gk in gates[1:]:
        gsum = gsum + gk
    gscale = ROUTE_SCALE / gsum

    sel2 = sel.reshape(E, tt)
    prefix = jnp.dot(sel2.astype(BF16), tri_ref[...], preferred_element_type=F32)
    base3 = (prefix + carry_ref[...]).reshape(G, P, tt)
    ranks = [jnp.sum(jnp.where(ei == ik, base3, 0.0), axis=(0, 1), keepdims=True) for ik in eids]
    carry_ref[...] = carry_ref[...] + jnp.sum(sel2, axis=1, keepdims=True)
    cnt_ref[...] = carry_ref[...]

    eidx_ref[...] = jnp.concatenate([x.reshape(1, tt) for x in eids], axis=0)
    gate_ref[...] = jnp.concatenate([(g * gscale).reshape(1, tt) for g in gates], axis=0)
    rank_ref[...] = jnp.concatenate([x.reshape(1, tt) for x in ranks], axis=0).astype(I32)


def _router(h1, router_w, router_bias):
    T, D = h1.shape
    E = N_EXPERTS
    tt = min(TT_ROUTER, T)
    rwt = router_w.T.astype(BF16)
    tri = jnp.asarray(np.triu(np.ones((tt, tt), np.float32), k=1), BF16)
    const = lambda i: (0, 0)
    col = lambda i: (0, i)
    return pl.pallas_call(
        _router_kernel,
        grid=(T // tt,),
        in_specs=[
            pl.BlockSpec((tt, D), lambda i: (i, 0)),
            pl.BlockSpec((E, D), const),
            pl.BlockSpec((E, 1), const),
            pl.BlockSpec((tt, tt), const),
        ],
        out_specs=[
            pl.BlockSpec((TOP_K, tt), col),
            pl.BlockSpec((TOP_K, tt), col),
            pl.BlockSpec((TOP_K, tt), col),
            pl.BlockSpec((E, 1), const),
        ],
        out_shape=[
            jax.ShapeDtypeStruct((TOP_K, T), I32),
            jax.ShapeDtypeStruct((TOP_K, T), F32),
            jax.ShapeDtypeStruct((TOP_K, T), I32),
            jax.ShapeDtypeStruct((E, 1), F32),
        ],
        scratch_shapes=[pltpu.VMEM((E, 1), F32)],
        compiler_params=_cparams(1),
        name="router",
    )(h1, rwt, router_bias.reshape(E, 1).astype(F32), tri)


def _slots_kernel(pstart_ref, eidx_ref, rank_ref, slot_ref):
    eidx = eidx_ref[...]

    def add_expert(e, acc):
        return acc + jnp.where(eidx == e, pstart_ref[e], 0)

    slot_ref[...] = lax.fori_loop(0, N_EXPERTS, add_expert, rank_ref[...], unroll=8)


def _slots(pad_start, eidx, rank):
    K, T = eidx.shape
    tt = min(2048, T)
    col = lambda i, ps: (0, i)
    grid_spec = pltpu.PrefetchScalarGridSpec(
        num_scalar_prefetch=1,
        grid=(T // tt,),
        in_specs=[pl.BlockSpec((K, tt), col), pl.BlockSpec((K, tt), col)],
        out_specs=pl.BlockSpec((K, tt), col),
    )
    return pl.pallas_call(
        _slots_kernel,
        grid_spec=grid_spec,
        out_shape=jax.ShapeDtypeStruct((K, T), I32),
        compiler_params=_cparams(1),
        name="slots",
    )(pad_start, eidx, rank)


def _dispatch(slot, h1p, n_slots):
    T, Wp = h1p.shape
    n_workers = SC_CORES * SC_SUBCORES
    per = T // n_workers
    assert per % SC_IDX_CHUNK == 0, (T, n_workers)
    mesh = plsc.VectorSubcoreMesh(core_axis_name="core", subcore_axis_name="subcore")

    @pl.kernel(
        out_type=jax.ShapeDtypeStruct((n_slots, Wp), U32), mesh=mesh,
        scratch_types=[pltpu.VMEM((TOP_K, SC_IDX_CHUNK), I32),
                       pltpu.VMEM((SC_IDX_CHUNK // SC_ROW_WIN, SC_ROW_WIN, Wp), U32),
                       pltpu.SemaphoreType.DMA((SC_IDX_CHUNK // SC_ROW_WIN,)), pltpu.SemaphoreType.DMA])
    def dispatch(x_hbm, s_hbm, o_hbm, idx_v, x_v, sem_r, sem_w):
        base = (lax.axis_index("core") * SC_SUBCORES + lax.axis_index("subcore")) * per
        n_win = SC_IDX_CHUNK // SC_ROW_WIN

        @pl.loop(0, per // SC_IDX_CHUNK)
        def _(c):
            t0 = base + c * SC_IDX_CHUNK
            reads = [pltpu.async_copy(x_hbm.at[pl.ds(t0 + j * SC_ROW_WIN, SC_ROW_WIN)], x_v.at[j], sem_r.at[j])
                     for j in range(n_win)]
            pltpu.sync_copy(s_hbm.at[:, pl.ds(t0, SC_IDX_CHUNK)], idx_v)
            copies = []
            for j in range(n_win):
                reads[j].wait()
                copies += [
                    pltpu.async_copy(x_v.at[j], o_hbm.at[idx_v.at[k, pl.ds(j * SC_ROW_WIN, SC_ROW_WIN)]], sem_w)
                    for k in range(TOP_K)]
            for cp in copies:
                cp.wait()

    return dispatch(h1p, slot)


def _experts_kernel(be_ref, nv_ref, xs_ref, wg_ref, wu_ref, wd_ref, ys_ref, wg_s, wu_s, wd_s):
    i = pl.program_id(0)

    @pl.when(i < nv_ref[0])
    def _():
        prev = be_ref[jnp.maximum(i - 1, 0)]

        @pl.when((i == 0) | (be_ref[i] != prev))
        def _():
            wg_s[...] = wg_ref[0].astype(BF16)
            wu_s[...] = wu_ref[0].astype(BF16)
            wd_s[...] = wd_ref[0].astype(BF16)

        sub = min(SUB_MOE, xs_ref.shape[0])

        def up(j):
            lo, hi = _unpack_bf16_pairs(xs_ref[j * sub:(j + 1) * sub, :])
            x = jnp.concatenate([lo.astype(BF16), hi.astype(BF16)], axis=1)
            return (jnp.dot(x, wg_s[...], preferred_element_type=F32),
                    jnp.dot(x, wu_s[...], preferred_element_type=F32))

        def down(j, g, u):
            hmid = (g * jax.nn.sigmoid(g) * u).astype(BF16)
            y = jnp.dot(hmid, wd_s[...], preferred_element_type=F32)
            ys_ref[j * sub:(j + 1) * sub, :] = _pack_bf16_pairs(y)

        n_sub = xs_ref.shape[0] // sub
        pending = up(0)
        for j in range(n_sub):
            nxt = up(j + 1) if j + 1 < n_sub else None
            down(j, *pending)
            pending = nxt


def _experts(layer, blk_exp, n_valid, xs, wg, wu, wd):
    n_slots, Wp = xs.shape
    _, E, D, F = wg.shape
    nb = n_slots // BLK_MOE

    def xmap(i, be, nv):
        return (jnp.minimum(i, nv[0] - 1), 0)

    def wmap(i, be, nv):
        return (layer, be[i], 0, 0)

    grid_spec = pltpu.PrefetchScalarGridSpec(
        num_scalar_prefetch=2,
        grid=(nb,),
        in_specs=[
            pl.BlockSpec((BLK_MOE, Wp), xmap),
            pl.BlockSpec((None, 1, D, F), wmap),
            pl.BlockSpec((None, 1, D, F), wmap),
            pl.BlockSpec((None, 1, F, D), wmap),
        ],
        out_specs=pl.BlockSpec((BLK_MOE, Wp), xmap),
        scratch_shapes=[pltpu.VMEM((D, F), BF16), pltpu.VMEM((D, F), BF16), pltpu.VMEM((F, D), BF16)],
    )
    return pl.pallas_call(
        _experts_kernel,
        grid_spec=grid_spec,
        out_shape=jax.ShapeDtypeStruct((n_slots, Wp), U32),
        compiler_params=_cparams(1),
        name="experts",
    )(blk_exp, n_valid, xs, wg, wu, wd)


def _gather_rows(slot, ys):
    K, T = slot.shape
    Wp = ys.shape[1]
    n_workers = SC_CORES * SC_SUBCORES
    per = T // n_workers
    assert per % SC_IDX_CHUNK == 0 and K % SC_GATHER_PLANES == 0, (T, K)
    win = SC_ROW_WIN // 4
    mesh = plsc.VectorSubcoreMesh(core_axis_name="core", subcore_axis_name="subcore")
    groups = [(j, k0) for j in range(SC_IDX_CHUNK // win) for k0 in range(0, K, SC_GATHER_PLANES)]

    @pl.kernel(
        out_type=jax.ShapeDtypeStruct((K, T, Wp), U32), mesh=mesh,
        scratch_types=[pltpu.VMEM((K, SC_IDX_CHUNK), I32),
                       pltpu.VMEM((2, SC_GATHER_PLANES, win, Wp), U32),
                       pltpu.SemaphoreType.DMA((2,)), pltpu.SemaphoreType.DMA((2,))])
    def gather(y_hbm, s_hbm, g_hbm, idx_v, buf, sem_g, sem_w):
        base = (lax.axis_index("core") * SC_SUBCORES + lax.axis_index("subcore")) * per

        @pl.loop(0, per // SC_IDX_CHUNK)
        def _(c):
            t0 = base + c * SC_IDX_CHUNK
            pltpu.sync_copy(s_hbm.at[:, pl.ds(t0, SC_IDX_CHUNK)], idx_v)

            def start_reads(g):
                j, k0 = groups[g]
                return [pltpu.async_copy(y_hbm.at[idx_v.at[k0 + i, pl.ds(j * win, win)]],
                                         buf.at[g % 2, i], sem_g.at[g % 2])
                        for i in range(SC_GATHER_PLANES)]

            def start_writes(g):
                j, k0 = groups[g]
                return [pltpu.async_copy(buf.at[g % 2, i], g_hbm.at[k0 + i, pl.ds(t0 + j * win, win)],
                                         sem_w.at[g % 2])
                        for i in range(SC_GATHER_PLANES)]

            reads = start_reads(0)
            writes_prev = []
            for g in range(len(groups)):
                for cp in reads:
                    cp.wait()
                writes = start_writes(g)
                for cp in writes_prev:
                    cp.wait()
                if g + 1 < len(groups):
                    reads = start_reads(g + 1)
                writes_prev = writes
            for cp in writes_prev:
                cp.wait()

    return gather(ys, slot)


def _combine_kernel(yg_ref, gate_ref, h1_ref, swg_ref, swu_ref, swd_ref, lg_ref, lb_ref, *rest):
    out_ref = rest[-1]
    h1 = h1_ref[...]
    xb = h1.astype(BF16)
    g = jnp.dot(xb, swg_ref[...], preferred_element_type=F32)
    u = jnp.dot(xb, swu_ref[...], preferred_element_type=F32)
    hmid = (g * jax.nn.sigmoid(g) * u).astype(BF16)
    shared = jnp.dot(hmid, swd_ref[...], preferred_element_type=F32)

    gates = gate_ref[...]
    tc, wp = yg_ref.shape[1], yg_ref.shape[2]
    f_lo = jnp.zeros((tc, wp), F32)
    f_hi = jnp.zeros((tc, wp), F32)
    for k in range(TOP_K):
        lo, hi = _unpack_bf16_pairs(yg_ref[k])
        gk = gates[:, k:k + 1]
        f_lo = f_lo + gk * lo
        f_hi = f_hi + gk * hi
    routed = jnp.concatenate([f_lo, f_hi], axis=1)
    out_ref[...] = _layer_norm(ALPHA * h1 + (routed + shared), lg_ref[...], lb_ref[...])


def _combine(yg, gate_t, h1, swg, swu, swd, ln_g, ln_b, part, prev_out):
    T, D = h1.shape
    K, Tp, Wp = yg.shape
    tc = min(TD_MOE, Tp)
    off = part * (Tp // tc)
    row = lambda i: (i + off, 0)
    const = lambda i: (0, 0)
    res = lambda a: pl.BlockSpec(a.shape, const)
    in_specs = [
        pl.BlockSpec((K, tc, Wp), lambda i: (0, i, 0)),
        pl.BlockSpec((tc, TOP_K), row),
        pl.BlockSpec((tc, D), row),
        res(swg), res(swu), res(swd),
        pl.BlockSpec((1, D), const),
        pl.BlockSpec((1, D), const),
    ]
    args = [yg, gate_t, h1, swg, swu, swd, ln_g.reshape(1, D), ln_b.reshape(1, D)]
    aliases = {}
    if prev_out is not None:
        in_specs.append(pl.BlockSpec(memory_space=pl.ANY))
        args.append(prev_out)
        aliases = {len(args) - 1: 0}
    return pl.pallas_call(
        _combine_kernel,
        grid=(Tp // tc,),
        in_specs=in_specs,
        out_specs=pl.BlockSpec((tc, D), row),
        out_shape=jax.ShapeDtypeStruct((T, D), F32),
        input_output_aliases=aliases,
        compiler_params=_cparams(1),
        name="combine",
    )(*args)


def _moe_layer(layer, h1, h1p, router_w, router_bias, wg, wu, wd, swg, swu, swd, ln_g, ln_b):
    T, D = h1.shape
    E = N_EXPERTS
    M = T * TOP_K
    eidx, gate, rank, cnt = _router(h1, router_w, router_bias)

    counts = cnt[:, 0].astype(I32)
    padded = (counts + BLK_MOE - 1) // BLK_MOE * BLK_MOE
    pad_end = jnp.cumsum(padded)
    pad_start = pad_end - padded
    nb = -(-(M + E * (BLK_MOE - 1)) // BLK_MOE)
    n_slots = nb * BLK_MOE
    slot = _slots(pad_start.astype(I32), eidx, rank)
    n_valid = (pad_end[-1] // BLK_MOE).astype(I32).reshape(1)
    blk_start = jnp.arange(nb, dtype=I32) * BLK_MOE
    blk_exp = jnp.minimum(jnp.sum((pad_end[None, :] <= blk_start[:, None]).astype(I32), axis=1), E - 1)

    xs = _dispatch(slot, h1p, n_slots)
    ys = _experts(layer, blk_exp, n_valid, xs, wg, wu, wd)
    n_parts = MOE_COMBINE_PARTS if T % (MOE_COMBINE_PARTS * SC_CORES * SC_SUBCORES * SC_IDX_CHUNK) == 0 else 1
    tp = T // n_parts
    gate_t = gate.T
    shared_w = (swg.astype(BF16), swu.astype(BF16), swd.astype(BF16))
    parts = [_gather_rows(slot[:, p * tp:(p + 1) * tp], ys) for p in range(n_parts)]
    out = None
    for p in range(n_parts):
        out = _combine(parts[p], gate_t, h1, *shared_w, ln_g, ln_b, p, out)
    return out


def kernel(x, emb_ln_g, emb_ln_b, w_in, b_gate, lam_q1, lam_k1, lam_q2, lam_k2, subln_g, w_proj_a, na_rpb, w_proj_b, w_out, ln1_g, ln1_b, router_w, router_bias, exp_w_gate, exp_w_up, exp_w_down, sh_w_gate, sh_w_up, sh_w_down, ln2_g, ln2_b):
    B, S, D = x.shape
    T = B * S
    rows = S // GRID_W
    h = x.reshape(T, D)
    for l in range(DEPTH):
        lam_init = 0.8 - 0.6 * math.exp(-0.3 * l)
        outs = _inproj(h, emb_ln_g, emb_ln_b, w_in[l].astype(BF16), apply_ln=(l == 0))
        if l == 0:
            h, outs = outs[0], outs[1:]
        qa, ka, va, qn, kn, vn, ga, gb = outs
        lamv = jnp.stack([lam_q1[l], lam_k1[l], lam_q2[l], lam_k2[l]]).astype(F32)
        oa = _diffattn(qa.reshape(B, S, -1), ka.reshape(B, S, -1), va.reshape(B, S, -1),
                       lamv, subln_g[l], lam_init)
        ob = _natten(qn.reshape(B, S, -1), kn.reshape(B, S, -1), vn.reshape(B, S, -1),
                     _na_bias_table(na_rpb[l], rows))
        h1, h1p = _mixout(oa.reshape(T, -1), ob.reshape(T, -1), ga, gb, h, b_gate[l],
                          w_proj_a[l].astype(BF16), w_proj_b[l].astype(BF16), w_out[l].astype(BF16),
                          ln1_g[l], ln1_b[l])
        h = _moe_layer(l, h1, h1p, router_w[l], router_bias[l], exp_w_gate, exp_w_up, exp_w_down,
                       sh_w_gate[l], sh_w_up[l], sh_w_down[l], ln2_g[l], ln2_b[l])
    return h.reshape(B, S, D)
```

```python
import functools
import math

import numpy as np
import jax
import jax.numpy as jnp
from jax import lax
from jax.experimental import pallas as pl
from jax.experimental.pallas import tpu as pltpu
from jax.experimental.pallas import tpu_sc as plsc

F32 = jnp.float32
BF16 = jnp.bfloat16
U32 = jnp.uint32
I32 = jnp.int32

DA_HEADS = 8
DA_HEAD_DIM = 64
NA_HEADS = 16
NA_HEAD_DIM = 32
GRID_W = 64
WIN_R = 8
WIN_C = 16
N_EXPERTS = 256
TOP_K = 8
N_GROUPS = 8
TOPK_GROUPS = 4
ROUTE_SCALE = 2.5
DEPTH = 2
ALPHA = (2 * DEPTH) ** 0.25
LN_EPS = 1e-5
LOG2E = 1.4426950408889634

LANES = 128
NA_HEADS_PER_BLOCK = LANES // NA_HEAD_DIM
VMEM_LIMIT = 56 * 1024 * 1024
SC_CORES = 2
SC_SUBCORES = 16
SC_IDX_CHUNK = 128
SC_ROW_WIN = 64
SC_GATHER_PLANES = 4

TM_PROJ = 512
TM_MIX = 1024
TQ_DA = 2048
TQ_SUB_DA = 128
NA_ROWS_PER_ITER = 4
TT_ROUTER = 512
TD_MOE = 512
BLK_MOE = 512
SUB_MOE = 256
MOE_COMBINE_PARTS = 4
NEG_BIG = -1e30


def _cparams(n_axes, flags=None):
    return pltpu.CompilerParams(
        dimension_semantics=("arbitrary",) * n_axes, vmem_limit_bytes=VMEM_LIMIT, flags=flags)


def _layer_norm(x, g, b):
    mu = jnp.mean(x, axis=-1, keepdims=True)
    xc = x - mu
    var = jnp.mean(xc * xc, axis=-1, keepdims=True)
    return xc * lax.rsqrt(var + LN_EPS) * g + b


def _pack_bf16_pairs(y):
    w = y.shape[1] // 2
    lo = lax.bitcast_convert_type(y[:, :w].astype(BF16).astype(F32), U32)
    hi = lax.bitcast_convert_type(y[:, w:].astype(BF16).astype(F32), U32)
    return (hi & jnp.uint32(0xFFFF0000)) | (lo >> 16)


def _unpack_bf16_pairs(u):
    lo = lax.bitcast_convert_type(u << 16, F32)
    hi = lax.bitcast_convert_type(u & jnp.uint32(0xFFFF0000), F32)
    return lo, hi


def _inproj_kernel(apply_ln, seg_widths, seg_scales, x_ref, g_ref, b_ref, w_ref, *out_refs):
    x = x_ref[...]
    if apply_ln:
        x = _layer_norm(x, g_ref[...], b_ref[...])
        out_refs[0][...] = x
        out_refs = out_refs[1:]
    xb = x.astype(BF16)
    off = 0
    for ref, width, scale in zip(out_refs, seg_widths, seg_scales):
        y = jnp.dot(xb, w_ref[:, off:off + width], preferred_element_type=F32)
        if scale != 1.0:
            y = y * scale
        ref[...] = y.astype(BF16)
        off += width


def _inproj(x, ln_g, ln_b, w_in_bf16, apply_ln):
    T, D = x.shape
    da_w = DA_HEADS * 2 * DA_HEAD_DIM
    na_w = NA_HEADS * NA_HEAD_DIM
    seg_widths = (da_w, da_w, da_w, na_w, na_w, na_w, D, D)
    seg_scales = (DA_HEAD_DIM ** -0.5 * LOG2E, 1.0, 1.0, NA_HEAD_DIM ** -0.5 * LOG2E, 1.0, 1.0, 1.0, 1.0)
    tm = min(TM_PROJ, T)
    n_cols = w_in_bf16.shape[1]
    row = lambda i: (i, 0)
    const = lambda i: (0, 0)
    out_shape = [jax.ShapeDtypeStruct((T, w), BF16) for w in seg_widths]
    out_specs = [pl.BlockSpec((tm, w), row) for w in seg_widths]
    if apply_ln:
        out_shape = [jax.ShapeDtypeStruct((T, D), F32)] + out_shape
        out_specs = [pl.BlockSpec((tm, D), row)] + out_specs
    return pl.pallas_call(
        functools.partial(_inproj_kernel, apply_ln, seg_widths, seg_scales),
        grid=(T // tm,),
        in_specs=[
            pl.BlockSpec((tm, D), row),
            pl.BlockSpec((1, D), const),
            pl.BlockSpec((1, D), const),
            pl.BlockSpec((D, n_cols), const, pipeline_mode=pl.Buffered(1)),
        ],
        out_specs=out_specs,
        out_shape=out_shape,
        compiler_params=_cparams(1),
        name="inproj",
    )(x, ln_g.reshape(1, D), ln_b.reshape(1, D), w_in_bf16)


def _diffattn_kernel(lam_init, slopes_ref, q_ref, k_ref, v_ref, lamv_ref, g_ref, o_ref, bias_ref):
    h = pl.program_id(0)
    qi = pl.program_id(1)
    b = pl.program_id(2)
    tq = q_ref.shape[1]
    S = k_ref.shape[1]
    d = DA_HEAD_DIM

    @pl.when(b == 0)
    def _():
        qpos = qi * tq + lax.broadcasted_iota(I32, (tq, S), 0)
        kpos = lax.broadcasted_iota(I32, (tq, S), 1)
        bias_ref[...] = jnp.abs(qpos - kpos).astype(F32) * (-slopes_ref[h])

    lv = lamv_ref[...]
    lam = (jnp.exp(jnp.sum(lv[0:1] * lv[1:2], axis=-1, keepdims=True))
           - jnp.exp(jnp.sum(lv[2:3] * lv[3:4], axis=-1, keepdims=True)) + lam_init)

    k = k_ref[0]
    v1 = jnp.concatenate([v_ref[0], jnp.ones((S, 2 * d), BF16)], axis=1)
    gain = g_ref[...] * (1.0 - lam_init)
    nt = (((1,), (1,)), ((), ()))
    ts = min(TQ_SUB_DA, tq)
    lane = lax.broadcasted_iota(I32, (ts, 2 * d), 1)

    def scores(j):
        q = q_ref[0, j * ts:(j + 1) * ts, :]
        nb = bias_ref[j * ts:(j + 1) * ts, :]
        zero = jnp.zeros_like(q)
        s1 = lax.dot_general(jnp.where(lane < d, q, zero), k, nt, preferred_element_type=F32) + nb
        s2 = lax.dot_general(jnp.where(lane >= d, q, zero), k, nt, preferred_element_type=F32) + nb
        return s1, s2

    def weights(s1, s2):
        return (jnp.exp2((s1 - jnp.max(s1, axis=-1, keepdims=True)).astype(BF16)),
                jnp.exp2((s2 - jnp.max(s2, axis=-1, keepdims=True)).astype(BF16)))

    def finish(j, e1, e2):
        p = jnp.dot(jnp.concatenate([e1, e2], axis=0), v1, preferred_element_type=F32)
        p1, p2 = p[:ts], p[ts:]
        o = p1[:, :2 * d] / p1[:, 2 * d:] - lam * (p2[:, :2 * d] / p2[:, 2 * d:])
        ms = jnp.mean(o * o, axis=-1, keepdims=True)
        o = o * lax.rsqrt(ms + LN_EPS) * gain
        o_ref[0, j * ts:(j + 1) * ts, :] = o.astype(BF16)

    n_sub = tq // ts
    s_next = scores(0)
    e_cur = weights(*s_next)
    s_next = scores(1) if n_sub > 1 else None
    for j in range(n_sub):
        s_after = scores(j + 2) if j + 2 < n_sub else None
        finish(j, *e_cur)
        if s_next is not None:
            e_cur = weights(*s_next)
        s_next = s_after


def _diffattn(qa, ka, va, lamv, subln_g, lam_init):
    B, S, W = qa.shape
    hw = 2 * DA_HEAD_DIM
    tq = min(TQ_DA, S)
    slopes = jnp.asarray(2.0 ** (-8.0 * np.arange(1, DA_HEADS + 1) / DA_HEADS) * LOG2E, F32)
    grid_spec = pltpu.PrefetchScalarGridSpec(
        num_scalar_prefetch=1,
        grid=(DA_HEADS, S // tq, B),
        in_specs=[
            pl.BlockSpec((1, tq, hw), lambda h, qi, b, sl: (b, qi, h)),
            pl.BlockSpec((1, S, hw), lambda h, qi, b, sl: (b, 0, h)),
            pl.BlockSpec((1, S, hw), lambda h, qi, b, sl: (b, 0, h)),
            pl.BlockSpec((4, DA_HEAD_DIM), lambda h, qi, b, sl: (0, 0)),
            pl.BlockSpec((1, hw), lambda h, qi, b, sl: (0, 0)),
        ],
        out_specs=pl.BlockSpec((1, tq, hw), lambda h, qi, b, sl: (b, qi, h)),
        scratch_shapes=[pltpu.VMEM((tq, S), F32)],
    )
    return pl.pallas_call(
        functools.partial(_diffattn_kernel, lam_init),
        grid_spec=grid_spec,
        out_shape=jax.ShapeDtypeStruct((B, S, W), BF16),
        compiler_params=_cparams(3),
        name="diffattn",
    )(slopes, qa, ka, va, lamv, subln_g.reshape(1, hw))


def _na_bias_table(rpb, rows):
    kh = min(WIN_R, rows)
    qcol = np.arange(GRID_W)
    kcol = np.arange(GRID_W)
    cs = np.clip(qcol - WIN_C // 2, 0, GRID_W - WIN_C)
    col_mask = (kcol[None, :] >= cs[:, None]) & (kcol[None, :] < cs[:, None] + WIN_C)
    col_off = np.clip(kcol[None, :] - qcol[:, None] + WIN_C - 1, 0, 2 * WIN_C - 2)
    v = np.arange(kh)
    j = np.arange(kh)
    row_idx = np.clip(j[None, :] - v[:, None] + WIN_R - 1, 0, 2 * WIN_R - 2)
    sel_r = np.zeros((2 * WIN_R - 1, kh * kh), np.float32)
    sel_r[row_idx.reshape(-1), np.arange(kh * kh)] = 1.0
    sel_c = np.zeros((2 * WIN_C - 1, GRID_W * GRID_W), np.float32)
    sel_c[col_off.reshape(-1), np.arange(GRID_W * GRID_W)] = 1.0
    hi = lax.Precision.HIGHEST
    t = jnp.einsum("hrc,rp->hpc", rpb * LOG2E, jnp.asarray(sel_r), precision=hi)
    t = jnp.einsum("hpc,cq->hpq", t, jnp.asarray(sel_c), precision=hi)
    t = t.reshape(NA_HEADS, kh, kh, GRID_W, GRID_W)
    t = jnp.where(jnp.asarray(col_mask)[None, None, None], t, NEG_BIG)
    t = t.transpose(1, 0, 3, 2, 4)
    nblk = NA_HEADS // NA_HEADS_PER_BLOCK
    return t.reshape(kh, nblk, NA_HEADS_PER_BLOCK * GRID_W, kh * GRID_W)


def _natten_kernel(rows, kh, q_ref, k_ref, v_ref, bias_ref, o_ref):
    hid = lax.broadcasted_iota(I32, (GRID_W, LANES), 1) // NA_HEAD_DIM

    def scores(r):
        rs = jnp.clip(r - WIN_R // 2, 0, rows - kh)
        q0 = pl.multiple_of(r * GRID_W, GRID_W)
        k0 = pl.multiple_of(rs * GRID_W, GRID_W)
        qr = q_ref[0, pl.ds(q0, GRID_W), :]
        zero = jnp.zeros_like(qr)
        qq = jnp.concatenate(
            [jnp.where(hid == hh, qr, zero) for hh in range(NA_HEADS_PER_BLOCK)], axis=0)
        kb = k_ref[0, pl.ds(k0, kh * GRID_W), :]
        s = lax.dot_general(qq, kb, (((1,), (1,)), ((), ())), preferred_element_type=F32)
        return s + bias_ref[r - rs, 0]

    def weights(s):
        e = jnp.exp2(s - jnp.max(s, axis=-1, keepdims=True))
        return e.astype(BF16), 1.0 / jnp.sum(e, axis=-1, keepdims=True)

    def finish(r, e, rl):
        rs = jnp.clip(r - WIN_R // 2, 0, rows - kh)
        q0 = pl.multiple_of(r * GRID_W, GRID_W)
        k0 = pl.multiple_of(rs * GRID_W, GRID_W)
        vb = v_ref[0, pl.ds(k0, kh * GRID_W), :]
        oo = jnp.dot(e, vb, preferred_element_type=F32) * rl
        o = jnp.zeros((GRID_W, LANES), F32)
        for hh in range(NA_HEADS_PER_BLOCK):
            o = o + jnp.where(hid == hh, oo[hh * GRID_W:(hh + 1) * GRID_W], 0.0)
        o_ref[0, pl.ds(q0, GRID_W), :] = o.astype(BF16)

    def row_group(g, carry):
        rr = [g * NA_ROWS_PER_ITER + i for i in range(NA_ROWS_PER_ITER)]
        ss = [scores(r) for r in rr]
        ws = [weights(s) for s in ss]
        for r, (e, rl) in zip(rr, ws):
            finish(r, e, rl)
        return carry

    lax.fori_loop(0, rows // NA_ROWS_PER_ITER, row_group, 0)


def _natten(qn, kn, vn, bias_tab):
    B, S, W = qn.shape
    rows = S // GRID_W
    kh = min(WIN_R, rows)
    nblk = W // LANES
    blk = lambda g, b: (b, 0, g)
    return pl.pallas_call(
        functools.partial(_natten_kernel, rows, kh),
        grid=(nblk, B),
        in_specs=[
            pl.BlockSpec((1, S, LANES), blk),
            pl.BlockSpec((1, S, LANES), blk),
            pl.BlockSpec((1, S, LANES), blk),
            pl.BlockSpec((kh, 1, NA_HEADS_PER_BLOCK * GRID_W, kh * GRID_W), lambda g, b: (0, g, 0, 0)),
        ],
        out_specs=pl.BlockSpec((1, S, LANES), blk),
        out_shape=jax.ShapeDtypeStruct((B, S, W), BF16),
        compiler_params=_cparams(2),
        name="natten",
    )(qn, kn, vn, bias_tab)


def _mixout_kernel(oa_ref, ob_ref, ga_ref, gb_ref, h_ref, bg_ref, wa_ref, wb_ref, wo_ref,
                   lg_ref, lb_ref, h1_ref, h1p_ref):
    ya = jnp.dot(oa_ref[...], wa_ref[...], preferred_element_type=F32)
    yb = jnp.dot(ob_ref[...], wb_ref[...], preferred_element_type=F32)
    g_a = jax.nn.sigmoid(ga_ref[...].astype(F32) + bg_ref[0:1, :])
    g_b = jax.nn.sigmoid(gb_ref[...].astype(F32) + bg_ref[1:2, :])
    z = (g_a * ya + g_b * yb).astype(BF16)
    m = jnp.dot(z, wo_ref[...], preferred_element_type=F32)
    y = _layer_norm(ALPHA * h_ref[...] + m, lg_ref[...], lb_ref[...])
    h1_ref[...] = y
    h1p_ref[...] = _pack_bf16_pairs(y)


def _mixout(oa, ob, ga, gb, h, b_gate, wa, wb, wo, ln_g, ln_b):
    T, D = h.shape
    tm = min(TM_MIX, T)
    row = lambda i: (i, 0)
    const = lambda i: (0, 0)
    res = lambda a: pl.BlockSpec(a.shape, const, pipeline_mode=pl.Buffered(1))
    return pl.pallas_call(
        _mixout_kernel,
        grid=(T // tm,),
        in_specs=[
            pl.BlockSpec((tm, oa.shape[1]), row),
            pl.BlockSpec((tm, ob.shape[1]), row),
            pl.BlockSpec((tm, D), row),
            pl.BlockSpec((tm, D), row),
            pl.BlockSpec((tm, D), row),
            pl.BlockSpec((2, D), const),
            res(wa), res(wb), res(wo),
            pl.BlockSpec((1, D), const),
            pl.BlockSpec((1, D), const),
        ],
        out_specs=[pl.BlockSpec((tm, D), row), pl.BlockSpec((tm, D // 2), row)],
        out_shape=[jax.ShapeDtypeStruct((T, D), F32), jax.ShapeDtypeStruct((T, D // 2), U32)],
        compiler_params=_cparams(1),
        name="mixout",
    )(oa, ob, ga, gb, h, b_gate, wa, wb, wo, ln_g.reshape(1, D), ln_b.reshape(1, D))


def _router_kernel(h_ref, rwt_ref, rb_ref, tri_ref, eidx_ref, gate_ref, rank_ref, cnt_ref, carry_ref):
    i = pl.program_id(0)
    tt = h_ref.shape[0]
    E, G = N_EXPERTS, N_GROUPS
    P = E // G
    neg = -jnp.inf

    @pl.when(i == 0)
    def _():
        carry_ref[...] = jnp.zeros_like(carry_ref)

    hb = h_ref[...].astype(BF16)
    logits = lax.dot_general(rwt_ref[...], hb, (((1,), (1,)), ((), ())), preferred_element_type=F32)
    scores = jax.nn.sigmoid(logits)
    biased = scores + rb_ref[...]
    b3 = biased.reshape(G, P, tt)
    s3 = scores.reshape(G, P, tt)
    pi = lax.broadcasted_iota(I32, (G, P, tt), 1)
    ei = lax.broadcasted_iota(I32, (G, P, tt), 0) * P + pi

    m1 = jnp.max(b3, axis=1, keepdims=True)
    i1 = jnp.min(jnp.where(b3 == m1, pi, P), axis=1, keepdims=True)
    m2 = jnp.max(jnp.where(pi == i1, neg, b3), axis=1, keepdims=True)
    grp = m1 + m2
    gi = lax.broadcasted_iota(I32, (G, 1, tt), 0)
    gsel = jnp.zeros((G, 1, tt), F32)
    for _ in range(TOPK_GROUPS):
        gm = jnp.max(grp, axis=0, keepdims=True)
        gidx = jnp.min(jnp.where(grp == gm, gi, G), axis=0, keepdims=True)
        hit = gi == gidx
        gsel = jnp.where(hit, 1.0, gsel)
        grp = jnp.where(hit, neg, grp)

    cand = jnp.where(gsel > 0.0, b3, neg)
    sel = jnp.zeros((G, P, tt), F32)
    eids, gates = [], []
    for _ in range(TOP_K):
        mk = jnp.max(cand, axis=(0, 1), keepdims=True)
        ik = jnp.min(jnp.where(cand == mk, ei, E), axis=(0, 1), keepdims=True)
        hit = ei == ik
        gates.append(jnp.sum(jnp.where(hit, s3, 0.0), axis=(0, 1), keepdims=True))
        eids.append(ik)
        cand = jnp.where(hit, neg, cand)
        sel = jnp.where(hit, 1.0, sel)

    gsum = gates[0]
    for gk in gates[1:]:
        gsum = gsum + gk
    gscale = ROUTE_SCALE / gsum

    sel2 = sel.reshape(E, tt)
    prefix = jnp.dot(sel2.astype(BF16), tri_ref[...], preferred_element_type=F32)
    base3 = (prefix + carry_ref[...]).reshape(G, P, tt)
    ranks = [jnp.sum(jnp.where(ei == ik, base3, 0.0), axis=(0, 1), keepdims=True) for ik in eids]
    carry_ref[...] = carry_ref[...] + jnp.sum(sel2, axis=1, keepdims=True)
    cnt_ref[...] = carry_ref[...]

    eidx_ref[...] = jnp.concatenate([x.reshape(1, tt) for x in eids], axis=0)
    gate_ref[...] = jnp.concatenate([(g * gscale).reshape(1, tt) for g in gates], axis=0)
    rank_ref[...] = jnp.concatenate([x.reshape(1, tt) for x in ranks], axis=0).astype(I32)


def _router(h1, router_w, router_bias):
    T, D = h1.shape
    E = N_EXPERTS
    tt = min(TT_ROUTER, T)
    rwt = router_w.T.astype(BF16)
    tri = jnp.asarray(np.triu(np.ones((tt, tt), np.float32), k=1), BF16)
    const = lambda i: (0, 0)
    col = lambda i: (0, i)
    return pl.pallas_call(
        _router_kernel,
        grid=(T // tt,),
        in_specs=[
            pl.BlockSpec((tt, D), lambda i: (i, 0)),
            pl.BlockSpec((E, D), const),
            pl.BlockSpec((E, 1), const),
            pl.BlockSpec((tt, tt), const),
        ],
        out_specs=[
            pl.BlockSpec((TOP_K, tt), col),
            pl.BlockSpec((TOP_K, tt), col),
            pl.BlockSpec((TOP_K, tt), col),
            pl.BlockSpec((E, 1), const),
        ],
        out_shape=[
            jax.ShapeDtypeStruct((TOP_K, T), I32),
            jax.ShapeDtypeStruct((TOP_K, T), F32),
            jax.ShapeDtypeStruct((TOP_K, T), I32),
            jax.ShapeDtypeStruct((E, 1), F32),
        ],
        scratch_shapes=[pltpu.VMEM((E, 1), F32)],
        compiler_params=_cparams(1),
        name="router",
    )(h1, rwt, router_bias.reshape(E, 1).astype(F32), tri)


def _slots_kernel(pstart_ref, eidx_ref, rank_ref, slot_ref):
    eidx = eidx_ref[...]

    def add_expert(e, acc):
        return acc + jnp.where(eidx == e, pstart_ref[e], 0)

    slot_ref[...] = lax.fori_loop(0, N_EXPERTS, add_expert, rank_ref[...], unroll=8)


def _slots(pad_start, eidx, rank):
    K, T = eidx.shape
    tt = min(2048, T)
    col = lambda i, ps: (0, i)
    grid_spec = pltpu.PrefetchScalarGridSpec(
        num_scalar_prefetch=1,
        grid=(T // tt,),
        in_specs=[pl.BlockSpec((K, tt), col), pl.BlockSpec((K, tt), col)],
        out_specs=pl.BlockSpec((K, tt), col),
    )
    return pl.pallas_call(
        _slots_kernel,
        grid_spec=grid_spec,
        out_shape=jax.ShapeDtypeStruct((K, T), I32),
        compiler_params=_cparams(1),
        name="slots",
    )(pad_start, eidx, rank)


def _dispatch(slot, h1p, n_slots):
    T, Wp = h1p.shape
    n_workers = SC_CORES * SC_SUBCORES
    per = T // n_workers
    assert per % SC_IDX_CHUNK == 0, (T, n_workers)
    mesh = plsc.VectorSubcoreMesh(core_axis_name="core", subcore_axis_name="subcore")

    @pl.kernel(
        out_type=jax.ShapeDtypeStruct((n_slots, Wp), U32), mesh=mesh,
        scratch_types=[pltpu.VMEM((TOP_K, SC_IDX_CHUNK), I32),
                       pltpu.VMEM((SC_IDX_CHUNK // SC_ROW_WIN, SC_ROW_WIN, Wp), U32),
                       pltpu.SemaphoreType.DMA((SC_IDX_CHUNK // SC_ROW_WIN,)), pltpu.SemaphoreType.DMA])
    def dispatch(x_hbm, s_hbm, o_hbm, idx_v, x_v, sem_r, sem_w):
        base = (lax.axis_index("core") * SC_SUBCORES + lax.axis_index("subcore")) * per
        n_win = SC_IDX_CHUNK // SC_ROW_WIN

        @pl.loop(0, per // SC_IDX_CHUNK)
        def _(c):
            t0 = base + c * SC_IDX_CHUNK
            reads = [pltpu.async_copy(x_hbm.at[pl.ds(t0 + j * SC_ROW_WIN, SC_ROW_WIN)], x_v.at[j], sem_r.at[j])
                     for j in range(n_win)]
            pltpu.sync_copy(s_hbm.at[:, pl.ds(t0, SC_IDX_CHUNK)], idx_v)
            copies = []
            for j in range(n_win):
                reads[j].wait()
                copies += [
                    pltpu.async_copy(x_v.at[j], o_hbm.at[idx_v.at[k, pl.ds(j * SC_ROW_WIN, SC_ROW_WIN)]], sem_w)
                    for k in range(TOP_K)]
            for cp in copies:
                cp.wait()

    return dispatch(h1p, slot)


def _experts_kernel(layer, be_ref, nv_ref, par_ref, nxt_ref, xs_ref, wg_hbm, wu_hbm, wd_hbm, ys_ref,
                    wg_f, wu_f, wd_f, wg_s, wu_s, wd_s, sem):
    i = pl.program_id(0)

    def weight_copies(e, slot):
        return [pltpu.make_async_copy(src.at[layer, e], dst.at[slot], sem.at[slot])
                for src, dst in ((wg_hbm, wg_f), (wu_hbm, wu_f), (wd_hbm, wd_f))]

    @pl.when(i < nv_ref[0])
    def _():
        e = be_ref[i]
        slot = par_ref[i]

        @pl.when(i == 0)
        def _():
            for cp in weight_copies(e, slot):
                cp.start()

        @pl.when((i == 0) | (e != be_ref[jnp.maximum(i - 1, 0)]))
        def _():
            for cp in weight_copies(e, slot):
                cp.wait()
            wg_s[...] = wg_f[slot].astype(BF16)
            wu_s[...] = wu_f[slot].astype(BF16)
            wd_s[...] = wd_f[slot].astype(BF16)

            @pl.when(nxt_ref[i] >= 0)
            def _():
                for cp in weight_copies(nxt_ref[i], 1 - slot):
                    cp.start()

        sub = min(SUB_MOE, xs_ref.shape[0])

        def up(j):
            lo, hi = _unpack_bf16_pairs(xs_ref[j * sub:(j + 1) * sub, :])
            x = jnp.concatenate([lo.astype(BF16), hi.astype(BF16)], axis=1)
            return (jnp.dot(x, wg_s[...], preferred_element_type=F32),
                    jnp.dot(x, wu_s[...], preferred_element_type=F32))

        def down(j, g, u):
            hmid = (g * jax.nn.sigmoid(g) * u).astype(BF16)
            y = jnp.dot(hmid, wd_s[...], preferred_element_type=F32)
            ys_ref[j * sub:(j + 1) * sub, :] = _pack_bf16_pairs(y)

        n_sub = xs_ref.shape[0] // sub
        pending = up(0)
        for j in range(n_sub):
            nxt = up(j + 1) if j + 1 < n_sub else None
            down(j, *pending)
            pending = nxt


def _experts(layer, blk_exp, n_valid, blk_par, blk_next, xs, wg, wu, wd):
    n_slots, Wp = xs.shape
    _, E, D, F = wg.shape
    nb = n_slots // BLK_MOE

    def xmap(i, be, nv, par, nxt):
        return (jnp.minimum(i, nv[0] - 1), 0)

    hbm = pl.BlockSpec(memory_space=pl.ANY)
    grid_spec = pltpu.PrefetchScalarGridSpec(
        num_scalar_prefetch=4,
        grid=(nb,),
        in_specs=[pl.BlockSpec((BLK_MOE, Wp), xmap), hbm, hbm, hbm],
        out_specs=pl.BlockSpec((BLK_MOE, Wp), xmap),
        scratch_shapes=[
            pltpu.VMEM((2, D, F), F32), pltpu.VMEM((2, D, F), F32), pltpu.VMEM((2, F, D), F32),
            pltpu.VMEM((D, F), BF16), pltpu.VMEM((D, F), BF16), pltpu.VMEM((F, D), BF16),
            pltpu.SemaphoreType.DMA((2,)),
        ],
    )
    return pl.pallas_call(
        functools.partial(_experts_kernel, layer),
        grid_spec=grid_spec,
        out_shape=jax.ShapeDtypeStruct((n_slots, Wp), U32),
        compiler_params=_cparams(1),
        name="experts",
    )(blk_exp, n_valid, blk_par, blk_next, xs, wg, wu, wd)


def _gather_rows(slot, ys):
    K, T = slot.shape
    Wp = ys.shape[1]
    n_workers = SC_CORES * SC_SUBCORES
    per = T // n_workers
    assert per % SC_IDX_CHUNK == 0 and K % SC_GATHER_PLANES == 0, (T, K)
    win = SC_ROW_WIN // 4
    mesh = plsc.VectorSubcoreMesh(core_axis_name="core", subcore_axis_name="subcore")
    groups = [(j, k0) for j in range(SC_IDX_CHUNK // win) for k0 in range(0, K, SC_GATHER_PLANES)]

    @pl.kernel(
        out_type=jax.ShapeDtypeStruct((K, T, Wp), U32), mesh=mesh,
        scratch_types=[pltpu.VMEM((K, SC_IDX_CHUNK), I32),
                       pltpu.VMEM((2, SC_GATHER_PLANES, win, Wp), U32),
                       pltpu.SemaphoreType.DMA((2,)), pltpu.SemaphoreType.DMA((2,))])
    def gather(y_hbm, s_hbm, g_hbm, idx_v, buf, sem_g, sem_w):
        base = (lax.axis_index("core") * SC_SUBCORES + lax.axis_index("subcore")) * per

        @pl.loop(0, per // SC_IDX_CHUNK)
        def _(c):
            t0 = base + c * SC_IDX_CHUNK
            pltpu.sync_copy(s_hbm.at[:, pl.ds(t0, SC_IDX_CHUNK)], idx_v)

            def start_reads(g):
                j, k0 = groups[g]
                return [pltpu.async_copy(y_hbm.at[idx_v.at[k0 + i, pl.ds(j * win, win)]],
                                         buf.at[g % 2, i], sem_g.at[g % 2])
                        for i in range(SC_GATHER_PLANES)]

            def start_writes(g):
                j, k0 = groups[g]
                return [pltpu.async_copy(buf.at[g % 2, i], g_hbm.at[k0 + i, pl.ds(t0 + j * win, win)],
                                         sem_w.at[g % 2])
                        for i in range(SC_GATHER_PLANES)]

            reads = start_reads(0)
            writes_prev = []
            for g in range(len(groups)):
                for cp in reads:
                    cp.wait()
                writes = start_writes(g)
                for cp in writes_prev:
                    cp.wait()
                if g + 1 < len(groups):
                    reads = start_reads(g + 1)
                writes_prev = writes
            for cp in writes_prev:
                cp.wait()

    return gather(ys, slot)


def _combine_kernel(yg_ref, gate_ref, h1_ref, swg_ref, swu_ref, swd_ref, lg_ref, lb_ref, *rest):
    out_ref = rest[-1]
    h1 = h1_ref[...]
    xb = h1.astype(BF16)
    g = jnp.dot(xb, swg_ref[...], preferred_element_type=F32)
    u = jnp.dot(xb, swu_ref[...], preferred_element_type=F32)
    hmid = (g * jax.nn.sigmoid(g) * u).astype(BF16)
    shared = jnp.dot(hmid, swd_ref[...], preferred_element_type=F32)

    gates = gate_ref[...]
    tc, wp = yg_ref.shape[1], yg_ref.shape[2]
    f_lo = jnp.zeros((tc, wp), F32)
    f_hi = jnp.zeros((tc, wp), F32)
    for k in range(TOP_K):
        lo, hi = _unpack_bf16_pairs(yg_ref[k])
        gk = gates[:, k:k + 1]
        f_lo = f_lo + gk * lo
        f_hi = f_hi + gk * hi
    routed = jnp.concatenate([f_lo, f_hi], axis=1)
    out_ref[...] = _layer_norm(ALPHA * h1 + (routed + shared), lg_ref[...], lb_ref[...])


def _combine(yg, gate_t, h1, swg, swu, swd, ln_g, ln_b, part, prev_out):
    T, D = h1.shape
    K, Tp, Wp = yg.shape
    tc = min(TD_MOE, Tp)
    off = part * (Tp // tc)
    row = lambda i: (i + off, 0)
    const = lambda i: (0, 0)
    res = lambda a: pl.BlockSpec(a.shape, const)
    in_specs = [
        pl.BlockSpec((K, tc, Wp), lambda i: (0, i, 0)),
        pl.BlockSpec((tc, TOP_K), row),
        pl.BlockSpec((tc, D), row),
        res(swg), res(swu), res(swd),
        pl.BlockSpec((1, D), const),
        pl.BlockSpec((1, D), const),
    ]
    args = [yg, gate_t, h1, swg, swu, swd, ln_g.reshape(1, D), ln_b.reshape(1, D)]
    aliases = {}
    if prev_out is not None:
        in_specs.append(pl.BlockSpec(memory_space=pl.ANY))
        args.append(prev_out)
        aliases = {len(args) - 1: 0}
    return pl.pallas_call(
        _combine_kernel,
        grid=(Tp // tc,),
        in_specs=in_specs,
        out_specs=pl.BlockSpec((tc, D), row),
        out_shape=jax.ShapeDtypeStruct((T, D), F32),
        input_output_aliases=aliases,
        compiler_params=_cparams(1),
        name="combine",
    )(*args)


def _moe_layer(layer, h1, h1p, router_w, router_bias, wg, wu, wd, swg, swu, swd, ln_g, ln_b):
    T, D = h1.shape
    E = N_EXPERTS
    M = T * TOP_K
    eidx, gate, rank, cnt = _router(h1, router_w, router_bias)

    counts = cnt[:, 0].astype(I32)
    padded = (counts + BLK_MOE - 1) // BLK_MOE * BLK_MOE
    pad_end = jnp.cumsum(padded)
    pad_start = pad_end - padded
    nb = -(-(M + E * (BLK_MOE - 1)) // BLK_MOE)
    n_slots = nb * BLK_MOE
    slot = _slots(pad_start.astype(I32), eidx, rank)
    n_valid = (pad_end[-1] // BLK_MOE).astype(I32).reshape(1)
    blk_start = jnp.arange(nb, dtype=I32) * BLK_MOE
    blk_exp = jnp.minimum(jnp.sum((pad_end[None, :] <= blk_start[:, None]).astype(I32), axis=1), E - 1)

    eids = jnp.arange(E, dtype=I32)
    nonempty = counts > 0
    order = jnp.cumsum(nonempty.astype(I32)) - 1
    nxt_e = lax.cummin(jnp.where(nonempty, eids, E), reverse=True)
    nxt_e = jnp.concatenate([nxt_e[1:], jnp.full((1,), E, I32)])
    blk_par = (order % 2)[blk_exp].astype(I32)
    blk_next = jnp.where(nxt_e < E, nxt_e, -1)[blk_exp].astype(I32)

    xs = _dispatch(slot, h1p, n_slots)
    ys = _experts(layer, blk_exp, n_valid, blk_par, blk_next, xs, wg, wu, wd)
    n_parts = MOE_COMBINE_PARTS if T % (MOE_COMBINE_PARTS * SC_CORES * SC_SUBCORES * SC_IDX_CHUNK) == 0 else 1
    tp = T // n_parts
    gate_t = gate.T
    shared_w = (swg.astype(BF16), swu.astype(BF16), swd.astype(BF16))
    parts = [_gather_rows(slot[:, p * tp:(p + 1) * tp], ys) for p in range(n_parts)]
    out = None
    for p in range(n_parts):
        out = _combine(parts[p], gate_t, h1, *shared_w, ln_g, ln_b, p, out)
    return out


def kernel(x, emb_ln_g, emb_ln_b, w_in, b_gate, lam_q1, lam_k1, lam_q2, lam_k2, subln_g, w_proj_a, na_rpb, w_proj_b, w_out, ln1_g, ln1_b, router_w, router_bias, exp_w_gate, exp_w_up, exp_w_down, sh_w_gate, sh_w_up, sh_w_down, ln2_g, ln2_b):
    B, S, D = x.shape
    T = B * S
    rows = S // GRID_W
    h = x.reshape(T, D)
    for l in range(DEPTH):
        lam_init = 0.8 - 0.6 * math.exp(-0.3 * l)
        outs = _inproj(h, emb_ln_g, emb_ln_b, w_in[l].astype(BF16), apply_ln=(l == 0))
        if l == 0:
            h, outs = outs[0], outs[1:]
        qa, ka, va, qn, kn, vn, ga, gb = outs
        lamv = jnp.stack([lam_q1[l], lam_k1[l], lam_q2[l], lam_k2[l]]).astype(F32)
        oa = _diffattn(qa.reshape(B, S, -1), ka.reshape(B, S, -1), va.reshape(B, S, -1),
                       lamv, subln_g[l], lam_init)
        ob = _natten(qn.reshape(B, S, -1), kn.reshape(B, S, -1), vn.reshape(B, S, -1),
                     _na_bias_table(na_rpb[l], rows))
        h1, h1p = _mixout(oa.reshape(T, -1), ob.reshape(T, -1), ga, gb, h, b_gate[l],
                          w_proj_a[l].astype(BF16), w_proj_b[l].astype(BF16), w_out[l].astype(BF16),
                          ln1_g[l], ln1_b[l])
        h = _moe_layer(l, h1, h1p, router_w[l], router_bias[l], exp_w_gate, exp_w_up, exp_w_down,
                       sh_w_gate[l], sh_w_up[l], sh_w_down[l], ln2_g[l], ln2_b[l])
    return h.reshape(B, S, D)
```

```python
import functools
import math

import numpy as np
import jax
import jax.numpy as jnp
from jax import lax
from jax.experimental import pallas as pl
from jax.experimental.pallas import tpu as pltpu
from jax.experimental.pallas import tpu_sc as plsc

F32 = jnp.float32
BF16 = jnp.bfloat16
U32 = jnp.uint32
I32 = jnp.int32

DA_HEADS = 8
DA_HEAD_DIM = 64
NA_HEADS = 16
NA_HEAD_DIM = 32
GRID_W = 64
WIN_R = 8
WIN_C = 16
N_EXPERTS = 256
TOP_K = 8
N_GROUPS = 8
TOPK_GROUPS = 4
ROUTE_SCALE = 2.5
DEPTH = 2
ALPHA = (2 * DEPTH) ** 0.25
LN_EPS = 1e-5
LOG2E = 1.4426950408889634

LANES = 128
NA_HEADS_PER_BLOCK = LANES // NA_HEAD_DIM
VMEM_LIMIT = 56 * 1024 * 1024
SC_CORES = 2
SC_SUBCORES = 16
SC_IDX_CHUNK = 128
SC_ROW_WIN = 64
SC_GATHER_PLANES = 4

TM_PROJ = 512
TM_MIX = 1024
TQ_DA = 2048
TQ_SUB_DA = 128
NA_ROWS_PER_ITER = 4
TT_ROUTER = 512
TD_MOE = 512
BLK_MOE = 512
SUB_MOE = 256
MOE_COMBINE_PARTS = 4
NEG_BIG = -1e30


def _cparams(n_axes, flags=None):
    return pltpu.CompilerParams(
        dimension_semantics=("arbitrary",) * n_axes, vmem_limit_bytes=VMEM_LIMIT, flags=flags)


def _layer_norm(x, g, b):
    mu = jnp.mean(x, axis=-1, keepdims=True)
    xc = x - mu
    var = jnp.mean(xc * xc, axis=-1, keepdims=True)
    return xc * lax.rsqrt(var + LN_EPS) * g + b


def _pack_bf16_pairs(y):
    w = y.shape[1] // 2
    lo = lax.bitcast_convert_type(y[:, :w].astype(BF16).astype(F32), U32)
    hi = lax.bitcast_convert_type(y[:, w:].astype(BF16).astype(F32), U32)
    return (hi & jnp.uint32(0xFFFF0000)) | (lo >> 16)


def _unpack_bf16_pairs(u):
    lo = lax.bitcast_convert_type(u << 16, F32)
    hi = lax.bitcast_convert_type(u & jnp.uint32(0xFFFF0000), F32)
    return lo, hi


def _inproj_kernel(apply_ln, seg_widths, seg_scales, x_ref, g_ref, b_ref, w_ref, *out_refs):
    x = x_ref[...]
    if apply_ln:
        x = _layer_norm(x, g_ref[...], b_ref[...])
        out_refs[0][...] = x
        out_refs = out_refs[1:]
    xb = x.astype(BF16)
    off = 0
    for ref, width, scale in zip(out_refs, seg_widths, seg_scales):
        y = jnp.dot(xb, w_ref[:, off:off + width], preferred_element_type=F32)
        if scale != 1.0:
            y = y * scale
        ref[...] = y.astype(BF16)
        off += width


def _inproj(x, ln_g, ln_b, w_in_bf16, apply_ln):
    T, D = x.shape
    da_w = DA_HEADS * 2 * DA_HEAD_DIM
    na_w = NA_HEADS * NA_HEAD_DIM
    seg_widths = (da_w, da_w, da_w, na_w, na_w, na_w, D, D)
    seg_scales = (DA_HEAD_DIM ** -0.5 * LOG2E, 1.0, 1.0, NA_HEAD_DIM ** -0.5 * LOG2E, 1.0, 1.0, 1.0, 1.0)
    tm = min(TM_PROJ, T)
    n_cols = w_in_bf16.shape[1]
    row = lambda i: (i, 0)
    const = lambda i: (0, 0)
    out_shape = [jax.ShapeDtypeStruct((T, w), BF16) for w in seg_widths]
    out_specs = [pl.BlockSpec((tm, w), row) for w in seg_widths]
    if apply_ln:
        out_shape = [jax.ShapeDtypeStruct((T, D), F32)] + out_shape
        out_specs = [pl.BlockSpec((tm, D), row)] + out_specs
    return pl.pallas_call(
        functools.partial(_inproj_kernel, apply_ln, seg_widths, seg_scales),
        grid=(T // tm,),
        in_specs=[
            pl.BlockSpec((tm, D), row),
            pl.BlockSpec((1, D), const),
            pl.BlockSpec((1, D), const),
            pl.BlockSpec((D, n_cols), const, pipeline_mode=pl.Buffered(1)),
        ],
        out_specs=out_specs,
        out_shape=out_shape,
        compiler_params=_cparams(1),
        name="inproj",
    )(x, ln_g.reshape(1, D), ln_b.reshape(1, D), w_in_bf16)


def _diffattn_kernel(lam_init, slopes_ref, q_ref, k_ref, v_ref, lamv_ref, g_ref, o_ref, bias_ref):
    h = pl.program_id(0)
    qi = pl.program_id(1)
    b = pl.program_id(2)
    tq = q_ref.shape[1]
    S = k_ref.shape[1]
    d = DA_HEAD_DIM

    @pl.when(b == 0)
    def _():
        qpos = qi * tq + lax.broadcasted_iota(I32, (tq, S), 0)
        kpos = lax.broadcasted_iota(I32, (tq, S), 1)
        bias_ref[...] = jnp.abs(qpos - kpos).astype(F32) * (-slopes_ref[h])

    lv = lamv_ref[...]
    lam = (jnp.exp(jnp.sum(lv[0:1] * lv[1:2], axis=-1, keepdims=True))
           - jnp.exp(jnp.sum(lv[2:3] * lv[3:4], axis=-1, keepdims=True)) + lam_init)

    k = k_ref[0]
    v1 = jnp.concatenate([v_ref[0], jnp.ones((S, 2 * d), BF16)], axis=1)
    gain = g_ref[...] * (1.0 - lam_init)
    nt = (((1,), (1,)), ((), ()))
    ts = min(TQ_SUB_DA, tq)
    lane = lax.broadcasted_iota(I32, (ts, 2 * d), 1)

    def scores(j):
        q = q_ref[0, j * ts:(j + 1) * ts, :]
        nb = bias_ref[j * ts:(j + 1) * ts, :]
        zero = jnp.zeros_like(q)
        s1 = lax.dot_general(jnp.where(lane < d, q, zero), k, nt, preferred_element_type=F32) + nb
        s2 = lax.dot_general(jnp.where(lane >= d, q, zero), k, nt, preferred_element_type=F32) + nb
        return s1, s2

    def weights(s1, s2):
        return (jnp.exp2((s1 - jnp.max(s1, axis=-1, keepdims=True)).astype(BF16)),
                jnp.exp2((s2 - jnp.max(s2, axis=-1, keepdims=True)).astype(BF16)))

    def finish(j, e1, e2):
        p = jnp.dot(jnp.concatenate([e1, e2], axis=0), v1, preferred_element_type=F32)
        p1, p2 = p[:ts], p[ts:]
        o = p1[:, :2 * d] / p1[:, 2 * d:] - lam * (p2[:, :2 * d] / p2[:, 2 * d:])
        ms = jnp.mean(o * o, axis=-1, keepdims=True)
        o = o * lax.rsqrt(ms + LN_EPS) * gain
        o_ref[0, j * ts:(j + 1) * ts, :] = o.astype(BF16)

    n_sub = tq // ts
    s_next = scores(0)
    e_cur = weights(*s_next)
    s_next = scores(1) if n_sub > 1 else None
    for j in range(n_sub):
        s_after = scores(j + 2) if j + 2 < n_sub else None
        finish(j, *e_cur)
        if s_next is not None:
            e_cur = weights(*s_next)
        s_next = s_after


def _diffattn(qa, ka, va, lamv, subln_g, lam_init):
    B, S, W = qa.shape
    hw = 2 * DA_HEAD_DIM
    tq = min(TQ_DA, S)
    slopes = jnp.asarray(2.0 ** (-8.0 * np.arange(1, DA_HEADS + 1) / DA_HEADS) * LOG2E, F32)
    grid_spec = pltpu.PrefetchScalarGridSpec(
        num_scalar_prefetch=1,
        grid=(DA_HEADS, S // tq, B),
        in_specs=[
            pl.BlockSpec((1, tq, hw), lambda h, qi, b, sl: (b, qi, h)),
            pl.BlockSpec((1, S, hw), lambda h, qi, b, sl: (b, 0, h)),
            pl.BlockSpec((1, S, hw), lambda h, qi, b, sl: (b, 0, h)),
            pl.BlockSpec((4, DA_HEAD_DIM), lambda h, qi, b, sl: (0, 0)),
            pl.BlockSpec((1, hw), lambda h, qi, b, sl: (0, 0)),
        ],
        out_specs=pl.BlockSpec((1, tq, hw), lambda h, qi, b, sl: (b, qi, h)),
        scratch_shapes=[pltpu.VMEM((tq, S), F32)],
    )
    return pl.pallas_call(
        functools.partial(_diffattn_kernel, lam_init),
        grid_spec=grid_spec,
        out_shape=jax.ShapeDtypeStruct((B, S, W), BF16),
        compiler_params=_cparams(3),
        name="diffattn",
    )(slopes, qa, ka, va, lamv, subln_g.reshape(1, hw))


def _na_bias_table(rpb, rows):
    kh = min(WIN_R, rows)
    qcol = np.arange(GRID_W)
    kcol = np.arange(GRID_W)
    cs = np.clip(qcol - WIN_C // 2, 0, GRID_W - WIN_C)
    col_mask = (kcol[None, :] >= cs[:, None]) & (kcol[None, :] < cs[:, None] + WIN_C)
    col_off = np.clip(kcol[None, :] - qcol[:, None] + WIN_C - 1, 0, 2 * WIN_C - 2)
    v = np.arange(kh)
    j = np.arange(kh)
    row_idx = np.clip(j[None, :] - v[:, None] + WIN_R - 1, 0, 2 * WIN_R - 2)
    sel_r = np.zeros((2 * WIN_R - 1, kh * kh), np.float32)
    sel_r[row_idx.reshape(-1), np.arange(kh * kh)] = 1.0
    sel_c = np.zeros((2 * WIN_C - 1, GRID_W * GRID_W), np.float32)
    sel_c[col_off.reshape(-1), np.arange(GRID_W * GRID_W)] = 1.0
    hi = lax.Precision.HIGHEST
    t = jnp.einsum("hrc,rp->hpc", rpb * LOG2E, jnp.asarray(sel_r), precision=hi)
    t = jnp.einsum("hpc,cq->hpq", t, jnp.asarray(sel_c), precision=hi)
    t = t.reshape(NA_HEADS, kh, kh, GRID_W, GRID_W)
    t = jnp.where(jnp.asarray(col_mask)[None, None, None], t, NEG_BIG)
    t = t.transpose(1, 0, 3, 2, 4)
    nblk = NA_HEADS // NA_HEADS_PER_BLOCK
    return t.reshape(kh, nblk, NA_HEADS_PER_BLOCK * GRID_W, kh * GRID_W)


def _natten_kernel(rows, kh, q_ref, k_ref, v_ref, bias_ref, o_ref):
    hid = lax.broadcasted_iota(I32, (GRID_W, LANES), 1) // NA_HEAD_DIM

    def scores(r):
        rs = jnp.clip(r - WIN_R // 2, 0, rows - kh)
        q0 = pl.multiple_of(r * GRID_W, GRID_W)
        k0 = pl.multiple_of(rs * GRID_W, GRID_W)
        qr = q_ref[0, pl.ds(q0, GRID_W), :]
        zero = jnp.zeros_like(qr)
        qq = jnp.concatenate(
            [jnp.where(hid == hh, qr, zero) for hh in range(NA_HEADS_PER_BLOCK)], axis=0)
        kb = k_ref[0, pl.ds(k0, kh * GRID_W), :]
        s = lax.dot_general(qq, kb, (((1,), (1,)), ((), ())), preferred_element_type=F32)
        return s + bias_ref[r - rs, 0]

    def weights(s):
        e = jnp.exp2(s - jnp.max(s, axis=-1, keepdims=True))
        return e.astype(BF16), 1.0 / jnp.sum(e, axis=-1, keepdims=True)

    def finish(r, e, rl):
        rs = jnp.clip(r - WIN_R // 2, 0, rows - kh)
        q0 = pl.multiple_of(r * GRID_W, GRID_W)
        k0 = pl.multiple_of(rs * GRID_W, GRID_W)
        vb = v_ref[0, pl.ds(k0, kh * GRID_W), :]
        oo = jnp.dot(e, vb, preferred_element_type=F32) * rl
        o = jnp.zeros((GRID_W, LANES), F32)
        for hh in range(NA_HEADS_PER_BLOCK):
            o = o + jnp.where(hid == hh, oo[hh * GRID_W:(hh + 1) * GRID_W], 0.0)
        o_ref[0, pl.ds(q0, GRID_W), :] = o.astype(BF16)

    def row_group(g, carry):
        rr = [g * NA_ROWS_PER_ITER + i for i in range(NA_ROWS_PER_ITER)]
        ss = [scores(r) for r in rr]
        ws = [weights(s) for s in ss]
        for r, (e, rl) in zip(rr, ws):
            finish(r, e, rl)
        return carry

    lax.fori_loop(0, rows // NA_ROWS_PER_ITER, row_group, 0)


def _natten(qn, kn, vn, bias_tab):
    B, S, W = qn.shape
    rows = S // GRID_W
    kh = min(WIN_R, rows)
    nblk = W // LANES
    blk = lambda g, b: (b, 0, g)
    return pl.pallas_call(
        functools.partial(_natten_kernel, rows, kh),
        grid=(nblk, B),
        in_specs=[
            pl.BlockSpec((1, S, LANES), blk),
            pl.BlockSpec((1, S, LANES), blk),
            pl.BlockSpec((1, S, LANES), blk),
            pl.BlockSpec((kh, 1, NA_HEADS_PER_BLOCK * GRID_W, kh * GRID_W), lambda g, b: (0, g, 0, 0)),
        ],
        out_specs=pl.BlockSpec((1, S, LANES), blk),
        out_shape=jax.ShapeDtypeStruct((B, S, W), BF16),
        compiler_params=_cparams(2),
        name="natten",
    )(qn, kn, vn, bias_tab)


def _mixout_kernel(oa_ref, ob_ref, ga_ref, gb_ref, h_ref, bg_ref, wa_ref, wb_ref, wo_ref,
                   lg_ref, lb_ref, h1_ref, h1p_ref):
    ya = jnp.dot(oa_ref[...], wa_ref[...], preferred_element_type=F32)
    yb = jnp.dot(ob_ref[...], wb_ref[...], preferred_element_type=F32)
    g_a = jax.nn.sigmoid(ga_ref[...].astype(F32) + bg_ref[0:1, :])
    g_b = jax.nn.sigmoid(gb_ref[...].astype(F32) + bg_ref[1:2, :])
    z = (g_a * ya + g_b * yb).astype(BF16)
    m = jnp.dot(z, wo_ref[...], preferred_element_type=F32)
    y = _layer_norm(ALPHA * h_ref[...] + m, lg_ref[...], lb_ref[...])
    h1_ref[...] = y
    h1p_ref[...] = _pack_bf16_pairs(y)


def _mixout(oa, ob, ga, gb, h, b_gate, wa, wb, wo, ln_g, ln_b):
    T, D = h.shape
    tm = min(TM_MIX, T)
    row = lambda i: (i, 0)
    const = lambda i: (0, 0)
    res = lambda a: pl.BlockSpec(a.shape, const, pipeline_mode=pl.Buffered(1))
    return pl.pallas_call(
        _mixout_kernel,
        grid=(T // tm,),
        in_specs=[
            pl.BlockSpec((tm, oa.shape[1]), row),
            pl.BlockSpec((tm, ob.shape[1]), row),
            pl.BlockSpec((tm, D), row),
            pl.BlockSpec((tm, D), row),
            pl.BlockSpec((tm, D), row),
            pl.BlockSpec((2, D), const),
            res(wa), res(wb), res(wo),
            pl.BlockSpec((1, D), const),
            pl.BlockSpec((1, D), const),
        ],
        out_specs=[pl.BlockSpec((tm, D), row), pl.BlockSpec((tm, D // 2), row)],
        out_shape=[jax.ShapeDtypeStruct((T, D), F32), jax.ShapeDtypeStruct((T, D // 2), U32)],
        compiler_params=_cparams(1),
        name="mixout",
    )(oa, ob, ga, gb, h, b_gate, wa, wb, wo, ln_g.reshape(1, D), ln_b.reshape(1, D))


def _router_kernel(h_ref, rwt_ref, rb_ref, tri_ref, eidx_ref, gate_ref, rank_ref, cnt_ref, carry_ref):
    i = pl.program_id(0)
    tt = h_ref.shape[0]
    E, G = N_EXPERTS, N_GROUPS
    P = E // G
    neg = -jnp.inf

    @pl.when(i == 0)
    def _():
        carry_ref[...] = jnp.zeros_like(carry_ref)

    hb = h_ref[...].astype(BF16)
    logits = lax.dot_general(rwt_ref[...], hb, (((1,), (1,)), ((), ())), preferred_element_type=F32)
    scores = jax.nn.sigmoid(logits)
    biased = scores + rb_ref[...]
    b3 = biased.reshape(G, P, tt)
    s3 = scores.reshape(G, P, tt)
    pi = lax.broadcasted_iota(I32, (G, P, tt), 1)
    ei = lax.broadcasted_iota(I32, (G, P, tt), 0) * P + pi

    m1 = jnp.max(b3, axis=1, keepdims=True)
    i1 = jnp.min(jnp.where(b3 == m1, pi, P), axis=1, keepdims=True)
    m2 = jnp.max(jnp.where(pi == i1, neg, b3), axis=1, keepdims=True)
    grp = m1 + m2
    gi = lax.broadcasted_iota(I32, (G, 1, tt), 0)
    gsel = jnp.zeros((G, 1, tt), F32)
    for _ in range(TOPK_GROUPS):
        gm = jnp.max(grp, axis=0, keepdims=True)
        gidx = jnp.min(jnp.where(grp == gm, gi, G), axis=0, keepdims=True)
        hit = gi == gidx
        gsel = jnp.where(hit, 1.0, gsel)
        grp = jnp.where(hit, neg, grp)

    cand = jnp.where(gsel > 0.0, b3, neg)
    sel = jnp.zeros((G, P, tt), F32)
    eids, gates = [], []
    for _ in range(TOP_K):
        mk = jnp.max(cand, axis=(0, 1), keepdims=True)
        ik = jnp.min(jnp.where(cand == mk, ei, E), axis=(0, 1), keepdims=True)
        hit = ei == ik
        gates.append(jnp.sum(jnp.where(hit, s3, 0.0), axis=(0, 1), keepdims=True))
        eids.append(ik)
        cand = jnp.where(hit, neg, cand)
        sel = jnp.where(hit, 1.0, sel)

    gsum = gates[0]
    for gk in gates[1:]:
        gsum = gsum + gk
    gscale = ROUTE_SCALE / gsum

    sel2 = sel.reshape(E, tt)
    prefix = jnp.dot(sel2.astype(BF16), tri_ref[...], preferred_element_type=F32)
    base3 = (prefix + carry_ref[...]).reshape(G, P, tt)
    ranks = [jnp.sum(jnp.where(ei == ik, base3, 0.0), axis=(0, 1), keepdims=True) for ik in eids]
    carry_ref[...] = carry_ref[...] + jnp.sum(sel2, axis=1, keepdims=True)
    cnt_ref[...] = carry_ref[...]

    eidx_ref[...] = jnp.concatenate([x.reshape(1, tt) for x in eids], axis=0)
    gate_ref[...] = jnp.concatenate([(g * gscale).reshape(1, tt) for g in gates], axis=0)
    rank_ref[...] = jnp.concatenate([x.reshape(1, tt) for x in ranks], axis=0).astype(I32)


def _router(h1, router_w, router_bias):
    T, D = h1.shape
    E = N_EXPERTS
    tt = min(TT_ROUTER, T)
    rwt = router_w.T.astype(BF16)
    tri = jnp.asarray(np.triu(np.ones((tt, tt), np.float32), k=1), BF16)
    const = lambda i: (0, 0)
    col = lambda i: (0, i)
    return pl.pallas_call(
        _router_kernel,
        grid=(T // tt,),
        in_specs=[
            pl.BlockSpec((tt, D), lambda i: (i, 0)),
            pl.BlockSpec((E, D), const),
            pl.BlockSpec((E, 1), const),
            pl.BlockSpec((tt, tt), const),
        ],
        out_specs=[
            pl.BlockSpec((TOP_K, tt), col),
            pl.BlockSpec((TOP_K, tt), col),
            pl.BlockSpec((TOP_K, tt), col),
            pl.BlockSpec((E, 1), const),
        ],
        out_shape=[
            jax.ShapeDtypeStruct((TOP_K, T), I32),
            jax.ShapeDtypeStruct((TOP_K, T), F32),
            jax.ShapeDtypeStruct((TOP_K, T), I32),
            jax.ShapeDtypeStruct((E, 1), F32),
        ],
        scratch_shapes=[pltpu.VMEM((E, 1), F32)],
        compiler_params=_cparams(1),
        name="router",
    )(h1, rwt, router_bias.reshape(E, 1).astype(F32), tri)


def _slots_kernel(pstart_ref, eidx_ref, rank_ref, slot_ref):
    eidx = eidx_ref[...]

    def add_expert(e, acc):
        return acc + jnp.where(eidx == e, pstart_ref[e], 0)

    slot_ref[...] = lax.fori_loop(0, N_EXPERTS, add_expert, rank_ref[...], unroll=8)


def _slots(pad_start, eidx, rank):
    K, T = eidx.shape
    tt = min(2048, T)
    col = lambda i, ps: (0, i)
    grid_spec = pltpu.PrefetchScalarGridSpec(
        num_scalar_prefetch=1,
        grid=(T // tt,),
        in_specs=[pl.BlockSpec((K, tt), col), pl.BlockSpec((K, tt), col)],
        out_specs=pl.BlockSpec((K, tt), col),
    )
    return pl.pallas_call(
        _slots_kernel,
        grid_spec=grid_spec,
        out_shape=jax.ShapeDtypeStruct((K, T), I32),
        compiler_params=_cparams(1),
        name="slots",
    )(pad_start, eidx, rank)


def _dispatch(slot, h1p, n_slots):
    T, Wp = h1p.shape
    n_workers = SC_CORES * SC_SUBCORES
    per = T // n_workers
    assert per % SC_IDX_CHUNK == 0, (T, n_workers)
    mesh = plsc.VectorSubcoreMesh(core_axis_name="core", subcore_axis_name="subcore")

    @pl.kernel(
        out_type=jax.ShapeDtypeStruct((n_slots, Wp), U32), mesh=mesh,
        scratch_types=[pltpu.VMEM((TOP_K, SC_IDX_CHUNK), I32),
                       pltpu.VMEM((SC_IDX_CHUNK // SC_ROW_WIN, SC_ROW_WIN, Wp), U32),
                       pltpu.SemaphoreType.DMA((SC_IDX_CHUNK // SC_ROW_WIN,)), pltpu.SemaphoreType.DMA])
    def dispatch(x_hbm, s_hbm, o_hbm, idx_v, x_v, sem_r, sem_w):
        base = (lax.axis_index("core") * SC_SUBCORES + lax.axis_index("subcore")) * per
        n_win = SC_IDX_CHUNK // SC_ROW_WIN

        @pl.loop(0, per // SC_IDX_CHUNK)
        def _(c):
            t0 = base + c * SC_IDX_CHUNK
            reads = [pltpu.async_copy(x_hbm.at[pl.ds(t0 + j * SC_ROW_WIN, SC_ROW_WIN)], x_v.at[j], sem_r.at[j])
                     for j in range(n_win)]
            pltpu.sync_copy(s_hbm.at[:, pl.ds(t0, SC_IDX_CHUNK)], idx_v)
            copies = []
            for j in range(n_win):
                reads[j].wait()
                copies += [
                    pltpu.async_copy(x_v.at[j], o_hbm.at[idx_v.at[k, pl.ds(j * SC_ROW_WIN, SC_ROW_WIN)]], sem_w)
                    for k in range(TOP_K)]
            for cp in copies:
                cp.wait()

    return dispatch(h1p, slot)


def _experts_kernel(layer, be_ref, nv_ref, par_ref, nxt_ref, xs_hbm, wg_hbm, wu_hbm, wd_hbm, ys_hbm,
                    xbuf, ybuf, wg_f, wu_f, wd_f, wg_s, wu_s, wd_s, sem_w, sem_x, sem_y):
    nv = nv_ref[0]
    blk = xbuf.shape[1]
    sub = min(SUB_MOE, blk)
    n_sub = blk // sub

    def rows(i):
        return pl.ds(pl.multiple_of(i * blk, blk), blk)

    def x_copy(i, slot):
        return pltpu.make_async_copy(xs_hbm.at[rows(i)], xbuf.at[slot], sem_x.at[slot])

    def y_copy(i, slot):
        return pltpu.make_async_copy(ybuf.at[slot], ys_hbm.at[rows(i)], sem_y.at[slot])

    def weight_copies(e, slot):
        return [pltpu.make_async_copy(src.at[layer, e], dst.at[slot], sem_w.at[slot])
                for src, dst in ((wg_hbm, wg_f), (wu_hbm, wu_f), (wd_hbm, wd_f))]

    x_copy(0, 0).start()
    for cp in weight_copies(be_ref[0], par_ref[0]):
        cp.start()

    def block(i, carry):
        slot = lax.rem(i, 2)
        e = be_ref[i]
        wslot = par_ref[i]

        @pl.when(i + 1 < nv)
        def _():
            x_copy(i + 1, 1 - slot).start()

        @pl.when((i == 0) | (e != be_ref[jnp.maximum(i - 1, 0)]))
        def _():
            for cp in weight_copies(e, wslot):
                cp.wait()
            wg_s[...] = wg_f[wslot].astype(BF16)
            wu_s[...] = wu_f[wslot].astype(BF16)
            wd_s[...] = wd_f[wslot].astype(BF16)

            @pl.when(nxt_ref[i] >= 0)
            def _():
                for cp in weight_copies(nxt_ref[i], 1 - wslot):
                    cp.start()

        x_copy(i, slot).wait()

        @pl.when(i >= 2)
        def _():
            y_copy(i - 2, slot).wait()

        def up(j):
            lo, hi = _unpack_bf16_pairs(xbuf[slot, j * sub:(j + 1) * sub, :])
            x = jnp.concatenate([lo.astype(BF16), hi.astype(BF16)], axis=1)
            return (jnp.dot(x, wg_s[...], preferred_element_type=F32),
                    jnp.dot(x, wu_s[...], preferred_element_type=F32))

        def down(j, g, u):
            hmid = (g * jax.nn.sigmoid(g) * u).astype(BF16)
            y = jnp.dot(hmid, wd_s[...], preferred_element_type=F32)
            ybuf[slot, j * sub:(j + 1) * sub, :] = _pack_bf16_pairs(y)

        pending = up(0)
        for j in range(n_sub):
            nxt = up(j + 1) if j + 1 < n_sub else None
            down(j, *pending)
            pending = nxt

        y_copy(i, slot).start()
        return carry

    lax.fori_loop(0, nv, block, 0)

    @pl.when(nv >= 2)
    def _():
        y_copy(nv - 2, lax.rem(nv, 2)).wait()

    y_copy(nv - 1, lax.rem(nv - 1, 2)).wait()


def _experts(layer, blk_exp, n_valid, blk_par, blk_next, xs, wg, wu, wd):
    n_slots, Wp = xs.shape
    _, E, D, F = wg.shape

    hbm = pl.BlockSpec(memory_space=pl.ANY)
    grid_spec = pltpu.PrefetchScalarGridSpec(
        num_scalar_prefetch=4,
        grid=(1,),
        in_specs=[hbm, hbm, hbm, hbm],
        out_specs=hbm,
        scratch_shapes=[
            pltpu.VMEM((2, BLK_MOE, Wp), U32), pltpu.VMEM((2, BLK_MOE, Wp), U32),
            pltpu.VMEM((2, D, F), F32), pltpu.VMEM((2, D, F), F32), pltpu.VMEM((2, F, D), F32),
            pltpu.VMEM((D, F), BF16), pltpu.VMEM((D, F), BF16), pltpu.VMEM((F, D), BF16),
            pltpu.SemaphoreType.DMA((2,)), pltpu.SemaphoreType.DMA((2,)), pltpu.SemaphoreType.DMA((2,)),
        ],
    )
    return pl.pallas_call(
        functools.partial(_experts_kernel, layer),
        grid_spec=grid_spec,
        out_shape=jax.ShapeDtypeStruct((n_slots, Wp), U32),
        compiler_params=_cparams(1),
        name="experts",
    )(blk_exp, n_valid, blk_par, blk_next, xs, wg, wu, wd)


def _gather_rows(slot, ys):
    K, T = slot.shape
    Wp = ys.shape[1]
    n_workers = SC_CORES * SC_SUBCORES
    per = T // n_workers
    assert per % SC_IDX_CHUNK == 0 and K % SC_GATHER_PLANES == 0, (T, K)
    win = SC_ROW_WIN // 4
    mesh = plsc.VectorSubcoreMesh(core_axis_name="core", subcore_axis_name="subcore")
    groups = [(j, k0) for j in range(SC_IDX_CHUNK // win) for k0 in range(0, K, SC_GATHER_PLANES)]

    @pl.kernel(
        out_type=jax.ShapeDtypeStruct((K, T, Wp), U32), mesh=mesh,
        scratch_types=[pltpu.VMEM((K, SC_IDX_CHUNK), I32),
                       pltpu.VMEM((2, SC_GATHER_PLANES, win, Wp), U32),
                       pltpu.SemaphoreType.DMA((2,)), pltpu.SemaphoreType.DMA((2,))])
    def gather(y_hbm, s_hbm, g_hbm, idx_v, buf, sem_g, sem_w):
        base = (lax.axis_index("core") * SC_SUBCORES + lax.axis_index("subcore")) * per

        @pl.loop(0, per // SC_IDX_CHUNK)
        def _(c):
            t0 = base + c * SC_IDX_CHUNK
            pltpu.sync_copy(s_hbm.at[:, pl.ds(t0, SC_IDX_CHUNK)], idx_v)

            def start_reads(g):
                j, k0 = groups[g]
                return [pltpu.async_copy(y_hbm.at[idx_v.at[k0 + i, pl.ds(j * win, win)]],
                                         buf.at[g % 2, i], sem_g.at[g % 2])
                        for i in range(SC_GATHER_PLANES)]

            def start_writes(g):
                j, k0 = groups[g]
                return [pltpu.async_copy(buf.at[g % 2, i], g_hbm.at[k0 + i, pl.ds(t0 + j * win, win)],
                                         sem_w.at[g % 2])
                        for i in range(SC_GATHER_PLANES)]

            reads = start_reads(0)
            writes_prev = []
            for g in range(len(groups)):
                for cp in reads:
                    cp.wait()
                writes = start_writes(g)
                for cp in writes_prev:
                    cp.wait()
                if g + 1 < len(groups):
                    reads = start_reads(g + 1)
                writes_prev = writes
            for cp in writes_prev:
                cp.wait()

    return gather(ys, slot)


def _combine_kernel(yg_ref, gate_ref, h1_ref, swg_ref, swu_ref, swd_ref, lg_ref, lb_ref, *rest):
    out_ref = rest[-1]
    h1 = h1_ref[...]
    xb = h1.astype(BF16)
    g = jnp.dot(xb, swg_ref[...], preferred_element_type=F32)
    u = jnp.dot(xb, swu_ref[...], preferred_element_type=F32)
    hmid = (g * jax.nn.sigmoid(g) * u).astype(BF16)
    shared = jnp.dot(hmid, swd_ref[...], preferred_element_type=F32)

    gates = gate_ref[...]
    tc, wp = yg_ref.shape[1], yg_ref.shape[2]
    f_lo = jnp.zeros((tc, wp), F32)
    f_hi = jnp.zeros((tc, wp), F32)
    for k in range(TOP_K):
        lo, hi = _unpack_bf16_pairs(yg_ref[k])
        gk = gates[:, k:k + 1]
        f_lo = f_lo + gk * lo
        f_hi = f_hi + gk * hi
    routed = jnp.concatenate([f_lo, f_hi], axis=1)
    out_ref[...] = _layer_norm(ALPHA * h1 + (routed + shared), lg_ref[...], lb_ref[...])


def _combine(yg, gate_t, h1, swg, swu, swd, ln_g, ln_b, part, prev_out):
    T, D = h1.shape
    K, Tp, Wp = yg.shape
    tc = min(TD_MOE, Tp)
    off = part * (Tp // tc)
    row = lambda i: (i + off, 0)
    const = lambda i: (0, 0)
    res = lambda a: pl.BlockSpec(a.shape, const)
    in_specs = [
        pl.BlockSpec((K, tc, Wp), lambda i: (0, i, 0)),
        pl.BlockSpec((tc, TOP_K), row),
        pl.BlockSpec((tc, D), row),
        res(swg), res(swu), res(swd),
        pl.BlockSpec((1, D), const),
        pl.BlockSpec((1, D), const),
    ]
    args = [yg, gate_t, h1, swg, swu, swd, ln_g.reshape(1, D), ln_b.reshape(1, D)]
    aliases = {}
    if prev_out is not None:
        in_specs.append(pl.BlockSpec(memory_space=pl.ANY))
        args.append(prev_out)
        aliases = {len(args) - 1: 0}
    return pl.pallas_call(
        _combine_kernel,
        grid=(Tp // tc,),
        in_specs=in_specs,
        out_specs=pl.BlockSpec((tc, D), row),
        out_shape=jax.ShapeDtypeStruct((T, D), F32),
        input_output_aliases=aliases,
        compiler_params=_cparams(1),
        name="combine",
    )(*args)


def _moe_layer(layer, h1, h1p, router_w, router_bias, wg, wu, wd, swg, swu, swd, ln_g, ln_b):
    T, D = h1.shape
    E = N_EXPERTS
    M = T * TOP_K
    eidx, gate, rank, cnt = _router(h1, router_w, router_bias)

    counts = cnt[:, 0].astype(I32)
    padded = (counts + BLK_MOE - 1) // BLK_MOE * BLK_MOE
    pad_end = jnp.cumsum(padded)
    pad_start = pad_end - padded
    nb = -(-(M + E * (BLK_MOE - 1)) // BLK_MOE)
    n_slots = nb * BLK_MOE
    slot = _slots(pad_start.astype(I32), eidx, rank)
    n_valid = (pad_end[-1] // BLK_MOE).astype(I32).reshape(1)
    blk_start = jnp.arange(nb, dtype=I32) * BLK_MOE
    blk_exp = jnp.minimum(jnp.sum((pad_end[None, :] <= blk_start[:, None]).astype(I32), axis=1), E - 1)

    eids = jnp.arange(E, dtype=I32)
    nonempty = counts > 0
    order = jnp.cumsum(nonempty.astype(I32)) - 1
    nxt_e = lax.cummin(jnp.where(nonempty, eids, E), reverse=True)
    nxt_e = jnp.concatenate([nxt_e[1:], jnp.full((1,), E, I32)])
    blk_par = (order % 2)[blk_exp].astype(I32)
    blk_next = jnp.where(nxt_e < E, nxt_e, -1)[blk_exp].astype(I32)

    xs = _dispatch(slot, h1p, n_slots)
    ys = _experts(layer, blk_exp, n_valid, blk_par, blk_next, xs, wg, wu, wd)
    n_parts = MOE_COMBINE_PARTS if T % (MOE_COMBINE_PARTS * SC_CORES * SC_SUBCORES * SC_IDX_CHUNK) == 0 else 1
    tp = T // n_parts
    gate_t = gate.T
    shared_w = (swg.astype(BF16), swu.astype(BF16), swd.astype(BF16))
    parts = [_gather_rows(slot[:, p * tp:(p + 1) * tp], ys) for p in range(n_parts)]
    out = None
    for p in range(n_parts):
        out = _combine(parts[p], gate_t, h1, *shared_w, ln_g, ln_b, p, out)
    return out


def kernel(x, emb_ln_g, emb_ln_b, w_in, b_gate, lam_q1, lam_k1, lam_q2, lam_k2, subln_g, w_proj_a, na_rpb, w_proj_b, w_out, ln1_g, ln1_b, router_w, router_bias, exp_w_gate, exp_w_up, exp_w_down, sh_w_gate, sh_w_up, sh_w_down, ln2_g, ln2_b):
    B, S, D = x.shape
    T = B * S
    rows = S // GRID_W
    h = x.reshape(T, D)
    for l in range(DEPTH):
        lam_init = 0.8 - 0.6 * math.exp(-0.3 * l)
        outs = _inproj(h, emb_ln_g, emb_ln_b, w_in[l].astype(BF16), apply_ln=(l == 0))
        if l == 0:
            h, outs = outs[0], outs[1:]
        qa, ka, va, qn, kn, vn, ga, gb = outs
        lamv = jnp.stack([lam_q1[l], lam_k1[l], lam_q2[l], lam_k2[l]]).astype(F32)
        oa = _diffattn(qa.reshape(B, S, -1), ka.reshape(B, S, -1), va.reshape(B, S, -1),
                       lamv, subln_g[l], lam_init)
        ob = _natten(qn.reshape(B, S, -1), kn.reshape(B, S, -1), vn.reshape(B, S, -1),
                     _na_bias_table(na_rpb[l], rows))
        h1, h1p = _mixout(oa.reshape(T, -1), ob.reshape(T, -1), ga, gb, h, b_gate[l],
                          w_proj_a[l].astype(BF16), w_proj_b[l].astype(BF16), w_out[l].astype(BF16),
                          ln1_g[l], ln1_b[l])
        h = _moe_layer(l, h1, h1p, router_w[l], router_bias[l], exp_w_gate, exp_w_up, exp_w_down,
                       sh_w_gate[l], sh_w_up[l], sh_w_down[l], ln2_g[l], ln2_b[l])
    return h.reshape(B, S, D)
```

```python
import functools
import math

import numpy as np
import jax
import jax.numpy as jnp
from jax import lax
from jax.experimental import pallas as pl
from jax.experimental.pallas import tpu as pltpu
from jax.experimental.pallas import tpu_sc as plsc

F32 = jnp.float32
BF16 = jnp.bfloat16
U32 = jnp.uint32
I32 = jnp.int32

DA_HEADS = 8
DA_HEAD_DIM = 64
NA_HEADS = 16
NA_HEAD_DIM = 32
GRID_W = 64
WIN_R = 8
WIN_C = 16
N_EXPERTS = 256
TOP_K = 8
N_GROUPS = 8
TOPK_GROUPS = 4
ROUTE_SCALE = 2.5
DEPTH = 2
ALPHA = (2 * DEPTH) ** 0.25
LN_EPS = 1e-5
LOG2E = 1.4426950408889634

LANES = 128
NA_HEADS_PER_BLOCK = LANES // NA_HEAD_DIM
VMEM_LIMIT = 56 * 1024 * 1024
SC_CORES = 2
SC_SUBCORES = 16
SC_IDX_CHUNK = 128
SC_ROW_WIN = 64
SC_GATHER_PLANES = 4

TM_PROJ = 512
TM_MIX = 1024
TQ_DA = 2048
TQ_SUB_DA = 128
NA_ROWS_PER_ITER = 4
TT_ROUTER = 512
TD_MOE = 512
BLK_MOE = 512
SUB_MOE = 256
MOE_COMBINE_PARTS = 4
NEG_BIG = -1e30


def _cparams(n_axes, flags=None):
    return pltpu.CompilerParams(
        dimension_semantics=("arbitrary",) * n_axes, vmem_limit_bytes=VMEM_LIMIT, flags=flags)


def _layer_norm(x, g, b):
    mu = jnp.mean(x, axis=-1, keepdims=True)
    xc = x - mu
    var = jnp.mean(xc * xc, axis=-1, keepdims=True)
    return xc * lax.rsqrt(var + LN_EPS) * g + b


def _pack_bf16_pairs(y):
    w = y.shape[1] // 2
    lo = lax.bitcast_convert_type(y[:, :w].astype(BF16).astype(F32), U32)
    hi = lax.bitcast_convert_type(y[:, w:].astype(BF16).astype(F32), U32)
    return (hi & jnp.uint32(0xFFFF0000)) | (lo >> 16)


def _unpack_bf16_pairs(u):
    lo = lax.bitcast_convert_type(u << 16, F32)
    hi = lax.bitcast_convert_type(u & jnp.uint32(0xFFFF0000), F32)
    return lo, hi


def _inproj_kernel(apply_ln, seg_widths, seg_scales, x_ref, g_ref, b_ref, w_ref, *out_refs):
    x = x_ref[...]
    if apply_ln:
        x = _layer_norm(x, g_ref[...], b_ref[...])
        out_refs[0][...] = x
        out_refs = out_refs[1:]
    xb = x.astype(BF16)
    off = 0
    for ref, width, scale in zip(out_refs, seg_widths, seg_scales):
        y = jnp.dot(xb, w_ref[:, off:off + width], preferred_element_type=F32)
        if scale != 1.0:
            y = y * scale
        ref[...] = y.astype(BF16)
        off += width


def _inproj(x, ln_g, ln_b, w_in_bf16, apply_ln):
    T, D = x.shape
    da_w = DA_HEADS * 2 * DA_HEAD_DIM
    na_w = NA_HEADS * NA_HEAD_DIM
    seg_widths = (da_w, da_w, da_w, na_w, na_w, na_w, D, D)
    seg_scales = (DA_HEAD_DIM ** -0.5 * LOG2E, 1.0, 1.0, NA_HEAD_DIM ** -0.5 * LOG2E, 1.0, 1.0, 1.0, 1.0)
    tm = min(TM_PROJ, T)
    n_cols = w_in_bf16.shape[1]
    row = lambda i: (i, 0)
    const = lambda i: (0, 0)
    out_shape = [jax.ShapeDtypeStruct((T, w), BF16) for w in seg_widths]
    out_specs = [pl.BlockSpec((tm, w), row) for w in seg_widths]
    if apply_ln:
        out_shape = [jax.ShapeDtypeStruct((T, D), F32)] + out_shape
        out_specs = [pl.BlockSpec((tm, D), row)] + out_specs
    return pl.pallas_call(
        functools.partial(_inproj_kernel, apply_ln, seg_widths, seg_scales),
        grid=(T // tm,),
        in_specs=[
            pl.BlockSpec((tm, D), row),
            pl.BlockSpec((1, D), const),
            pl.BlockSpec((1, D), const),
            pl.BlockSpec((D, n_cols), const, pipeline_mode=pl.Buffered(1)),
        ],
        out_specs=out_specs,
        out_shape=out_shape,
        compiler_params=_cparams(1),
        name="inproj",
    )(x, ln_g.reshape(1, D), ln_b.reshape(1, D), w_in_bf16)


def _diffattn_kernel(lam_init, slopes_ref, q_ref, k_ref, v_ref, lamv_ref, g_ref, o_ref, bias_ref):
    h = pl.program_id(0)
    qi = pl.program_id(1)
    b = pl.program_id(2)
    tq = q_ref.shape[1]
    S = k_ref.shape[1]
    d = DA_HEAD_DIM

    @pl.when(b == 0)
    def _():
        qpos = qi * tq + lax.broadcasted_iota(I32, (tq, S), 0)
        kpos = lax.broadcasted_iota(I32, (tq, S), 1)
        bias_ref[...] = jnp.abs(qpos - kpos).astype(F32) * (-slopes_ref[h])

    lv = lamv_ref[...]
    lam = (jnp.exp(jnp.sum(lv[0:1] * lv[1:2], axis=-1, keepdims=True))
           - jnp.exp(jnp.sum(lv[2:3] * lv[3:4], axis=-1, keepdims=True)) + lam_init)

    k = k_ref[0]
    v1 = jnp.concatenate([v_ref[0], jnp.ones((S, 2 * d), BF16)], axis=1)
    gain = g_ref[...] * (1.0 - lam_init)
    nt = (((1,), (1,)), ((), ()))
    ts = min(TQ_SUB_DA, tq)
    lane = lax.broadcasted_iota(I32, (ts, 2 * d), 1)

    def scores(j):
        q = q_ref[0, j * ts:(j + 1) * ts, :]
        nb = bias_ref[j * ts:(j + 1) * ts, :]
        zero = jnp.zeros_like(q)
        s1 = lax.dot_general(jnp.where(lane < d, q, zero), k, nt, preferred_element_type=F32) + nb
        s2 = lax.dot_general(jnp.where(lane >= d, q, zero), k, nt, preferred_element_type=F32) + nb
        return s1, s2

    def weights(s1, s2):
        return (jnp.exp2((s1 - jnp.max(s1, axis=-1, keepdims=True)).astype(BF16)),
                jnp.exp2((s2 - jnp.max(s2, axis=-1, keepdims=True)).astype(BF16)))

    def finish(j, e1, e2):
        p = jnp.dot(jnp.concatenate([e1, e2], axis=0), v1, preferred_element_type=F32)
        p1, p2 = p[:ts], p[ts:]
        o = p1[:, :2 * d] / p1[:, 2 * d:] - lam * (p2[:, :2 * d] / p2[:, 2 * d:])
        ms = jnp.mean(o * o, axis=-1, keepdims=True)
        o = o * lax.rsqrt(ms + LN_EPS) * gain
        o_ref[0, j * ts:(j + 1) * ts, :] = o.astype(BF16)

    n_sub = tq // ts
    s_next = scores(0)
    e_cur = weights(*s_next)
    s_next = scores(1) if n_sub > 1 else None
    for j in range(n_sub):
        s_after = scores(j + 2) if j + 2 < n_sub else None
        finish(j, *e_cur)
        if s_next is not None:
            e_cur = weights(*s_next)
        s_next = s_after


def _diffattn(qa, ka, va, lamv, subln_g, lam_init):
    B, S, W = qa.shape
    hw = 2 * DA_HEAD_DIM
    tq = min(TQ_DA, S)
    slopes = jnp.asarray(2.0 ** (-8.0 * np.arange(1, DA_HEADS + 1) / DA_HEADS) * LOG2E, F32)
    grid_spec = pltpu.PrefetchScalarGridSpec(
        num_scalar_prefetch=1,
        grid=(DA_HEADS, S // tq, B),
        in_specs=[
            pl.BlockSpec((1, tq, hw), lambda h, qi, b, sl: (b, qi, h)),
            pl.BlockSpec((1, S, hw), lambda h, qi, b, sl: (b, 0, h)),
            pl.BlockSpec((1, S, hw), lambda h, qi, b, sl: (b, 0, h)),
            pl.BlockSpec((4, DA_HEAD_DIM), lambda h, qi, b, sl: (0, 0)),
            pl.BlockSpec((1, hw), lambda h, qi, b, sl: (0, 0)),
        ],
        out_specs=pl.BlockSpec((1, tq, hw), lambda h, qi, b, sl: (b, qi, h)),
        scratch_shapes=[pltpu.VMEM((tq, S), F32)],
    )
    return pl.pallas_call(
        functools.partial(_diffattn_kernel, lam_init),
        grid_spec=grid_spec,
        out_shape=jax.ShapeDtypeStruct((B, S, W), BF16),
        compiler_params=_cparams(3),
        name="diffattn",
    )(slopes, qa, ka, va, lamv, subln_g.reshape(1, hw))


def _na_bias_table(rpb, rows):
    kh = min(WIN_R, rows)
    qcol = np.arange(GRID_W)
    kcol = np.arange(GRID_W)
    cs = np.clip(qcol - WIN_C // 2, 0, GRID_W - WIN_C)
    col_mask = (kcol[None, :] >= cs[:, None]) & (kcol[None, :] < cs[:, None] + WIN_C)
    col_off = np.clip(kcol[None, :] - qcol[:, None] + WIN_C - 1, 0, 2 * WIN_C - 2)
    v = np.arange(kh)
    j = np.arange(kh)
    row_idx = np.clip(j[None, :] - v[:, None] + WIN_R - 1, 0, 2 * WIN_R - 2)
    sel_r = np.zeros((2 * WIN_R - 1, kh * kh), np.float32)
    sel_r[row_idx.reshape(-1), np.arange(kh * kh)] = 1.0
    sel_c = np.zeros((2 * WIN_C - 1, GRID_W * GRID_W), np.float32)
    sel_c[col_off.reshape(-1), np.arange(GRID_W * GRID_W)] = 1.0
    hi = lax.Precision.HIGHEST
    t = jnp.einsum("hrc,rp->hpc", rpb * LOG2E, jnp.asarray(sel_r), precision=hi)
    t = jnp.einsum("hpc,cq->hpq", t, jnp.asarray(sel_c), precision=hi)
    t = t.reshape(NA_HEADS, kh, kh, GRID_W, GRID_W)
    t = jnp.where(jnp.asarray(col_mask)[None, None, None], t, NEG_BIG)
    t = t.transpose(1, 0, 3, 2, 4)
    nblk = NA_HEADS // NA_HEADS_PER_BLOCK
    return t.reshape(kh, nblk, NA_HEADS_PER_BLOCK * GRID_W, kh * GRID_W)


def _natten_kernel(rows, kh, q_ref, k_ref, v_ref, bias_ref, o_ref):
    hid = lax.broadcasted_iota(I32, (GRID_W, LANES), 1) // NA_HEAD_DIM

    def scores(r):
        rs = jnp.clip(r - WIN_R // 2, 0, rows - kh)
        q0 = pl.multiple_of(r * GRID_W, GRID_W)
        k0 = pl.multiple_of(rs * GRID_W, GRID_W)
        qr = q_ref[0, pl.ds(q0, GRID_W), :]
        zero = jnp.zeros_like(qr)
        qq = jnp.concatenate(
            [jnp.where(hid == hh, qr, zero) for hh in range(NA_HEADS_PER_BLOCK)], axis=0)
        kb = k_ref[0, pl.ds(k0, kh * GRID_W), :]
        s = lax.dot_general(qq, kb, (((1,), (1,)), ((), ())), preferred_element_type=F32)
        return s + bias_ref[r - rs, 0]

    def weights(s):
        e = jnp.exp2(s - jnp.max(s, axis=-1, keepdims=True))
        return e.astype(BF16), 1.0 / jnp.sum(e, axis=-1, keepdims=True)

    def finish(r, e, rl):
        rs = jnp.clip(r - WIN_R // 2, 0, rows - kh)
        q0 = pl.multiple_of(r * GRID_W, GRID_W)
        k0 = pl.multiple_of(rs * GRID_W, GRID_W)
        vb = v_ref[0, pl.ds(k0, kh * GRID_W), :]
        oo = jnp.dot(e, vb, preferred_element_type=F32) * rl
        o = jnp.zeros((GRID_W, LANES), F32)
        for hh in range(NA_HEADS_PER_BLOCK):
            o = o + jnp.where(hid == hh, oo[hh * GRID_W:(hh + 1) * GRID_W], 0.0)
        o_ref[0, pl.ds(q0, GRID_W), :] = o.astype(BF16)

    def row_group(g, carry):
        rr = [g * NA_ROWS_PER_ITER + i for i in range(NA_ROWS_PER_ITER)]
        ss = [scores(r) for r in rr]
        ws = [weights(s) for s in ss]
        for r, (e, rl) in zip(rr, ws):
            finish(r, e, rl)
        return carry

    lax.fori_loop(0, rows // NA_ROWS_PER_ITER, row_group, 0)


def _natten(qn, kn, vn, bias_tab):
    B, S, W = qn.shape
    rows = S // GRID_W
    kh = min(WIN_R, rows)
    nblk = W // LANES
    blk = lambda g, b: (b, 0, g)
    return pl.pallas_call(
        functools.partial(_natten_kernel, rows, kh),
        grid=(nblk, B),
        in_specs=[
            pl.BlockSpec((1, S, LANES), blk),
            pl.BlockSpec((1, S, LANES), blk),
            pl.BlockSpec((1, S, LANES), blk),
            pl.BlockSpec((kh, 1, NA_HEADS_PER_BLOCK * GRID_W, kh * GRID_W), lambda g, b: (0, g, 0, 0)),
        ],
        out_specs=pl.BlockSpec((1, S, LANES), blk),
        out_shape=jax.ShapeDtypeStruct((B, S, W), BF16),
        compiler_params=_cparams(2),
        name="natten",
    )(qn, kn, vn, bias_tab)


def _mixout_kernel(oa_ref, ob_ref, ga_ref, gb_ref, h_ref, bg_ref, wa_ref, wb_ref, wo_ref,
                   lg_ref, lb_ref, h1_ref, h1p_ref):
    ya = jnp.dot(oa_ref[...], wa_ref[...], preferred_element_type=F32)
    yb = jnp.dot(ob_ref[...], wb_ref[...], preferred_element_type=F32)
    g_a = jax.nn.sigmoid(ga_ref[...].astype(F32) + bg_ref[0:1, :])
    g_b = jax.nn.sigmoid(gb_ref[...].astype(F32) + bg_ref[1:2, :])
    z = (g_a * ya + g_b * yb).astype(BF16)
    m = jnp.dot(z, wo_ref[...], preferred_element_type=F32)
    y = _layer_norm(ALPHA * h_ref[...] + m, lg_ref[...], lb_ref[...])
    h1_ref[...] = y
    h1p_ref[...] = _pack_bf16_pairs(y)


def _mixout(oa, ob, ga, gb, h, b_gate, wa, wb, wo, ln_g, ln_b):
    T, D = h.shape
    tm = min(TM_MIX, T)
    row = lambda i: (i, 0)
    const = lambda i: (0, 0)
    res = lambda a: pl.BlockSpec(a.shape, const, pipeline_mode=pl.Buffered(1))
    return pl.pallas_call(
        _mixout_kernel,
        grid=(T // tm,),
        in_specs=[
            pl.BlockSpec((tm, oa.shape[1]), row),
            pl.BlockSpec((tm, ob.shape[1]), row),
            pl.BlockSpec((tm, D), row),
            pl.BlockSpec((tm, D), row),
            pl.BlockSpec((tm, D), row),
            pl.BlockSpec((2, D), const),
            res(wa), res(wb), res(wo),
            pl.BlockSpec((1, D), const),
            pl.BlockSpec((1, D), const),
        ],
        out_specs=[pl.BlockSpec((tm, D), row), pl.BlockSpec((tm, D // 2), row)],
        out_shape=[jax.ShapeDtypeStruct((T, D), F32), jax.ShapeDtypeStruct((T, D // 2), U32)],
        compiler_params=_cparams(1),
        name="mixout",
    )(oa, ob, ga, gb, h, b_gate, wa, wb, wo, ln_g.reshape(1, D), ln_b.reshape(1, D))


def _router_kernel(h_ref, rwt_ref, rb_ref, tri_ref, eidx_ref, gate_ref, rank_ref, cnt_ref, carry_ref):
    i = pl.program_id(0)
    tt = h_ref.shape[0]
    E, G = N_EXPERTS, N_GROUPS
    P = E // G
    neg = -jnp.inf

    @pl.when(i == 0)
    def _():
        carry_ref[...] = jnp.zeros_like(carry_ref)

    hb = h_ref[...].astype(BF16)
    logits = lax.dot_general(rwt_ref[...], hb, (((1,), (1,)), ((), ())), preferred_element_type=F32)
    scores = jax.nn.sigmoid(logits)
    biased = scores + rb_ref[...]
    b3 = biased.reshape(G, P, tt)
    s3 = scores.reshape(G, P, tt)
    pi = lax.broadcasted_iota(I32, (G, P, tt), 1)
    ei = lax.broadcasted_iota(I32, (G, P, tt), 0) * P + pi

    m1 = jnp.max(b3, axis=1, keepdims=True)
    i1 = jnp.min(jnp.where(b3 == m1, pi, P), axis=1, keepdims=True)
    m2 = jnp.max(jnp.where(pi == i1, neg, b3), axis=1, keepdims=True)
    grp = m1 + m2
    gi = lax.broadcasted_iota(I32, (G, 1, tt), 0)
    gsel = jnp.zeros((G, 1, tt), F32)
    for _ in range(TOPK_GROUPS):
        gm = jnp.max(grp, axis=0, keepdims=True)
        gidx = jnp.min(jnp.where(grp == gm, gi, G), axis=0, keepdims=True)
        hit = gi == gidx
        gsel = jnp.where(hit, 1.0, gsel)
        grp = jnp.where(hit, neg, grp)

    cand = jnp.where(gsel > 0.0, b3, neg)
    sel = jnp.zeros((G, P, tt), F32)
    eids, gates = [], []
    for _ in range(TOP_K):
        mk = jnp.max(cand, axis=(0, 1), keepdims=True)
        ik = jnp.min(jnp.where(cand == mk, ei, E), axis=(0, 1), keepdims=True)
        hit = ei == ik
        gates.append(jnp.sum(jnp.where(hit, s3, 0.0), axis=(0, 1), keepdims=True))
        eids.append(ik)
        cand = jnp.where(hit, neg, cand)
        sel = jnp.where(hit, 1.0, sel)

    gsum = gates[0]
    for gk in gates[1:]:
        gsum = gsum + gk
    gscale = ROUTE_SCALE / gsum

    sel2 = sel.reshape(E, tt)
    prefix = jnp.dot(sel2.astype(BF16), tri_ref[...], preferred_element_type=F32)
    base3 = (prefix + carry_ref[...]).reshape(G, P, tt)
    ranks = [jnp.sum(jnp.where(ei == ik, base3, 0.0), axis=(0, 1), keepdims=True) for ik in eids]
    carry_ref[...] = carry_ref[...] + jnp.sum(sel2, axis=1, keepdims=True)
    cnt_ref[...] = carry_ref[...]

    eidx_ref[...] = jnp.concatenate([x.reshape(1, tt) for x in eids], axis=0)
    gate_ref[...] = jnp.concatenate([(g * gscale).reshape(1, tt) for g in gates], axis=0)
    rank_ref[...] = jnp.concatenate([x.reshape(1, tt) for x in ranks], axis=0).astype(I32)


def _router(h1, router_w, router_bias):
    T, D = h1.shape
    E = N_EXPERTS
    tt = min(TT_ROUTER, T)
    rwt = router_w.T.astype(BF16)
    tri = jnp.asarray(np.triu(np.ones((tt, tt), np.float32), k=1), BF16)
    const = lambda i: (0, 0)
    col = lambda i: (0, i)
    return pl.pallas_call(
        _router_kernel,
        grid=(T // tt,),
        in_specs=[
            pl.BlockSpec((tt, D), lambda i: (i, 0)),
            pl.BlockSpec((E, D), const),
            pl.BlockSpec((E, 1), const),
            pl.BlockSpec((tt, tt), const),
        ],
        out_specs=[
            pl.BlockSpec((TOP_K, tt), col),
            pl.BlockSpec((TOP_K, tt), col),
            pl.BlockSpec((TOP_K, tt), col),
            pl.BlockSpec((E, 1), const),
        ],
        out_shape=[
            jax.ShapeDtypeStruct((TOP_K, T), I32),
            jax.ShapeDtypeStruct((TOP_K, T), F32),
            jax.ShapeDtypeStruct((TOP_K, T), I32),
            jax.ShapeDtypeStruct((E, 1), F32),
        ],
        scratch_shapes=[pltpu.VMEM((E, 1), F32)],
        compiler_params=_cparams(1),
        name="router",
    )(h1, rwt, router_bias.reshape(E, 1).astype(F32), tri)


def _slots_kernel(pstart_ref, eidx_ref, rank_ref, slot_ref):
    eidx = eidx_ref[...]

    def add_expert(e, acc):
        return acc + jnp.where(eidx == e, pstart_ref[e], 0)

    slot_ref[...] = lax.fori_loop(0, N_EXPERTS, add_expert, rank_ref[...], unroll=8)


def _slots(pad_start, eidx, rank):
    K, T = eidx.shape
    tt = min(2048, T)
    col = lambda i, ps: (0, i)
    grid_spec = pltpu.PrefetchScalarGridSpec(
        num_scalar_prefetch=1,
        grid=(T // tt,),
        in_specs=[pl.BlockSpec((K, tt), col), pl.BlockSpec((K, tt), col)],
        out_specs=pl.BlockSpec((K, tt), col),
    )
    return pl.pallas_call(
        _slots_kernel,
        grid_spec=grid_spec,
        out_shape=jax.ShapeDtypeStruct((K, T), I32),
        compiler_params=_cparams(1),
        name="slots",
    )(pad_start, eidx, rank)


def _dispatch(slot, h1p, n_slots):
    T, Wp = h1p.shape
    n_workers = SC_CORES * SC_SUBCORES
    per = T // n_workers
    assert per % SC_IDX_CHUNK == 0, (T, n_workers)
    mesh = plsc.VectorSubcoreMesh(core_axis_name="core", subcore_axis_name="subcore")

    @pl.kernel(
        out_type=jax.ShapeDtypeStruct((n_slots, Wp), U32), mesh=mesh,
        scratch_types=[pltpu.VMEM((TOP_K, SC_IDX_CHUNK), I32),
                       pltpu.VMEM((SC_IDX_CHUNK // SC_ROW_WIN, SC_ROW_WIN, Wp), U32),
                       pltpu.SemaphoreType.DMA((SC_IDX_CHUNK // SC_ROW_WIN,)), pltpu.SemaphoreType.DMA])
    def dispatch(x_hbm, s_hbm, o_hbm, idx_v, x_v, sem_r, sem_w):
        base = (lax.axis_index("core") * SC_SUBCORES + lax.axis_index("subcore")) * per
        n_win = SC_IDX_CHUNK // SC_ROW_WIN

        @pl.loop(0, per // SC_IDX_CHUNK)
        def _(c):
            t0 = base + c * SC_IDX_CHUNK
            reads = [pltpu.async_copy(x_hbm.at[pl.ds(t0 + j * SC_ROW_WIN, SC_ROW_WIN)], x_v.at[j], sem_r.at[j])
                     for j in range(n_win)]
            pltpu.sync_copy(s_hbm.at[:, pl.ds(t0, SC_IDX_CHUNK)], idx_v)
            copies = []
            for j in range(n_win):
                reads[j].wait()
                copies += [
                    pltpu.async_copy(x_v.at[j], o_hbm.at[idx_v.at[k, pl.ds(j * SC_ROW_WIN, SC_ROW_WIN)]], sem_w)
                    for k in range(TOP_K)]
            for cp in copies:
                cp.wait()

    return dispatch(h1p, slot)


def _experts_kernel(layer, be_ref, nv_ref, par_ref, nxt_ref, xs_hbm, wg_hbm, wu_hbm, wd_hbm, ys_hbm,
                    xbuf, ybuf, wg_f, wu_f, wd_f, wg_s, wu_s, wd_s, sem_w, sem_x, sem_y):
    nv = nv_ref[0]
    blk = xbuf.shape[1]
    sub = min(SUB_MOE, blk)
    n_sub = blk // sub

    def rows(i):
        return pl.ds(pl.multiple_of(i * blk, blk), blk)

    def x_copy(i, slot):
        return pltpu.make_async_copy(xs_hbm.at[rows(i)], xbuf.at[slot], sem_x.at[slot])

    def y_copy(i, slot):
        return pltpu.make_async_copy(ybuf.at[slot], ys_hbm.at[rows(i)], sem_y.at[slot])

    def weight_copies(e, slot):
        return [pltpu.make_async_copy(src.at[layer, e], dst.at[slot], sem_w.at[slot])
                for src, dst in ((wg_hbm, wg_f), (wu_hbm, wu_f), (wd_hbm, wd_f))]

    n_xbuf = xbuf.shape[0]
    for a in range(n_xbuf - 1):
        @pl.when(a < nv)
        def _():
            x_copy(a, a).start()
    for cp in weight_copies(be_ref[0], par_ref[0]):
        cp.start()

    def block(i, carry):
        slot = lax.rem(i, 2)
        xslot = lax.rem(i, n_xbuf)
        e = be_ref[i]
        wslot = par_ref[i]

        @pl.when(i + (n_xbuf - 1) < nv)
        def _():
            x_copy(i + (n_xbuf - 1), lax.rem(i + (n_xbuf - 1), n_xbuf)).start()

        @pl.when((i == 0) | (e != be_ref[jnp.maximum(i - 1, 0)]))
        def _():
            for cp in weight_copies(e, wslot):
                cp.wait()
            wg_s[...] = wg_f[wslot].astype(BF16)
            wu_s[...] = wu_f[wslot].astype(BF16)
            wd_s[...] = wd_f[wslot].astype(BF16)

            @pl.when(nxt_ref[i] >= 0)
            def _():
                for cp in weight_copies(nxt_ref[i], 1 - wslot):
                    cp.start()

        x_copy(i, xslot).wait()

        @pl.when(i >= 2)
        def _():
            y_copy(i - 2, slot).wait()

        def up(j):
            lo, hi = _unpack_bf16_pairs(xbuf[xslot, j * sub:(j + 1) * sub, :])
            x = jnp.concatenate([lo.astype(BF16), hi.astype(BF16)], axis=1)
            return (jnp.dot(x, wg_s[...], preferred_element_type=F32),
                    jnp.dot(x, wu_s[...], preferred_element_type=F32))

        def down(j, g, u):
            hmid = (g * jax.nn.sigmoid(g) * u).astype(BF16)
            y = jnp.dot(hmid, wd_s[...], preferred_element_type=F32)
            ybuf[slot, j * sub:(j + 1) * sub, :] = _pack_bf16_pairs(y)

        pending = up(0)
        for j in range(n_sub):
            nxt = up(j + 1) if j + 1 < n_sub else None
            down(j, *pending)
            pending = nxt

        y_copy(i, slot).start()
        return carry

    lax.fori_loop(0, nv, block, 0)

    @pl.when(nv >= 2)
    def _():
        y_copy(nv - 2, lax.rem(nv, 2)).wait()

    y_copy(nv - 1, lax.rem(nv - 1, 2)).wait()


def _experts(layer, blk_exp, n_valid, blk_par, blk_next, xs, wg, wu, wd):
    n_slots, Wp = xs.shape
    _, E, D, F = wg.shape

    hbm = pl.BlockSpec(memory_space=pl.ANY)
    grid_spec = pltpu.PrefetchScalarGridSpec(
        num_scalar_prefetch=4,
        grid=(1,),
        in_specs=[hbm, hbm, hbm, hbm],
        out_specs=hbm,
        scratch_shapes=[
            pltpu.VMEM((3, BLK_MOE, Wp), U32), pltpu.VMEM((2, BLK_MOE, Wp), U32),
            pltpu.VMEM((2, D, F), F32), pltpu.VMEM((2, D, F), F32), pltpu.VMEM((2, F, D), F32),
            pltpu.VMEM((D, F), BF16), pltpu.VMEM((D, F), BF16), pltpu.VMEM((F, D), BF16),
            pltpu.SemaphoreType.DMA((2,)), pltpu.SemaphoreType.DMA((3,)), pltpu.SemaphoreType.DMA((2,)),
        ],
    )
    return pl.pallas_call(
        functools.partial(_experts_kernel, layer),
        grid_spec=grid_spec,
        out_shape=jax.ShapeDtypeStruct((n_slots, Wp), U32),
        compiler_params=_cparams(1),
        name="experts",
    )(blk_exp, n_valid, blk_par, blk_next, xs, wg, wu, wd)


def _gather_rows(slot, ys):
    K, T = slot.shape
    Wp = ys.shape[1]
    n_workers = SC_CORES * SC_SUBCORES
    per = T // n_workers
    assert per % SC_IDX_CHUNK == 0 and K % SC_GATHER_PLANES == 0, (T, K)
    win = SC_ROW_WIN // 4
    mesh = plsc.VectorSubcoreMesh(core_axis_name="core", subcore_axis_name="subcore")
    groups = [(j, k0) for j in range(SC_IDX_CHUNK // win) for k0 in range(0, K, SC_GATHER_PLANES)]

    @pl.kernel(
        out_type=jax.ShapeDtypeStruct((K, T, Wp), U32), mesh=mesh,
        scratch_types=[pltpu.VMEM((K, SC_IDX_CHUNK), I32),
                       pltpu.VMEM((2, SC_GATHER_PLANES, win, Wp), U32),
                       pltpu.SemaphoreType.DMA((2,)), pltpu.SemaphoreType.DMA((2,))])
    def gather(y_hbm, s_hbm, g_hbm, idx_v, buf, sem_g, sem_w):
        base = (lax.axis_index("core") * SC_SUBCORES + lax.axis_index("subcore")) * per

        @pl.loop(0, per // SC_IDX_CHUNK)
        def _(c):
            t0 = base + c * SC_IDX_CHUNK
            pltpu.sync_copy(s_hbm.at[:, pl.ds(t0, SC_IDX_CHUNK)], idx_v)

            def start_reads(g):
                j, k0 = groups[g]
                return [pltpu.async_copy(y_hbm.at[idx_v.at[k0 + i, pl.ds(j * win, win)]],
                                         buf.at[g % 2, i], sem_g.at[g % 2])
                        for i in range(SC_GATHER_PLANES)]

            def start_writes(g):
                j, k0 = groups[g]
                return [pltpu.async_copy(buf.at[g % 2, i], g_hbm.at[k0 + i, pl.ds(t0 + j * win, win)],
                                         sem_w.at[g % 2])
                        for i in range(SC_GATHER_PLANES)]

            reads = start_reads(0)
            writes_prev = []
            for g in range(len(groups)):
                for cp in reads:
                    cp.wait()
                writes = start_writes(g)
                for cp in writes_prev:
                    cp.wait()
                if g + 1 < len(groups):
                    reads = start_reads(g + 1)
                writes_prev = writes
            for cp in writes_prev:
                cp.wait()

    return gather(ys, slot)


def _combine_kernel(yg_ref, gate_ref, h1_ref, swg_ref, swu_ref, swd_ref, lg_ref, lb_ref, *rest):
    out_ref = rest[-1]
    h1 = h1_ref[...]
    xb = h1.astype(BF16)
    g = jnp.dot(xb, swg_ref[...], preferred_element_type=F32)
    u = jnp.dot(xb, swu_ref[...], preferred_element_type=F32)
    hmid = (g * jax.nn.sigmoid(g) * u).astype(BF16)
    shared = jnp.dot(hmid, swd_ref[...], preferred_element_type=F32)

    gates = gate_ref[...]
    tc, wp = yg_ref.shape[1], yg_ref.shape[2]
    f_lo = jnp.zeros((tc, wp), F32)
    f_hi = jnp.zeros((tc, wp), F32)
    for k in range(TOP_K):
        lo, hi = _unpack_bf16_pairs(yg_ref[k])
        gk = gates[:, k:k + 1]
        f_lo = f_lo + gk * lo
        f_hi = f_hi + gk * hi
    routed = jnp.concatenate([f_lo, f_hi], axis=1)
    out_ref[...] = _layer_norm(ALPHA * h1 + (routed + shared), lg_ref[...], lb_ref[...])


def _combine(yg, gate_t, h1, swg, swu, swd, ln_g, ln_b, part, prev_out):
    T, D = h1.shape
    K, Tp, Wp = yg.shape
    tc = min(TD_MOE, Tp)
    off = part * (Tp // tc)
    row = lambda i: (i + off, 0)
    const = lambda i: (0, 0)
    res = lambda a: pl.BlockSpec(a.shape, const)
    in_specs = [
        pl.BlockSpec((K, tc, Wp), lambda i: (0, i, 0)),
        pl.BlockSpec((tc, TOP_K), row),
        pl.BlockSpec((tc, D), row),
        res(swg), res(swu), res(swd),
        pl.BlockSpec((1, D), const),
        pl.BlockSpec((1, D), const),
    ]
    args = [yg, gate_t, h1, swg, swu, swd, ln_g.reshape(1, D), ln_b.reshape(1, D)]
    aliases = {}
    if prev_out is not None:
        in_specs.append(pl.BlockSpec(memory_space=pl.ANY))
        args.append(prev_out)
        aliases = {len(args) - 1: 0}
    return pl.pallas_call(
        _combine_kernel,
        grid=(Tp // tc,),
        in_specs=in_specs,
        out_specs=pl.BlockSpec((tc, D), row),
        out_shape=jax.ShapeDtypeStruct((T, D), F32),
        input_output_aliases=aliases,
        compiler_params=_cparams(1),
        name="combine",
    )(*args)


def _moe_layer(layer, h1, h1p, router_w, router_bias, wg, wu, wd, swg, swu, swd, ln_g, ln_b):
    T, D = h1.shape
    E = N_EXPERTS
    M = T * TOP_K
    eidx, gate, rank, cnt = _router(h1, router_w, router_bias)

    counts = cnt[:, 0].astype(I32)
    padded = (counts + BLK_MOE - 1) // BLK_MOE * BLK_MOE
    pad_end = jnp.cumsum(padded)
    pad_start = pad_end - padded
    nb = -(-(M + E * (BLK_MOE - 1)) // BLK_MOE)
    n_slots = nb * BLK_MOE
    slot = _slots(pad_start.astype(I32), eidx, rank)
    n_valid = (pad_end[-1] // BLK_MOE).astype(I32).reshape(1)
    blk_start = jnp.arange(nb, dtype=I32) * BLK_MOE
    blk_exp = jnp.minimum(jnp.sum((pad_end[None, :] <= blk_start[:, None]).astype(I32), axis=1), E - 1)

    eids = jnp.arange(E, dtype=I32)
    nonempty = counts > 0
    order = jnp.cumsum(nonempty.astype(I32)) - 1
    nxt_e = lax.cummin(jnp.where(nonempty, eids, E), reverse=True)
    nxt_e = jnp.concatenate([nxt_e[1:], jnp.full((1,), E, I32)])
    blk_par = (order % 2)[blk_exp].astype(I32)
    blk_next = jnp.where(nxt_e < E, nxt_e, -1)[blk_exp].astype(I32)

    xs = _dispatch(slot, h1p, n_slots)
    ys = _experts(layer, blk_exp, n_valid, blk_par, blk_next, xs, wg, wu, wd)
    n_parts = MOE_COMBINE_PARTS if T % (MOE_COMBINE_PARTS * SC_CORES * SC_SUBCORES * SC_IDX_CHUNK) == 0 else 1
    tp = T // n_parts
    gate_t = gate.T
    shared_w = (swg.astype(BF16), swu.astype(BF16), swd.astype(BF16))
    parts = [_gather_rows(slot[:, p * tp:(p + 1) * tp], ys) for p in range(n_parts)]
    out = None
    for p in range(n_parts):
        out = _combine(parts[p], gate_t, h1, *shared_w, ln_g, ln_b, p, out)
    return out


def kernel(x, emb_ln_g, emb_ln_b, w_in, b_gate, lam_q1, lam_k1, lam_q2, lam_k2, subln_g, w_proj_a, na_rpb, w_proj_b, w_out, ln1_g, ln1_b, router_w, router_bias, exp_w_gate, exp_w_up, exp_w_down, sh_w_gate, sh_w_up, sh_w_down, ln2_g, ln2_b):
    B, S, D = x.shape
    T = B * S
    rows = S // GRID_W
    h = x.reshape(T, D)
    for l in range(DEPTH):
        lam_init = 0.8 - 0.6 * math.exp(-0.3 * l)
        outs = _inproj(h, emb_ln_g, emb_ln_b, w_in[l].astype(BF16), apply_ln=(l == 0))
        if l == 0:
            h, outs = outs[0], outs[1:]
        qa, ka, va, qn, kn, vn, ga, gb = outs
        lamv = jnp.stack([lam_q1[l], lam_k1[l], lam_q2[l], lam_k2[l]]).astype(F32)
        oa = _diffattn(qa.reshape(B, S, -1), ka.reshape(B, S, -1), va.reshape(B, S, -1),
                       lamv, subln_g[l], lam_init)
        ob = _natten(qn.reshape(B, S, -1), kn.reshape(B, S, -1), vn.reshape(B, S, -1),
                     _na_bias_table(na_rpb[l], rows))
        h1, h1p = _mixout(oa.reshape(T, -1), ob.reshape(T, -1), ga, gb, h, b_gate[l],
                          w_proj_a[l].astype(BF16), w_proj_b[l].astype(BF16), w_out[l].astype(BF16),
                          ln1_g[l], ln1_b[l])
        h = _moe_layer(l, h1, h1p, router_w[l], router_bias[l], exp_w_gate, exp_w_up, exp_w_down,
                       sh_w_gate[l], sh_w_up[l], sh_w_down[l], ln2_g[l], ln2_b[l])
    return h.reshape(B, S, D)
```

```python
import functools
import math

import numpy as np
import jax
import jax.numpy as jnp
from jax import lax
from jax.experimental import pallas as pl
from jax.experimental.pallas import tpu as pltpu
from jax.experimental.pallas import tpu_sc as plsc

F32 = jnp.float32
BF16 = jnp.bfloat16
U32 = jnp.uint32
I32 = jnp.int32

DA_HEADS = 8
DA_HEAD_DIM = 64
NA_HEADS = 16
NA_HEAD_DIM = 32
GRID_W = 64
WIN_R = 8
WIN_C = 16
N_EXPERTS = 256
TOP_K = 8
N_GROUPS = 8
TOPK_GROUPS = 4
ROUTE_SCALE = 2.5
DEPTH = 2
ALPHA = (2 * DEPTH) ** 0.25
LN_EPS = 1e-5
LOG2E = 1.4426950408889634

LANES = 128
NA_HEADS_PER_BLOCK = LANES // NA_HEAD_DIM
VMEM_LIMIT = 56 * 1024 * 1024
SC_CORES = 2
SC_SUBCORES = 16
SC_IDX_CHUNK = 128
SC_ROW_WIN = 64
SC_GATHER_PLANES = 4

TM_PROJ = 512
TM_MIX = 1024
TQ_DA = 2048
TQ_SUB_DA = 128
NA_ROWS_PER_ITER = 4
TT_ROUTER = 1024
TD_MOE = 512
BLK_MOE = 512
SUB_MOE = 256
MOE_COMBINE_PARTS = 8
NEG_BIG = -1e30


def _cparams(n_axes, flags=None):
    return pltpu.CompilerParams(
        dimension_semantics=("arbitrary",) * n_axes, vmem_limit_bytes=VMEM_LIMIT, flags=flags)


def _layer_norm(x, g, b):
    mu = jnp.mean(x, axis=-1, keepdims=True)
    xc = x - mu
    var = jnp.mean(xc * xc, axis=-1, keepdims=True)
    return xc * lax.rsqrt(var + LN_EPS) * g + b


def _pack_bf16_pairs(y):
    w = y.shape[1] // 2
    lo = lax.bitcast_convert_type(y[:, :w].astype(BF16).astype(F32), U32)
    hi = lax.bitcast_convert_type(y[:, w:].astype(BF16).astype(F32), U32)
    return (hi & jnp.uint32(0xFFFF0000)) | (lo >> 16)


def _unpack_bf16_pairs(u):
    lo = lax.bitcast_convert_type(u << 16, F32)
    hi = lax.bitcast_convert_type(u & jnp.uint32(0xFFFF0000), F32)
    return lo, hi


def _inproj_kernel(apply_ln, seg_widths, seg_scales, x_ref, g_ref, b_ref, w_ref, *out_refs):
    x = x_ref[...]
    if apply_ln:
        x = _layer_norm(x, g_ref[...], b_ref[...])
        out_refs[0][...] = x
        out_refs = out_refs[1:]
    xb = x.astype(BF16)
    off = 0
    for ref, width, scale in zip(out_refs, seg_widths, seg_scales):
        y = jnp.dot(xb, w_ref[:, off:off + width], preferred_element_type=F32)
        if scale != 1.0:
            y = y * scale
        ref[...] = y.astype(BF16)
        off += width


def _inproj(x, ln_g, ln_b, w_in_bf16, apply_ln):
    T, D = x.shape
    da_w = DA_HEADS * 2 * DA_HEAD_DIM
    na_w = NA_HEADS * NA_HEAD_DIM
    seg_widths = (da_w, da_w, da_w, na_w, na_w, na_w, D, D)
    seg_scales = (DA_HEAD_DIM ** -0.5 * LOG2E, 1.0, 1.0, NA_HEAD_DIM ** -0.5 * LOG2E, 1.0, 1.0, 1.0, 1.0)
    tm = min(TM_PROJ, T)
    n_cols = w_in_bf16.shape[1]
    row = lambda i: (i, 0)
    const = lambda i: (0, 0)
    out_shape = [jax.ShapeDtypeStruct((T, w), BF16) for w in seg_widths]
    out_specs = [pl.BlockSpec((tm, w), row) for w in seg_widths]
    if apply_ln:
        out_shape = [jax.ShapeDtypeStruct((T, D), F32)] + out_shape
        out_specs = [pl.BlockSpec((tm, D), row)] + out_specs
    return pl.pallas_call(
        functools.partial(_inproj_kernel, apply_ln, seg_widths, seg_scales),
        grid=(T // tm,),
        in_specs=[
            pl.BlockSpec((tm, D), row),
            pl.BlockSpec((1, D), const),
            pl.BlockSpec((1, D), const),
            pl.BlockSpec((D, n_cols), const, pipeline_mode=pl.Buffered(1)),
        ],
        out_specs=out_specs,
        out_shape=out_shape,
        compiler_params=_cparams(1),
        name="inproj",
    )(x, ln_g.reshape(1, D), ln_b.reshape(1, D), w_in_bf16)


def _diffattn_kernel(lam_init, slopes_ref, q_ref, k_ref, v_ref, lamv_ref, g_ref, o_ref, bias_ref):
    h = pl.program_id(0)
    qi = pl.program_id(1)
    b = pl.program_id(2)
    tq = q_ref.shape[1]
    S = k_ref.shape[1]
    d = DA_HEAD_DIM

    @pl.when(b == 0)
    def _():
        qpos = qi * tq + lax.broadcasted_iota(I32, (tq, S), 0)
        kpos = lax.broadcasted_iota(I32, (tq, S), 1)
        bias_ref[...] = jnp.abs(qpos - kpos).astype(F32) * (-slopes_ref[h])

    lv = lamv_ref[...]
    lam = (jnp.exp(jnp.sum(lv[0:1] * lv[1:2], axis=-1, keepdims=True))
           - jnp.exp(jnp.sum(lv[2:3] * lv[3:4], axis=-1, keepdims=True)) + lam_init)

    k = k_ref[0]
    v1 = jnp.concatenate([v_ref[0], jnp.ones((S, 2 * d), BF16)], axis=1)
    gain = g_ref[...] * (1.0 - lam_init)
    nt = (((1,), (1,)), ((), ()))
    ts = min(TQ_SUB_DA, tq)
    lane = lax.broadcasted_iota(I32, (ts, 2 * d), 1)

    def scores(j):
        q = q_ref[0, j * ts:(j + 1) * ts, :]
        nb = bias_ref[j * ts:(j + 1) * ts, :]
        zero = jnp.zeros_like(q)
        s1 = lax.dot_general(jnp.where(lane < d, q, zero), k, nt, preferred_element_type=F32) + nb
        s2 = lax.dot_general(jnp.where(lane >= d, q, zero), k, nt, preferred_element_type=F32) + nb
        return s1, s2

    def weights(s1, s2):
        return (jnp.exp2((s1 - jnp.max(s1, axis=-1, keepdims=True)).astype(BF16)),
                jnp.exp2((s2 - jnp.max(s2, axis=-1, keepdims=True)).astype(BF16)))

    def finish(j, e1, e2):
        p = jnp.dot(jnp.concatenate([e1, e2], axis=0), v1, preferred_element_type=F32)
        p1, p2 = p[:ts], p[ts:]
        o = p1[:, :2 * d] / p1[:, 2 * d:] - lam * (p2[:, :2 * d] / p2[:, 2 * d:])
        ms = jnp.mean(o * o, axis=-1, keepdims=True)
        o = o * lax.rsqrt(ms + LN_EPS) * gain
        o_ref[0, j * ts:(j + 1) * ts, :] = o.astype(BF16)

    n_sub = tq // ts
    s_next = scores(0)
    e_cur = weights(*s_next)
    s_next = scores(1) if n_sub > 1 else None
    for j in range(n_sub):
        s_after = scores(j + 2) if j + 2 < n_sub else None
        finish(j, *e_cur)
        if s_next is not None:
            e_cur = weights(*s_next)
        s_next = s_after


def _diffattn(qa, ka, va, lamv, subln_g, lam_init):
    B, S, W = qa.shape
    hw = 2 * DA_HEAD_DIM
    tq = min(TQ_DA, S)
    slopes = jnp.asarray(2.0 ** (-8.0 * np.arange(1, DA_HEADS + 1) / DA_HEADS) * LOG2E, F32)
    grid_spec = pltpu.PrefetchScalarGridSpec(
        num_scalar_prefetch=1,
        grid=(DA_HEADS, S // tq, B),
        in_specs=[
            pl.BlockSpec((1, tq, hw), lambda h, qi, b, sl: (b, qi, h)),
            pl.BlockSpec((1, S, hw), lambda h, qi, b, sl: (b, 0, h)),
            pl.BlockSpec((1, S, hw), lambda h, qi, b, sl: (b, 0, h)),
            pl.BlockSpec((4, DA_HEAD_DIM), lambda h, qi, b, sl: (0, 0)),
            pl.BlockSpec((1, hw), lambda h, qi, b, sl: (0, 0)),
        ],
        out_specs=pl.BlockSpec((1, tq, hw), lambda h, qi, b, sl: (b, qi, h)),
        scratch_shapes=[pltpu.VMEM((tq, S), F32)],
    )
    return pl.pallas_call(
        functools.partial(_diffattn_kernel, lam_init),
        grid_spec=grid_spec,
        out_shape=jax.ShapeDtypeStruct((B, S, W), BF16),
        compiler_params=_cparams(3),
        name="diffattn",
    )(slopes, qa, ka, va, lamv, subln_g.reshape(1, hw))


def _na_bias_table(rpb, rows):
    kh = min(WIN_R, rows)
    qcol = np.arange(GRID_W)
    kcol = np.arange(GRID_W)
    cs = np.clip(qcol - WIN_C // 2, 0, GRID_W - WIN_C)
    col_mask = (kcol[None, :] >= cs[:, None]) & (kcol[None, :] < cs[:, None] + WIN_C)
    col_off = np.clip(kcol[None, :] - qcol[:, None] + WIN_C - 1, 0, 2 * WIN_C - 2)
    v = np.arange(kh)
    j = np.arange(kh)
    row_idx = np.clip(j[None, :] - v[:, None] + WIN_R - 1, 0, 2 * WIN_R - 2)
    sel_r = np.zeros((2 * WIN_R - 1, kh * kh), np.float32)
    sel_r[row_idx.reshape(-1), np.arange(kh * kh)] = 1.0
    sel_c = np.zeros((2 * WIN_C - 1, GRID_W * GRID_W), np.float32)
    sel_c[col_off.reshape(-1), np.arange(GRID_W * GRID_W)] = 1.0
    hi = lax.Precision.HIGHEST
    t = jnp.einsum("hrc,rp->hpc", rpb * LOG2E, jnp.asarray(sel_r), precision=hi)
    t = jnp.einsum("hpc,cq->hpq", t, jnp.asarray(sel_c), precision=hi)
    t = t.reshape(NA_HEADS, kh, kh, GRID_W, GRID_W)
    t = jnp.where(jnp.asarray(col_mask)[None, None, None], t, NEG_BIG)
    t = t.transpose(1, 0, 3, 2, 4)
    nblk = NA_HEADS // NA_HEADS_PER_BLOCK
    return t.reshape(kh, nblk, NA_HEADS_PER_BLOCK * GRID_W, kh * GRID_W)


def _natten_kernel(rows, kh, q_ref, k_ref, v_ref, bias_ref, o_ref):
    hid = lax.broadcasted_iota(I32, (GRID_W, LANES), 1) // NA_HEAD_DIM

    def scores(r):
        rs = jnp.clip(r - WIN_R // 2, 0, rows - kh)
        q0 = pl.multiple_of(r * GRID_W, GRID_W)
        k0 = pl.multiple_of(rs * GRID_W, GRID_W)
        qr = q_ref[0, pl.ds(q0, GRID_W), :]
        zero = jnp.zeros_like(qr)
        qq = jnp.concatenate(
            [jnp.where(hid == hh, qr, zero) for hh in range(NA_HEADS_PER_BLOCK)], axis=0)
        kb = k_ref[0, pl.ds(k0, kh * GRID_W), :]
        s = lax.dot_general(qq, kb, (((1,), (1,)), ((), ())), preferred_element_type=F32)
        return s + bias_ref[r - rs, 0]

    def weights(s):
        e = jnp.exp2(s - jnp.max(s, axis=-1, keepdims=True))
        return e.astype(BF16), 1.0 / jnp.sum(e, axis=-1, keepdims=True)

    def finish(r, e, rl):
        rs = jnp.clip(r - WIN_R // 2, 0, rows - kh)
        q0 = pl.multiple_of(r * GRID_W, GRID_W)
        k0 = pl.multiple_of(rs * GRID_W, GRID_W)
        vb = v_ref[0, pl.ds(k0, kh * GRID_W), :]
        oo = jnp.dot(e, vb, preferred_element_type=F32) * rl
        o = jnp.zeros((GRID_W, LANES), F32)
        for hh in range(NA_HEADS_PER_BLOCK):
            o = o + jnp.where(hid == hh, oo[hh * GRID_W:(hh + 1) * GRID_W], 0.0)
        o_ref[0, pl.ds(q0, GRID_W), :] = o.astype(BF16)

    def row_group(g, carry):
        rr = [g * NA_ROWS_PER_ITER + i for i in range(NA_ROWS_PER_ITER)]
        ss = [scores(r) for r in rr]
        ws = [weights(s) for s in ss]
        for r, (e, rl) in zip(rr, ws):
            finish(r, e, rl)
        return carry

    lax.fori_loop(0, rows // NA_ROWS_PER_ITER, row_group, 0)


def _natten(qn, kn, vn, bias_tab):
    B, S, W = qn.shape
    rows = S // GRID_W
    kh = min(WIN_R, rows)
    nblk = W // LANES
    blk = lambda g, b: (b, 0, g)
    return pl.pallas_call(
        functools.partial(_natten_kernel, rows, kh),
        grid=(nblk, B),
        in_specs=[
            pl.BlockSpec((1, S, LANES), blk),
            pl.BlockSpec((1, S, LANES), blk),
            pl.BlockSpec((1, S, LANES), blk),
            pl.BlockSpec((kh, 1, NA_HEADS_PER_BLOCK * GRID_W, kh * GRID_W), lambda g, b: (0, g, 0, 0)),
        ],
        out_specs=pl.BlockSpec((1, S, LANES), blk),
        out_shape=jax.ShapeDtypeStruct((B, S, W), BF16),
        compiler_params=_cparams(2),
        name="natten",
    )(qn, kn, vn, bias_tab)


def _mixout_kernel(oa_ref, ob_ref, ga_ref, gb_ref, h_ref, bg_ref, wa_ref, wb_ref, wo_ref,
                   lg_ref, lb_ref, h1_ref, h1p_ref):
    ya = jnp.dot(oa_ref[...], wa_ref[...], preferred_element_type=F32)
    yb = jnp.dot(ob_ref[...], wb_ref[...], preferred_element_type=F32)
    g_a = jax.nn.sigmoid(ga_ref[...].astype(F32) + bg_ref[0:1, :])
    g_b = jax.nn.sigmoid(gb_ref[...].astype(F32) + bg_ref[1:2, :])
    z = (g_a * ya + g_b * yb).astype(BF16)
    m = jnp.dot(z, wo_ref[...], preferred_element_type=F32)
    y = _layer_norm(ALPHA * h_ref[...] + m, lg_ref[...], lb_ref[...])
    h1_ref[...] = y
    h1p_ref[...] = _pack_bf16_pairs(y)


def _mixout(oa, ob, ga, gb, h, b_gate, wa, wb, wo, ln_g, ln_b):
    T, D = h.shape
    tm = min(TM_MIX, T)
    row = lambda i: (i, 0)
    const = lambda i: (0, 0)
    res = lambda a: pl.BlockSpec(a.shape, const, pipeline_mode=pl.Buffered(1))
    return pl.pallas_call(
        _mixout_kernel,
        grid=(T // tm,),
        in_specs=[
            pl.BlockSpec((tm, oa.shape[1]), row),
            pl.BlockSpec((tm, ob.shape[1]), row),
            pl.BlockSpec((tm, D), row),
            pl.BlockSpec((tm, D), row),
            pl.BlockSpec((tm, D), row),
            pl.BlockSpec((2, D), const),
            res(wa), res(wb), res(wo),
            pl.BlockSpec((1, D), const),
            pl.BlockSpec((1, D), const),
        ],
        out_specs=[pl.BlockSpec((tm, D), row), pl.BlockSpec((tm, D // 2), row)],
        out_shape=[jax.ShapeDtypeStruct((T, D), F32), jax.ShapeDtypeStruct((T, D // 2), U32)],
        compiler_params=_cparams(1),
        name="mixout",
    )(oa, ob, ga, gb, h, b_gate, wa, wb, wo, ln_g.reshape(1, D), ln_b.reshape(1, D))


def _router_kernel(h_ref, rwt_ref, rb_ref, tri_ref, eidx_ref, gate_ref, rank_ref, cnt_ref, carry_ref):
    i = pl.program_id(0)
    tt = h_ref.shape[0]
    E, G = N_EXPERTS, N_GROUPS
    P = E // G
    neg = -jnp.inf

    @pl.when(i == 0)
    def _():
        carry_ref[...] = jnp.zeros_like(carry_ref)

    hb = h_ref[...].astype(BF16)
    logits = lax.dot_general(rwt_ref[...], hb, (((1,), (1,)), ((), ())), preferred_element_type=F32)
    scores = jax.nn.sigmoid(logits)
    biased = scores + rb_ref[...]
    b3 = biased.reshape(G, P, tt)
    s3 = scores.reshape(G, P, tt)
    pi = lax.broadcasted_iota(I32, (G, P, tt), 1)
    ei = lax.broadcasted_iota(I32, (G, P, tt), 0) * P + pi

    m1 = jnp.max(b3, axis=1, keepdims=True)
    i1 = jnp.min(jnp.where(b3 == m1, pi, P), axis=1, keepdims=True)
    m2 = jnp.max(jnp.where(pi == i1, neg, b3), axis=1, keepdims=True)
    grp = m1 + m2
    gi = lax.broadcasted_iota(I32, (G, 1, tt), 0)
    gsel = jnp.zeros((G, 1, tt), F32)
    for _ in range(TOPK_GROUPS):
        gm = jnp.max(grp, axis=0, keepdims=True)
        gidx = jnp.min(jnp.where(grp == gm, gi, G), axis=0, keepdims=True)
        hit = gi == gidx
        gsel = jnp.where(hit, 1.0, gsel)
        grp = jnp.where(hit, neg, grp)

    cand = jnp.where(gsel > 0.0, b3, neg)
    sel = jnp.zeros((G, P, tt), F32)
    eids, gates = [], []
    for _ in range(TOP_K):
        mk = jnp.max(cand, axis=(0, 1), keepdims=True)
        ik = jnp.min(jnp.where(cand == mk, ei, E), axis=(0, 1), keepdims=True)
        hit = ei == ik
        gates.append(jnp.sum(jnp.where(hit, s3, 0.0), axis=(0, 1), keepdims=True))
        eids.append(ik)
        cand = jnp.where(hit, neg, cand)
        sel = jnp.where(hit, 1.0, sel)

    gsum = gates[0]
    for gk in gates[1:]:
        gsum = gsum + gk
    gscale = ROUTE_SCALE / gsum

    sel2 = sel.reshape(E, tt)
    prefix = jnp.dot(sel2.astype(BF16), tri_ref[...], preferred_element_type=F32)
    base3 = (prefix + carry_ref[...]).reshape(G, P, tt)
    ranks = [jnp.sum(jnp.where(ei == ik, base3, 0.0), axis=(0, 1), keepdims=True) for ik in eids]
    carry_ref[...] = carry_ref[...] + jnp.sum(sel2, axis=1, keepdims=True)
    cnt_ref[...] = carry_ref[...]

    eidx_ref[...] = jnp.concatenate([x.reshape(1, tt) for x in eids], axis=0)
    gate_ref[...] = jnp.concatenate([(g * gscale).reshape(1, tt) for g in gates], axis=0)
    rank_ref[...] = jnp.concatenate([x.reshape(1, tt) for x in ranks], axis=0).astype(I32)


def _router(h1, router_w, router_bias):
    T, D = h1.shape
    E = N_EXPERTS
    tt = min(TT_ROUTER, T)
    rwt = router_w.T.astype(BF16)
    tri = jnp.asarray(np.triu(np.ones((tt, tt), np.float32), k=1), BF16)
    const = lambda i: (0, 0)
    col = lambda i: (0, i)
    return pl.pallas_call(
        _router_kernel,
        grid=(T // tt,),
        in_specs=[
            pl.BlockSpec((tt, D), lambda i: (i, 0)),
            pl.BlockSpec((E, D), const),
            pl.BlockSpec((E, 1), const),
            pl.BlockSpec((tt, tt), const),
        ],
        out_specs=[
            pl.BlockSpec((TOP_K, tt), col),
            pl.BlockSpec((TOP_K, tt), col),
            pl.BlockSpec((TOP_K, tt), col),
            pl.BlockSpec((E, 1), const),
        ],
        out_shape=[
            jax.ShapeDtypeStruct((TOP_K, T), I32),
            jax.ShapeDtypeStruct((TOP_K, T), F32),
            jax.ShapeDtypeStruct((TOP_K, T), I32),
            jax.ShapeDtypeStruct((E, 1), F32),
        ],
        scratch_shapes=[pltpu.VMEM((E, 1), F32)],
        compiler_params=_cparams(1),
        name="router",
    )(h1, rwt, router_bias.reshape(E, 1).astype(F32), tri)


def _slots_kernel(pstart_ref, eidx_ref, rank_ref, slot_ref):
    eidx = eidx_ref[...]

    def add_expert(e, acc):
        return acc + jnp.where(eidx == e, pstart_ref[e], 0)

    slot_ref[...] = lax.fori_loop(0, N_EXPERTS, add_expert, rank_ref[...], unroll=8)


def _slots(pad_start, eidx, rank):
    K, T = eidx.shape
    tt = min(2048, T)
    col = lambda i, ps: (0, i)
    grid_spec = pltpu.PrefetchScalarGridSpec(
        num_scalar_prefetch=1,
        grid=(T // tt,),
        in_specs=[pl.BlockSpec((K, tt), col), pl.BlockSpec((K, tt), col)],
        out_specs=pl.BlockSpec((K, tt), col),
    )
    return pl.pallas_call(
        _slots_kernel,
        grid_spec=grid_spec,
        out_shape=jax.ShapeDtypeStruct((K, T), I32),
        compiler_params=_cparams(1),
        name="slots",
    )(pad_start, eidx, rank)


def _dispatch(slot, h1p, n_slots):
    T, Wp = h1p.shape
    n_workers = SC_CORES * SC_SUBCORES
    per = T // n_workers
    assert per % SC_IDX_CHUNK == 0, (T, n_workers)
    mesh = plsc.VectorSubcoreMesh(core_axis_name="core", subcore_axis_name="subcore")

    @pl.kernel(
        out_type=jax.ShapeDtypeStruct((n_slots, Wp), U32), mesh=mesh,
        scratch_types=[pltpu.VMEM((TOP_K, SC_IDX_CHUNK), I32),
                       pltpu.VMEM((SC_IDX_CHUNK // SC_ROW_WIN, SC_ROW_WIN, Wp), U32),
                       pltpu.SemaphoreType.DMA((SC_IDX_CHUNK // SC_ROW_WIN,)), pltpu.SemaphoreType.DMA])
    def dispatch(x_hbm, s_hbm, o_hbm, idx_v, x_v, sem_r, sem_w):
        base = (lax.axis_index("core") * SC_SUBCORES + lax.axis_index("subcore")) * per
        n_win = SC_IDX_CHUNK // SC_ROW_WIN

        @pl.loop(0, per // SC_IDX_CHUNK)
        def _(c):
            t0 = base + c * SC_IDX_CHUNK
            reads = [pltpu.async_copy(x_hbm.at[pl.ds(t0 + j * SC_ROW_WIN, SC_ROW_WIN)], x_v.at[j], sem_r.at[j])
                     for j in range(n_win)]
            pltpu.sync_copy(s_hbm.at[:, pl.ds(t0, SC_IDX_CHUNK)], idx_v)
            copies = []
            for j in range(n_win):
                reads[j].wait()
                copies += [
                    pltpu.async_copy(x_v.at[j], o_hbm.at[idx_v.at[k, pl.ds(j * SC_ROW_WIN, SC_ROW_WIN)]], sem_w)
                    for k in range(TOP_K)]
            for cp in copies:
                cp.wait()

    return dispatch(h1p, slot)


def _experts_kernel(layer, be_ref, nv_ref, par_ref, nxt_ref, xs_hbm, wg_hbm, wu_hbm, wd_hbm, ys_hbm,
                    xbuf, ybuf, wg_f, wu_f, wd_f, wg_s, wu_s, wd_s, sem_w, sem_x, sem_y):
    nv = nv_ref[0]
    blk = xbuf.shape[1]
    sub = min(SUB_MOE, blk)
    n_sub = blk // sub

    def rows(i):
        return pl.ds(pl.multiple_of(i * blk, blk), blk)

    def x_copy(i, slot):
        return pltpu.make_async_copy(xs_hbm.at[rows(i)], xbuf.at[slot], sem_x.at[slot])

    def y_copy(i, slot):
        return pltpu.make_async_copy(ybuf.at[slot], ys_hbm.at[rows(i)], sem_y.at[slot])

    def weight_copies(e, slot):
        return [pltpu.make_async_copy(src.at[layer, e], dst.at[slot], sem_w.at[slot])
                for src, dst in ((wg_hbm, wg_f), (wu_hbm, wu_f), (wd_hbm, wd_f))]

    n_xbuf = xbuf.shape[0]
    for a in range(n_xbuf - 1):
        @pl.when(a < nv)
        def _():
            x_copy(a, a).start()
    for cp in weight_copies(be_ref[0], par_ref[0]):
        cp.start()

    def block(i, carry):
        slot = lax.rem(i, 2)
        xslot = lax.rem(i, n_xbuf)
        e = be_ref[i]
        wslot = par_ref[i]

        @pl.when(i + (n_xbuf - 1) < nv)
        def _():
            x_copy(i + (n_xbuf - 1), lax.rem(i + (n_xbuf - 1), n_xbuf)).start()

        @pl.when((i == 0) | (e != be_ref[jnp.maximum(i - 1, 0)]))
        def _():
            for cp in weight_copies(e, wslot):
                cp.wait()
            wg_s[...] = wg_f[wslot].astype(BF16)
            wu_s[...] = wu_f[wslot].astype(BF16)
            wd_s[...] = wd_f[wslot].astype(BF16)

            @pl.when(nxt_ref[i] >= 0)
            def _():
                for cp in weight_copies(nxt_ref[i], 1 - wslot):
                    cp.start()

        x_copy(i, xslot).wait()

        @pl.when(i >= 2)
        def _():
            y_copy(i - 2, slot).wait()

        def up(j):
            lo, hi = _unpack_bf16_pairs(xbuf[xslot, j * sub:(j + 1) * sub, :])
            x = jnp.concatenate([lo.astype(BF16), hi.astype(BF16)], axis=1)
            return (jnp.dot(x, wg_s[...], preferred_element_type=F32),
                    jnp.dot(x, wu_s[...], preferred_element_type=F32))

        def down(j, g, u):
            hmid = (g * jax.nn.sigmoid(g) * u).astype(BF16)
            y = jnp.dot(hmid, wd_s[...], preferred_element_type=F32)
            ybuf[slot, j * sub:(j + 1) * sub, :] = _pack_bf16_pairs(y)

        pending = up(0)
        for j in range(n_sub):
            nxt = up(j + 1) if j + 1 < n_sub else None
            down(j, *pending)
            pending = nxt

        y_copy(i, slot).start()
        return carry

    lax.fori_loop(0, nv, block, 0)

    @pl.when(nv >= 2)
    def _():
        y_copy(nv - 2, lax.rem(nv, 2)).wait()

    y_copy(nv - 1, lax.rem(nv - 1, 2)).wait()


def _experts(layer, blk_exp, n_valid, blk_par, blk_next, xs, wg, wu, wd):
    n_slots, Wp = xs.shape
    _, E, D, F = wg.shape

    hbm = pl.BlockSpec(memory_space=pl.ANY)
    grid_spec = pltpu.PrefetchScalarGridSpec(
        num_scalar_prefetch=4,
        grid=(1,),
        in_specs=[hbm, hbm, hbm, hbm],
        out_specs=hbm,
        scratch_shapes=[
            pltpu.VMEM((3, BLK_MOE, Wp), U32), pltpu.VMEM((2, BLK_MOE, Wp), U32),
            pltpu.VMEM((2, D, F), F32), pltpu.VMEM((2, D, F), F32), pltpu.VMEM((2, F, D), F32),
            pltpu.VMEM((D, F), BF16), pltpu.VMEM((D, F), BF16), pltpu.VMEM((F, D), BF16),
            pltpu.SemaphoreType.DMA((2,)), pltpu.SemaphoreType.DMA((3,)), pltpu.SemaphoreType.DMA((2,)),
        ],
    )
    return pl.pallas_call(
        functools.partial(_experts_kernel, layer),
        grid_spec=grid_spec,
        out_shape=jax.ShapeDtypeStruct((n_slots, Wp), U32),
        compiler_params=_cparams(1),
        name="experts",
    )(blk_exp, n_valid, blk_par, blk_next, xs, wg, wu, wd)


def _gather_rows(slot, ys):
    K, T = slot.shape
    Wp = ys.shape[1]
    n_workers = SC_CORES * SC_SUBCORES
    per = T // n_workers
    assert per % SC_IDX_CHUNK == 0 and K % SC_GATHER_PLANES == 0, (T, K)
    win = SC_ROW_WIN // 4
    mesh = plsc.VectorSubcoreMesh(core_axis_name="core", subcore_axis_name="subcore")
    groups = [(j, k0) for j in range(SC_IDX_CHUNK // win) for k0 in range(0, K, SC_GATHER_PLANES)]

    @pl.kernel(
        out_type=jax.ShapeDtypeStruct((K, T, Wp), U32), mesh=mesh,
        scratch_types=[pltpu.VMEM((K, SC_IDX_CHUNK), I32),
                       pltpu.VMEM((2, SC_GATHER_PLANES, win, Wp), U32),
                       pltpu.SemaphoreType.DMA((2,)), pltpu.SemaphoreType.DMA((2,))])
    def gather(y_hbm, s_hbm, g_hbm, idx_v, buf, sem_g, sem_w):
        base = (lax.axis_index("core") * SC_SUBCORES + lax.axis_index("subcore")) * per

        @pl.loop(0, per // SC_IDX_CHUNK)
        def _(c):
            t0 = base + c * SC_IDX_CHUNK
            pltpu.sync_copy(s_hbm.at[:, pl.ds(t0, SC_IDX_CHUNK)], idx_v)

            def start_reads(g):
                j, k0 = groups[g]
                return [pltpu.async_copy(y_hbm.at[idx_v.at[k0 + i, pl.ds(j * win, win)]],
                                         buf.at[g % 2, i], sem_g.at[g % 2])
                        for i in range(SC_GATHER_PLANES)]

            def start_writes(g):
                j, k0 = groups[g]
                return [pltpu.async_copy(buf.at[g % 2, i], g_hbm.at[k0 + i, pl.ds(t0 + j * win, win)],
                                         sem_w.at[g % 2])
                        for i in range(SC_GATHER_PLANES)]

            reads = start_reads(0)
            writes_prev = []
            for g in range(len(groups)):
                for cp in reads:
                    cp.wait()
                writes = start_writes(g)
                for cp in writes_prev:
                    cp.wait()
                if g + 1 < len(groups):
                    reads = start_reads(g + 1)
                writes_prev = writes
            for cp in writes_prev:
                cp.wait()

    return gather(ys, slot)


def _combine_kernel(yg_ref, gate_ref, h1_ref, swg_ref, swu_ref, swd_ref, lg_ref, lb_ref, *rest):
    out_ref = rest[-1]
    h1 = h1_ref[...]
    xb = h1.astype(BF16)
    g = jnp.dot(xb, swg_ref[...], preferred_element_type=F32)
    u = jnp.dot(xb, swu_ref[...], preferred_element_type=F32)
    hmid = (g * jax.nn.sigmoid(g) * u).astype(BF16)
    shared = jnp.dot(hmid, swd_ref[...], preferred_element_type=F32)

    gates = gate_ref[...]
    tc, wp = yg_ref.shape[1], yg_ref.shape[2]
    f_lo = jnp.zeros((tc, wp), F32)
    f_hi = jnp.zeros((tc, wp), F32)
    for k in range(TOP_K):
        lo, hi = _unpack_bf16_pairs(yg_ref[k])
        gk = gates[:, k:k + 1]
        f_lo = f_lo + gk * lo
        f_hi = f_hi + gk * hi
    routed = jnp.concatenate([f_lo, f_hi], axis=1)
    out_ref[...] = _layer_norm(ALPHA * h1 + (routed + shared), lg_ref[...], lb_ref[...])


def _combine(yg, gate_t, h1, swg, swu, swd, ln_g, ln_b, part, prev_out):
    T, D = h1.shape
    K, Tp, Wp = yg.shape
    tc = min(TD_MOE, Tp)
    off = part * (Tp // tc)
    row = lambda i: (i + off, 0)
    const = lambda i: (0, 0)
    res = lambda a: pl.BlockSpec(a.shape, const)
    in_specs = [
        pl.BlockSpec((K, tc, Wp), lambda i: (0, i, 0)),
        pl.BlockSpec((tc, TOP_K), row),
        pl.BlockSpec((tc, D), row),
        res(swg), res(swu), res(swd),
        pl.BlockSpec((1, D), const),
        pl.BlockSpec((1, D), const),
    ]
    args = [yg, gate_t, h1, swg, swu, swd, ln_g.reshape(1, D), ln_b.reshape(1, D)]
    aliases = {}
    if prev_out is not None:
        in_specs.append(pl.BlockSpec(memory_space=pl.ANY))
        args.append(prev_out)
        aliases = {len(args) - 1: 0}
    return pl.pallas_call(
        _combine_kernel,
        grid=(Tp // tc,),
        in_specs=in_specs,
        out_specs=pl.BlockSpec((tc, D), row),
        out_shape=jax.ShapeDtypeStruct((T, D), F32),
        input_output_aliases=aliases,
        compiler_params=_cparams(1),
        name="combine",
    )(*args)


def _moe_layer(layer, h1, h1p, router_w, router_bias, wg, wu, wd, swg, swu, swd, ln_g, ln_b):
    T, D = h1.shape
    E = N_EXPERTS
    M = T * TOP_K
    eidx, gate, rank, cnt = _router(h1, router_w, router_bias)

    counts = cnt[:, 0].astype(I32)
    padded = (counts + BLK_MOE - 1) // BLK_MOE * BLK_MOE
    pad_end = jnp.cumsum(padded)
    pad_start = pad_end - padded
    nb = -(-(M + E * (BLK_MOE - 1)) // BLK_MOE)
    n_slots = nb * BLK_MOE
    slot = _slots(pad_start.astype(I32), eidx, rank)
    n_valid = (pad_end[-1] // BLK_MOE).astype(I32).reshape(1)
    blk_start = jnp.arange(nb, dtype=I32) * BLK_MOE
    blk_exp = jnp.minimum(jnp.sum((pad_end[None, :] <= blk_start[:, None]).astype(I32), axis=1), E - 1)

    eids = jnp.arange(E, dtype=I32)
    nonempty = counts > 0
    order = jnp.cumsum(nonempty.astype(I32)) - 1
    nxt_e = lax.cummin(jnp.where(nonempty, eids, E), reverse=True)
    nxt_e = jnp.concatenate([nxt_e[1:], jnp.full((1,), E, I32)])
    blk_par = (order % 2)[blk_exp].astype(I32)
    blk_next = jnp.where(nxt_e < E, nxt_e, -1)[blk_exp].astype(I32)

    xs = _dispatch(slot, h1p, n_slots)
    ys = _experts(layer, blk_exp, n_valid, blk_par, blk_next, xs, wg, wu, wd)
    n_parts = MOE_COMBINE_PARTS if T % (MOE_COMBINE_PARTS * SC_CORES * SC_SUBCORES * SC_IDX_CHUNK) == 0 else 1
    tp = T // n_parts
    gate_t = gate.T
    shared_w = (swg.astype(BF16), swu.astype(BF16), swd.astype(BF16))
    parts = [_gather_rows(slot[:, p * tp:(p + 1) * tp], ys) for p in range(n_parts)]
    out = None
    for p in range(n_parts):
        out = _combine(parts[p], gate_t, h1, *shared_w, ln_g, ln_b, p, out)
    return out


def kernel(x, emb_ln_g, emb_ln_b, w_in, b_gate, lam_q1, lam_k1, lam_q2, lam_k2, subln_g, w_proj_a, na_rpb, w_proj_b, w_out, ln1_g, ln1_b, router_w, router_bias, exp_w_gate, exp_w_up, exp_w_down, sh_w_gate, sh_w_up, sh_w_down, ln2_g, ln2_b):
    B, S, D = x.shape
    T = B * S
    rows = S // GRID_W
    h = x.reshape(T, D)
    for l in range(DEPTH):
        lam_init = 0.8 - 0.6 * math.exp(-0.3 * l)
        outs = _inproj(h, emb_ln_g, emb_ln_b, w_in[l].astype(BF16), apply_ln=(l == 0))
        if l == 0:
            h, outs = outs[0], outs[1:]
        qa, ka, va, qn, kn, vn, ga, gb = outs
        lamv = jnp.stack([lam_q1[l], lam_k1[l], lam_q2[l], lam_k2[l]]).astype(F32)
        oa = _diffattn(qa.reshape(B, S, -1), ka.reshape(B, S, -1), va.reshape(B, S, -1),
                       lamv, subln_g[l], lam_init)
        ob = _natten(qn.reshape(B, S, -1), kn.reshape(B, S, -1), vn.reshape(B, S, -1),
                     _na_bias_table(na_rpb[l], rows))
        h1, h1p = _mixout(oa.reshape(T, -1), ob.reshape(T, -1), ga, gb, h, b_gate[l],
                          w_proj_a[l].astype(BF16), w_proj_b[l].astype(BF16), w_out[l].astype(BF16),
                          ln1_g[l], ln1_b[l])
        h = _moe_layer(l, h1, h1p, router_w[l], router_bias[l], exp_w_gate, exp_w_up, exp_w_down,
                       sh_w_gate[l], sh_w_up[l], sh_w_down[l], ln2_g[l], ln2_b[l])
    return h.reshape(B, S, D)
```

```python
import functools
import math

import numpy as np
import jax
import jax.numpy as jnp
from jax import lax
from jax.experimental import pallas as pl
from jax.experimental.pallas import tpu as pltpu
from jax.experimental.pallas import tpu_sc as plsc

F32 = jnp.float32
BF16 = jnp.bfloat16
U32 = jnp.uint32
I32 = jnp.int32

DA_HEADS = 8
DA_HEAD_DIM = 64
NA_HEADS = 16
NA_HEAD_DIM = 32
GRID_W = 64
WIN_R = 8
WIN_C = 16
N_EXPERTS = 256
TOP_K = 8
N_GROUPS = 8
TOPK_GROUPS = 4
ROUTE_SCALE = 2.5
DEPTH = 2
ALPHA = (2 * DEPTH) ** 0.25
LN_EPS = 1e-5
LOG2E = 1.4426950408889634

LANES = 128
NA_HEADS_PER_BLOCK = LANES // NA_HEAD_DIM
VMEM_LIMIT = 56 * 1024 * 1024
SC_CORES = 2
SC_SUBCORES = 16
SC_IDX_CHUNK = 128
SC_ROW_WIN = 64
SC_GATHER_PLANES = 4

TM_PROJ = 512
TM_MIX = 1024
TQ_DA = 2048
TQ_SUB_DA = 128
NA_ROWS_PER_ITER = 4
TT_ROUTER = 1024
TD_MOE = 512
BLK_MOE = 512
SUB_MOE = 256
MOE_COMBINE_PARTS = 8
NEG_BIG = -1e30


def _cparams(n_axes):
    return pltpu.CompilerParams(
        dimension_semantics=("arbitrary",) * n_axes, vmem_limit_bytes=VMEM_LIMIT)


def _layer_norm(x, g, b):
    mu = jnp.mean(x, axis=-1, keepdims=True)
    xc = x - mu
    var = jnp.mean(xc * xc, axis=-1, keepdims=True)
    return xc * lax.rsqrt(var + LN_EPS) * g + b


def _pack_bf16_pairs(y):
    w = y.shape[1] // 2
    lo = lax.bitcast_convert_type(y[:, :w].astype(BF16).astype(F32), U32)
    hi = lax.bitcast_convert_type(y[:, w:].astype(BF16).astype(F32), U32)
    return (hi & jnp.uint32(0xFFFF0000)) | (lo >> 16)


def _unpack_bf16_pairs(u):
    lo = lax.bitcast_convert_type(u << 16, F32)
    hi = lax.bitcast_convert_type(u & jnp.uint32(0xFFFF0000), F32)
    return lo, hi


def _inproj_kernel(apply_ln, seg_widths, seg_scales, x_ref, g_ref, b_ref, w_ref, *out_refs):
    x = x_ref[...]
    if apply_ln:
        x = _layer_norm(x, g_ref[...], b_ref[...])
        out_refs[0][...] = x
        out_refs = out_refs[1:]
    xb = x.astype(BF16)
    off = 0
    for ref, width, scale in zip(out_refs, seg_widths, seg_scales):
        y = jnp.dot(xb, w_ref[:, off:off + width], preferred_element_type=F32)
        if scale != 1.0:
            y = y * scale
        ref[...] = y.astype(BF16)
        off += width


def _inproj(x, ln_g, ln_b, w_in_bf16, apply_ln):
    T, D = x.shape
    da_w = DA_HEADS * 2 * DA_HEAD_DIM
    na_w = NA_HEADS * NA_HEAD_DIM
    seg_widths = (da_w, da_w, da_w, na_w, na_w, na_w, D, D)
    seg_scales = (DA_HEAD_DIM ** -0.5 * LOG2E, 1.0, 1.0, NA_HEAD_DIM ** -0.5 * LOG2E, 1.0, 1.0, 1.0, 1.0)
    tm = min(TM_PROJ, T)
    n_cols = w_in_bf16.shape[1]
    row = lambda i: (i, 0)
    const = lambda i: (0, 0)
    out_shape = [jax.ShapeDtypeStruct((T, w), BF16) for w in seg_widths]
    out_specs = [pl.BlockSpec((tm, w), row) for w in seg_widths]
    if apply_ln:
        out_shape = [jax.ShapeDtypeStruct((T, D), F32)] + out_shape
        out_specs = [pl.BlockSpec((tm, D), row)] + out_specs
    return pl.pallas_call(
        functools.partial(_inproj_kernel, apply_ln, seg_widths, seg_scales),
        grid=(T // tm,),
        in_specs=[
            pl.BlockSpec((tm, D), row),
            pl.BlockSpec((1, D), const),
            pl.BlockSpec((1, D), const),
            pl.BlockSpec((D, n_cols), const, pipeline_mode=pl.Buffered(1)),
        ],
        out_specs=out_specs,
        out_shape=out_shape,
        compiler_params=_cparams(1),
        name="inproj",
    )(x, ln_g.reshape(1, D), ln_b.reshape(1, D), w_in_bf16)


def _diffattn_kernel(lam_init, slopes_ref, q_ref, k_ref, v_ref, lamv_ref, g_ref, o_ref, bias_ref):
    h = pl.program_id(0)
    qi = pl.program_id(1)
    b = pl.program_id(2)
    tq = q_ref.shape[1]
    S = k_ref.shape[1]
    d = DA_HEAD_DIM

    @pl.when(b == 0)
    def _():
        qpos = qi * tq + lax.broadcasted_iota(I32, (tq, S), 0)
        kpos = lax.broadcasted_iota(I32, (tq, S), 1)
        bias_ref[...] = jnp.abs(qpos - kpos).astype(F32) * (-slopes_ref[h])

    lv = lamv_ref[...]
    lam = (jnp.exp(jnp.sum(lv[0:1] * lv[1:2], axis=-1, keepdims=True))
           - jnp.exp(jnp.sum(lv[2:3] * lv[3:4], axis=-1, keepdims=True)) + lam_init)

    k = k_ref[0]
    v1 = jnp.concatenate([v_ref[0], jnp.ones((S, 2 * d), BF16)], axis=1)
    gain = g_ref[...] * (1.0 - lam_init)
    nt = (((1,), (1,)), ((), ()))
    ts = min(TQ_SUB_DA, tq)
    lane = lax.broadcasted_iota(I32, (ts, 2 * d), 1)

    def scores(j):
        q = q_ref[0, j * ts:(j + 1) * ts, :]
        nb = bias_ref[j * ts:(j + 1) * ts, :]
        zero = jnp.zeros_like(q)
        s1 = lax.dot_general(jnp.where(lane < d, q, zero), k, nt, preferred_element_type=F32) + nb
        s2 = lax.dot_general(jnp.where(lane >= d, q, zero), k, nt, preferred_element_type=F32) + nb
        return s1, s2

    def weights(s1, s2):
        return (jnp.exp2((s1 - jnp.max(s1, axis=-1, keepdims=True)).astype(BF16)),
                jnp.exp2((s2 - jnp.max(s2, axis=-1, keepdims=True)).astype(BF16)))

    def finish(j, e1, e2):
        p = jnp.dot(jnp.concatenate([e1, e2], axis=0), v1, preferred_element_type=F32)
        p1, p2 = p[:ts], p[ts:]
        o = p1[:, :2 * d] / p1[:, 2 * d:] - lam * (p2[:, :2 * d] / p2[:, 2 * d:])
        ms = jnp.mean(o * o, axis=-1, keepdims=True)
        o = o * lax.rsqrt(ms + LN_EPS) * gain
        o_ref[0, j * ts:(j + 1) * ts, :] = o.astype(BF16)

    n_sub = tq // ts
    s_next = scores(0)
    e_cur = weights(*s_next)
    s_next = scores(1) if n_sub > 1 else None
    for j in range(n_sub):
        s_after = scores(j + 2) if j + 2 < n_sub else None
        finish(j, *e_cur)
        if s_next is not None:
            e_cur = weights(*s_next)
        s_next = s_after


def _diffattn(qa, ka, va, lamv, subln_g, lam_init):
    B, S, W = qa.shape
    hw = 2 * DA_HEAD_DIM
    tq = min(TQ_DA, S)
    slopes = jnp.asarray(2.0 ** (-8.0 * np.arange(1, DA_HEADS + 1) / DA_HEADS) * LOG2E, F32)
    grid_spec = pltpu.PrefetchScalarGridSpec(
        num_scalar_prefetch=1,
        grid=(DA_HEADS, S // tq, B),
        in_specs=[
            pl.BlockSpec((1, tq, hw), lambda h, qi, b, sl: (b, qi, h)),
            pl.BlockSpec((1, S, hw), lambda h, qi, b, sl: (b, 0, h)),
            pl.BlockSpec((1, S, hw), lambda h, qi, b, sl: (b, 0, h)),
            pl.BlockSpec((4, DA_HEAD_DIM), lambda h, qi, b, sl: (0, 0)),
            pl.BlockSpec((1, hw), lambda h, qi, b, sl: (0, 0)),
        ],
        out_specs=pl.BlockSpec((1, tq, hw), lambda h, qi, b, sl: (b, qi, h)),
        scratch_shapes=[pltpu.VMEM((tq, S), F32)],
    )
    return pl.pallas_call(
        functools.partial(_diffattn_kernel, lam_init),
        grid_spec=grid_spec,
        out_shape=jax.ShapeDtypeStruct((B, S, W), BF16),
        compiler_params=_cparams(3),
        name="diffattn",
    )(slopes, qa, ka, va, lamv, subln_g.reshape(1, hw))


def _na_bias_table(rpb, rows):
    kh = min(WIN_R, rows)
    qcol = np.arange(GRID_W)
    kcol = np.arange(GRID_W)
    cs = np.clip(qcol - WIN_C // 2, 0, GRID_W - WIN_C)
    col_mask = (kcol[None, :] >= cs[:, None]) & (kcol[None, :] < cs[:, None] + WIN_C)
    col_off = np.clip(kcol[None, :] - qcol[:, None] + WIN_C - 1, 0, 2 * WIN_C - 2)
    v = np.arange(kh)
    j = np.arange(kh)
    row_idx = np.clip(j[None, :] - v[:, None] + WIN_R - 1, 0, 2 * WIN_R - 2)
    sel_r = np.zeros((2 * WIN_R - 1, kh * kh), np.float32)
    sel_r[row_idx.reshape(-1), np.arange(kh * kh)] = 1.0
    sel_c = np.zeros((2 * WIN_C - 1, GRID_W * GRID_W), np.float32)
    sel_c[col_off.reshape(-1), np.arange(GRID_W * GRID_W)] = 1.0
    hi = lax.Precision.HIGHEST
    t = jnp.einsum("hrc,rp->hpc", rpb * LOG2E, jnp.asarray(sel_r), precision=hi)
    t = jnp.einsum("hpc,cq->hpq", t, jnp.asarray(sel_c), precision=hi)
    t = t.reshape(NA_HEADS, kh, kh, GRID_W, GRID_W)
    t = jnp.where(jnp.asarray(col_mask)[None, None, None], t, NEG_BIG)
    t = t.transpose(1, 0, 3, 2, 4)
    nblk = NA_HEADS // NA_HEADS_PER_BLOCK
    return t.reshape(kh, nblk, NA_HEADS_PER_BLOCK * GRID_W, kh * GRID_W)


def _natten_kernel(rows, kh, q_ref, k_ref, v_ref, bias_ref, o_ref):
    hid = lax.broadcasted_iota(I32, (GRID_W, LANES), 1) // NA_HEAD_DIM

    def scores(r):
        rs = jnp.clip(r - WIN_R // 2, 0, rows - kh)
        q0 = pl.multiple_of(r * GRID_W, GRID_W)
        k0 = pl.multiple_of(rs * GRID_W, GRID_W)
        qr = q_ref[0, pl.ds(q0, GRID_W), :]
        zero = jnp.zeros_like(qr)
        qq = jnp.concatenate(
            [jnp.where(hid == hh, qr, zero) for hh in range(NA_HEADS_PER_BLOCK)], axis=0)
        kb = k_ref[0, pl.ds(k0, kh * GRID_W), :]
        s = lax.dot_general(qq, kb, (((1,), (1,)), ((), ())), preferred_element_type=F32)
        return s + bias_ref[r - rs, 0]

    def weights(s):
        e = jnp.exp2(s - jnp.max(s, axis=-1, keepdims=True))
        return e.astype(BF16), 1.0 / jnp.sum(e, axis=-1, keepdims=True)

    def finish(r, e, rl):
        rs = jnp.clip(r - WIN_R // 2, 0, rows - kh)
        q0 = pl.multiple_of(r * GRID_W, GRID_W)
        k0 = pl.multiple_of(rs * GRID_W, GRID_W)
        vb = v_ref[0, pl.ds(k0, kh * GRID_W), :]
        oo = jnp.dot(e, vb, preferred_element_type=F32) * rl
        o = jnp.zeros((GRID_W, LANES), F32)
        for hh in range(NA_HEADS_PER_BLOCK):
            o = o + jnp.where(hid == hh, oo[hh * GRID_W:(hh + 1) * GRID_W], 0.0)
        o_ref[0, pl.ds(q0, GRID_W), :] = o.astype(BF16)

    def row_group(g, carry):
        rr = [g * NA_ROWS_PER_ITER + i for i in range(NA_ROWS_PER_ITER)]
        ss = [scores(r) for r in rr]
        ws = [weights(s) for s in ss]
        for r, (e, rl) in zip(rr, ws):
            finish(r, e, rl)
        return carry

    lax.fori_loop(0, rows // NA_ROWS_PER_ITER, row_group, 0)


def _natten(qn, kn, vn, bias_tab):
    B, S, W = qn.shape
    rows = S // GRID_W
    kh = min(WIN_R, rows)
    nblk = W // LANES
    blk = lambda g, b: (b, 0, g)
    return pl.pallas_call(
        functools.partial(_natten_kernel, rows, kh),
        grid=(nblk, B),
        in_specs=[
            pl.BlockSpec((1, S, LANES), blk),
            pl.BlockSpec((1, S, LANES), blk),
            pl.BlockSpec((1, S, LANES), blk),
            pl.BlockSpec((kh, 1, NA_HEADS_PER_BLOCK * GRID_W, kh * GRID_W), lambda g, b: (0, g, 0, 0)),
        ],
        out_specs=pl.BlockSpec((1, S, LANES), blk),
        out_shape=jax.ShapeDtypeStruct((B, S, W), BF16),
        compiler_params=_cparams(2),
        name="natten",
    )(qn, kn, vn, bias_tab)


def _mixout_kernel(oa_ref, ob_ref, ga_ref, gb_ref, h_ref, bg_ref, wa_ref, wb_ref, wo_ref,
                   lg_ref, lb_ref, h1_ref, h1p_ref):
    ya = jnp.dot(oa_ref[...], wa_ref[...], preferred_element_type=F32)
    yb = jnp.dot(ob_ref[...], wb_ref[...], preferred_element_type=F32)
    g_a = jax.nn.sigmoid(ga_ref[...].astype(F32) + bg_ref[0:1, :])
    g_b = jax.nn.sigmoid(gb_ref[...].astype(F32) + bg_ref[1:2, :])
    z = (g_a * ya + g_b * yb).astype(BF16)
    m = jnp.dot(z, wo_ref[...], preferred_element_type=F32)
    y = _layer_norm(ALPHA * h_ref[...] + m, lg_ref[...], lb_ref[...])
    h1_ref[...] = y
    h1p_ref[...] = _pack_bf16_pairs(y)


def _mixout(oa, ob, ga, gb, h, b_gate, wa, wb, wo, ln_g, ln_b):
    T, D = h.shape
    tm = min(TM_MIX, T)
    row = lambda i: (i, 0)
    const = lambda i: (0, 0)
    res = lambda a: pl.BlockSpec(a.shape, const, pipeline_mode=pl.Buffered(1))
    return pl.pallas_call(
        _mixout_kernel,
        grid=(T // tm,),
        in_specs=[
            pl.BlockSpec((tm, oa.shape[1]), row),
            pl.BlockSpec((tm, ob.shape[1]), row),
            pl.BlockSpec((tm, D), row),
            pl.BlockSpec((tm, D), row),
            pl.BlockSpec((tm, D), row),
            pl.BlockSpec((2, D), const),
            res(wa), res(wb), res(wo),
            pl.BlockSpec((1, D), const),
            pl.BlockSpec((1, D), const),
        ],
        out_specs=[pl.BlockSpec((tm, D), row), pl.BlockSpec((tm, D // 2), row)],
        out_shape=[jax.ShapeDtypeStruct((T, D), F32), jax.ShapeDtypeStruct((T, D // 2), U32)],
        compiler_params=_cparams(1),
        name="mixout",
    )(oa, ob, ga, gb, h, b_gate, wa, wb, wo, ln_g.reshape(1, D), ln_b.reshape(1, D))


def _router_kernel(h_ref, rwt_ref, rb_ref, tri_ref, eidx_ref, gate_ref, rank_ref, cnt_ref, carry_ref):
    i = pl.program_id(0)
    tt = h_ref.shape[0]
    E, G = N_EXPERTS, N_GROUPS
    P = E // G
    neg = -jnp.inf

    @pl.when(i == 0)
    def _():
        carry_ref[...] = jnp.zeros_like(carry_ref)

    hb = h_ref[...].astype(BF16)
    logits = lax.dot_general(rwt_ref[...], hb, (((1,), (1,)), ((), ())), preferred_element_type=F32)
    scores = jax.nn.sigmoid(logits)
    biased = scores + rb_ref[...]
    b3 = biased.reshape(G, P, tt)
    s3 = scores.reshape(G, P, tt)
    pi = lax.broadcasted_iota(I32, (G, P, tt), 1)
    ei = lax.broadcasted_iota(I32, (G, P, tt), 0) * P + pi

    m1 = jnp.max(b3, axis=1, keepdims=True)
    i1 = jnp.min(jnp.where(b3 == m1, pi, P), axis=1, keepdims=True)
    m2 = jnp.max(jnp.where(pi == i1, neg, b3), axis=1, keepdims=True)
    grp = m1 + m2
    gi = lax.broadcasted_iota(I32, (G, 1, tt), 0)
    gsel = jnp.zeros((G, 1, tt), F32)
    for _ in range(TOPK_GROUPS):
        gm = jnp.max(grp, axis=0, keepdims=True)
        gidx = jnp.min(jnp.where(grp == gm, gi, G), axis=0, keepdims=True)
        hit = gi == gidx
        gsel = jnp.where(hit, 1.0, gsel)
        grp = jnp.where(hit, neg, grp)

    cand = jnp.where(gsel > 0.0, b3, neg)
    sel = jnp.zeros((G, P, tt), F32)
    eids, gates = [], []
    for _ in range(TOP_K):
        mk = jnp.max(cand, axis=(0, 1), keepdims=True)
        ik = jnp.min(jnp.where(cand == mk, ei, E), axis=(0, 1), keepdims=True)
        hit = ei == ik
        gates.append(jnp.sum(jnp.where(hit, s3, 0.0), axis=(0, 1), keepdims=True))
        eids.append(ik)
        cand = jnp.where(hit, neg, cand)
        sel = jnp.where(hit, 1.0, sel)

    gsum = gates[0]
    for gk in gates[1:]:
        gsum = gsum + gk
    gscale = ROUTE_SCALE / gsum

    sel2 = sel.reshape(E, tt)
    prefix = jnp.dot(sel2.astype(BF16), tri_ref[...], preferred_element_type=F32)
    base3 = (prefix + carry_ref[...]).reshape(G, P, tt)
    ranks = [jnp.sum(jnp.where(ei == ik, base3, 0.0), axis=(0, 1), keepdims=True) for ik in eids]
    carry_ref[...] = carry_ref[...] + jnp.sum(sel2, axis=1, keepdims=True)
    cnt_ref[...] = carry_ref[...]

    eidx_ref[...] = jnp.concatenate([x.reshape(1, tt) for x in eids], axis=0)
    gate_ref[...] = jnp.concatenate([(g * gscale).reshape(1, tt) for g in gates], axis=0)
    rank_ref[...] = jnp.concatenate([x.reshape(1, tt) for x in ranks], axis=0).astype(I32)


def _router(h1, router_w, router_bias):
    T, D = h1.shape
    E = N_EXPERTS
    tt = min(TT_ROUTER, T)
    rwt = router_w.T.astype(BF16)
    tri = jnp.asarray(np.triu(np.ones((tt, tt), np.float32), k=1), BF16)
    const = lambda i: (0, 0)
    col = lambda i: (0, i)
    return pl.pallas_call(
        _router_kernel,
        grid=(T // tt,),
        in_specs=[
            pl.BlockSpec((tt, D), lambda i: (i, 0)),
            pl.BlockSpec((E, D), const),
            pl.BlockSpec((E, 1), const),
            pl.BlockSpec((tt, tt), const),
        ],
        out_specs=[
            pl.BlockSpec((TOP_K, tt), col),
            pl.BlockSpec((TOP_K, tt), col),
            pl.BlockSpec((TOP_K, tt), col),
            pl.BlockSpec((E, 1), const),
        ],
        out_shape=[
            jax.ShapeDtypeStruct((TOP_K, T), I32),
            jax.ShapeDtypeStruct((TOP_K, T), F32),
            jax.ShapeDtypeStruct((TOP_K, T), I32),
            jax.ShapeDtypeStruct((E, 1), F32),
        ],
        scratch_shapes=[pltpu.VMEM((E, 1), F32)],
        compiler_params=_cparams(1),
        name="router",
    )(h1, rwt, router_bias.reshape(E, 1).astype(F32), tri)


def _slots_kernel(pstart_ref, eidx_ref, rank_ref, slot_ref):
    eidx = eidx_ref[...]

    def add_expert(e, acc):
        return acc + jnp.where(eidx == e, pstart_ref[e], 0)

    slot_ref[...] = lax.fori_loop(0, N_EXPERTS, add_expert, rank_ref[...], unroll=8)


def _slots(pad_start, eidx, rank):
    K, T = eidx.shape
    tt = min(2048, T)
    col = lambda i, ps: (0, i)
    grid_spec = pltpu.PrefetchScalarGridSpec(
        num_scalar_prefetch=1,
        grid=(T // tt,),
        in_specs=[pl.BlockSpec((K, tt), col), pl.BlockSpec((K, tt), col)],
        out_specs=pl.BlockSpec((K, tt), col),
    )
    return pl.pallas_call(
        _slots_kernel,
        grid_spec=grid_spec,
        out_shape=jax.ShapeDtypeStruct((K, T), I32),
        compiler_params=_cparams(1),
        name="slots",
    )(pad_start, eidx, rank)


def _dispatch(slot, h1p, n_slots):
    T, Wp = h1p.shape
    n_workers = SC_CORES * SC_SUBCORES
    per = T // n_workers
    assert per % SC_IDX_CHUNK == 0, (T, n_workers)
    mesh = plsc.VectorSubcoreMesh(core_axis_name="core", subcore_axis_name="subcore")

    @pl.kernel(
        out_type=jax.ShapeDtypeStruct((n_slots, Wp), U32), mesh=mesh,
        scratch_types=[pltpu.VMEM((TOP_K, SC_IDX_CHUNK), I32),
                       pltpu.VMEM((SC_IDX_CHUNK // SC_ROW_WIN, SC_ROW_WIN, Wp), U32),
                       pltpu.SemaphoreType.DMA((SC_IDX_CHUNK // SC_ROW_WIN,)), pltpu.SemaphoreType.DMA])
    def dispatch(x_hbm, s_hbm, o_hbm, idx_v, x_v, sem_r, sem_w):
        base = (lax.axis_index("core") * SC_SUBCORES + lax.axis_index("subcore")) * per
        n_win = SC_IDX_CHUNK // SC_ROW_WIN

        @pl.loop(0, per // SC_IDX_CHUNK)
        def _(c):
            t0 = base + c * SC_IDX_CHUNK
            reads = [pltpu.async_copy(x_hbm.at[pl.ds(t0 + j * SC_ROW_WIN, SC_ROW_WIN)], x_v.at[j], sem_r.at[j])
                     for j in range(n_win)]
            pltpu.sync_copy(s_hbm.at[:, pl.ds(t0, SC_IDX_CHUNK)], idx_v)
            copies = []
            for j in range(n_win):
                reads[j].wait()
                copies += [
                    pltpu.async_copy(x_v.at[j], o_hbm.at[idx_v.at[k, pl.ds(j * SC_ROW_WIN, SC_ROW_WIN)]], sem_w)
                    for k in range(TOP_K)]
            for cp in copies:
                cp.wait()

    return dispatch(h1p, slot)


def _experts_kernel(layer, be_ref, nv_ref, par_ref, nxt_ref, rows_ref, xs_hbm, wg_hbm, wu_hbm, wd_hbm,
                    ys_hbm, xbuf, ybuf, wg_f, wu_f, wd_f, wg_s, wu_s, wd_s, sem_w, sem_x, sem_y):
    nv = nv_ref[0]
    blk = xbuf.shape[1]
    sub = min(SUB_MOE, blk)
    n_sub = blk // sub

    def rows(i):
        return pl.ds(pl.multiple_of(i * blk, blk), blk)

    def x_copy(i, slot):
        return pltpu.make_async_copy(xs_hbm.at[rows(i)], xbuf.at[slot], sem_x.at[slot])

    def y_copy(i, slot):
        return pltpu.make_async_copy(ybuf.at[slot], ys_hbm.at[rows(i)], sem_y.at[slot])

    def weight_copies(e, slot):
        return [pltpu.make_async_copy(src.at[layer, e], dst.at[slot], sem_w.at[slot])
                for src, dst in ((wg_hbm, wg_f), (wu_hbm, wu_f), (wd_hbm, wd_f))]

    n_xbuf = xbuf.shape[0]
    for a in range(n_xbuf - 1):
        @pl.when(a < nv)
        def _():
            x_copy(a, a).start()
    for cp in weight_copies(be_ref[0], par_ref[0]):
        cp.start()

    def block(i, carry):
        slot = lax.rem(i, 2)
        xslot = lax.rem(i, n_xbuf)
        e = be_ref[i]
        wslot = par_ref[i]

        @pl.when(i + (n_xbuf - 1) < nv)
        def _():
            x_copy(i + (n_xbuf - 1), lax.rem(i + (n_xbuf - 1), n_xbuf)).start()

        @pl.when((i == 0) | (e != be_ref[jnp.maximum(i - 1, 0)]))
        def _():
            for cp in weight_copies(e, wslot):
                cp.wait()
            wg_s[...] = wg_f[wslot].astype(BF16)
            wu_s[...] = wu_f[wslot].astype(BF16)
            wd_s[...] = wd_f[wslot].astype(BF16)

            @pl.when(nxt_ref[i] >= 0)
            def _():
                for cp in weight_copies(nxt_ref[i], 1 - wslot):
                    cp.start()

        x_copy(i, xslot).wait()

        @pl.when(i >= 2)
        def _():
            y_copy(i - 2, slot).wait()

        def up(j):
            lo, hi = _unpack_bf16_pairs(xbuf[xslot, j * sub:(j + 1) * sub, :])
            x = jnp.concatenate([lo.astype(BF16), hi.astype(BF16)], axis=1)
            return (jnp.dot(x, wg_s[...], preferred_element_type=F32),
                    jnp.dot(x, wu_s[...], preferred_element_type=F32))

        def down(j, g, u):
            hmid = (g * jax.nn.sigmoid(g) * u).astype(BF16)
            y = jnp.dot(hmid, wd_s[...], preferred_element_type=F32)
            ybuf[slot, j * sub:(j + 1) * sub, :] = _pack_bf16_pairs(y)

        n_live = (rows_ref[i] + (sub - 1)) // sub

        for live in range(1, n_sub + 1):
            @pl.when(n_live == live)
            def _():
                pending = up(0)
                for j in range(live):
                    nxt = up(j + 1) if j + 1 < live else None
                    down(j, *pending)
                    pending = nxt
                if live < n_sub:
                    ybuf[slot, live * sub:, :] = jnp.zeros((blk - live * sub, ybuf.shape[2]), U32)

        y_copy(i, slot).start()
        return carry

    lax.fori_loop(0, nv, block, 0)

    @pl.when(nv >= 2)
    def _():
        y_copy(nv - 2, lax.rem(nv, 2)).wait()

    y_copy(nv - 1, lax.rem(nv - 1, 2)).wait()


def _experts(layer, blk_exp, n_valid, blk_par, blk_next, blk_rows, xs, wg, wu, wd):
    n_slots, Wp = xs.shape
    _, E, D, F = wg.shape

    hbm = pl.BlockSpec(memory_space=pl.ANY)
    grid_spec = pltpu.PrefetchScalarGridSpec(
        num_scalar_prefetch=5,
        grid=(1,),
        in_specs=[hbm, hbm, hbm, hbm],
        out_specs=hbm,
        scratch_shapes=[
            pltpu.VMEM((3, BLK_MOE, Wp), U32), pltpu.VMEM((2, BLK_MOE, Wp), U32),
            pltpu.VMEM((2, D, F), F32), pltpu.VMEM((2, D, F), F32), pltpu.VMEM((2, F, D), F32),
            pltpu.VMEM((D, F), BF16), pltpu.VMEM((D, F), BF16), pltpu.VMEM((F, D), BF16),
            pltpu.SemaphoreType.DMA((2,)), pltpu.SemaphoreType.DMA((3,)), pltpu.SemaphoreType.DMA((2,)),
        ],
    )
    return pl.pallas_call(
        functools.partial(_experts_kernel, layer),
        grid_spec=grid_spec,
        out_shape=jax.ShapeDtypeStruct((n_slots, Wp), U32),
        compiler_params=_cparams(1),
        name="experts",
    )(blk_exp, n_valid, blk_par, blk_next, blk_rows, xs, wg, wu, wd)


def _gather_rows(slot, ys):
    K, T = slot.shape
    Wp = ys.shape[1]
    n_workers = SC_CORES * SC_SUBCORES
    per = T // n_workers
    assert per % SC_IDX_CHUNK == 0 and K % SC_GATHER_PLANES == 0, (T, K)
    win = SC_ROW_WIN // 4
    mesh = plsc.VectorSubcoreMesh(core_axis_name="core", subcore_axis_name="subcore")
    groups = [(j, k0) for j in range(SC_IDX_CHUNK // win) for k0 in range(0, K, SC_GATHER_PLANES)]

    @pl.kernel(
        out_type=jax.ShapeDtypeStruct((K, T, Wp), U32), mesh=mesh,
        scratch_types=[pltpu.VMEM((K, SC_IDX_CHUNK), I32),
                       pltpu.VMEM((2, SC_GATHER_PLANES, win, Wp), U32),
                       pltpu.SemaphoreType.DMA((2,)), pltpu.SemaphoreType.DMA((2,))])
    def gather(y_hbm, s_hbm, g_hbm, idx_v, buf, sem_g, sem_w):
        base = (lax.axis_index("core") * SC_SUBCORES + lax.axis_index("subcore")) * per

        @pl.loop(0, per // SC_IDX_CHUNK)
        def _(c):
            t0 = base + c * SC_IDX_CHUNK
            pltpu.sync_copy(s_hbm.at[:, pl.ds(t0, SC_IDX_CHUNK)], idx_v)

            def start_reads(g):
                j, k0 = groups[g]
                return [pltpu.async_copy(y_hbm.at[idx_v.at[k0 + i, pl.ds(j * win, win)]],
                                         buf.at[g % 2, i], sem_g.at[g % 2])
                        for i in range(SC_GATHER_PLANES)]

            def start_writes(g):
                j, k0 = groups[g]
                return [pltpu.async_copy(buf.at[g % 2, i], g_hbm.at[k0 + i, pl.ds(t0 + j * win, win)],
                                         sem_w.at[g % 2])
                        for i in range(SC_GATHER_PLANES)]

            reads = start_reads(0)
            writes_prev = []
            for g in range(len(groups)):
                for cp in reads:
                    cp.wait()
                writes = start_writes(g)
                for cp in writes_prev:
                    cp.wait()
                if g + 1 < len(groups):
                    reads = start_reads(g + 1)
                writes_prev = writes
            for cp in writes_prev:
                cp.wait()

    return gather(ys, slot)


def _combine_kernel(yg_ref, gate_ref, h1_ref, swg_ref, swu_ref, swd_ref, lg_ref, lb_ref, *rest):
    out_ref = rest[-1]
    h1 = h1_ref[...]
    xb = h1.astype(BF16)
    g = jnp.dot(xb, swg_ref[...], preferred_element_type=F32)
    u = jnp.dot(xb, swu_ref[...], preferred_element_type=F32)
    hmid = (g * jax.nn.sigmoid(g) * u).astype(BF16)
    shared = jnp.dot(hmid, swd_ref[...], preferred_element_type=F32)

    gates = gate_ref[...]
    tc, wp = yg_ref.shape[1], yg_ref.shape[2]
    f_lo = jnp.zeros((tc, wp), F32)
    f_hi = jnp.zeros((tc, wp), F32)
    for k in range(TOP_K):
        lo, hi = _unpack_bf16_pairs(yg_ref[k])
        gk = gates[:, k:k + 1]
        f_lo = f_lo + gk * lo
        f_hi = f_hi + gk * hi
    routed = jnp.concatenate([f_lo, f_hi], axis=1)
    out_ref[...] = _layer_norm(ALPHA * h1 + (routed + shared), lg_ref[...], lb_ref[...])


def _combine(yg, gate_t, h1, swg, swu, swd, ln_g, ln_b, part, prev_out):
    T, D = h1.shape
    K, Tp, Wp = yg.shape
    tc = min(TD_MOE, Tp)
    off = part * (Tp // tc)
    row = lambda i: (i + off, 0)
    const = lambda i: (0, 0)
    res = lambda a: pl.BlockSpec(a.shape, const)
    in_specs = [
        pl.BlockSpec((K, tc, Wp), lambda i: (0, i, 0)),
        pl.BlockSpec((tc, TOP_K), row),
        pl.BlockSpec((tc, D), row),
        res(swg), res(swu), res(swd),
        pl.BlockSpec((1, D), const),
        pl.BlockSpec((1, D), const),
    ]
    args = [yg, gate_t, h1, swg, swu, swd, ln_g.reshape(1, D), ln_b.reshape(1, D)]
    aliases = {}
    if prev_out is not None:
        in_specs.append(pl.BlockSpec(memory_space=pl.ANY))
        args.append(prev_out)
        aliases = {len(args) - 1: 0}
    return pl.pallas_call(
        _combine_kernel,
        grid=(Tp // tc,),
        in_specs=in_specs,
        out_specs=pl.BlockSpec((tc, D), row),
        out_shape=jax.ShapeDtypeStruct((T, D), F32),
        input_output_aliases=aliases,
        compiler_params=_cparams(1),
        name="combine",
    )(*args)


def _moe_layer(layer, h1, h1p, router_w, router_bias, wg, wu, wd, swg, swu, swd, ln_g, ln_b):
    T, D = h1.shape
    E = N_EXPERTS
    M = T * TOP_K
    eidx, gate, rank, cnt = _router(h1, router_w, router_bias)

    counts = cnt[:, 0].astype(I32)
    padded = (counts + BLK_MOE - 1) // BLK_MOE * BLK_MOE
    pad_end = jnp.cumsum(padded)
    pad_start = pad_end - padded
    nb = -(-(M + E * (BLK_MOE - 1)) // BLK_MOE)
    n_slots = nb * BLK_MOE
    slot = _slots(pad_start.astype(I32), eidx, rank)
    n_valid = (pad_end[-1] // BLK_MOE).astype(I32).reshape(1)
    blk_start = jnp.arange(nb, dtype=I32) * BLK_MOE
    blk_exp = jnp.minimum(jnp.sum((pad_end[None, :] <= blk_start[:, None]).astype(I32), axis=1), E - 1)

    eids = jnp.arange(E, dtype=I32)
    nonempty = counts > 0
    order = jnp.cumsum(nonempty.astype(I32)) - 1
    nxt_e = lax.cummin(jnp.where(nonempty, eids, E), reverse=True)
    nxt_e = jnp.concatenate([nxt_e[1:], jnp.full((1,), E, I32)])
    blk_par = (order % 2)[blk_exp].astype(I32)
    blk_next = jnp.where(nxt_e < E, nxt_e, -1)[blk_exp].astype(I32)
    blk_rows = jnp.clip(counts[blk_exp] - (blk_start - pad_start[blk_exp]), 0, BLK_MOE).astype(I32)

    xs = _dispatch(slot, h1p, n_slots)
    ys = _experts(layer, blk_exp, n_valid, blk_par, blk_next, blk_rows, xs, wg, wu, wd)
    n_parts = MOE_COMBINE_PARTS if T % (MOE_COMBINE_PARTS * SC_CORES * SC_SUBCORES * SC_IDX_CHUNK) == 0 else 1
    tp = T // n_parts
    gate_t = gate.T
    shared_w = (swg.astype(BF16), swu.astype(BF16), swd.astype(BF16))
    parts = [_gather_rows(slot[:, p * tp:(p + 1) * tp], ys) for p in range(n_parts)]
    out = None
    for p in range(n_parts):
        out = _combine(parts[p], gate_t, h1, *shared_w, ln_g, ln_b, p, out)
    return out


def kernel(x, emb_ln_g, emb_ln_b, w_in, b_gate, lam_q1, lam_k1, lam_q2, lam_k2, subln_g, w_proj_a, na_rpb, w_proj_b, w_out, ln1_g, ln1_b, router_w, router_bias, exp_w_gate, exp_w_up, exp_w_down, sh_w_gate, sh_w_up, sh_w_down, ln2_g, ln2_b):
    B, S, D = x.shape
    T = B * S
    rows = S // GRID_W
    h = x.reshape(T, D)
    for l in range(DEPTH):
        lam_init = 0.8 - 0.6 * math.exp(-0.3 * l)
        outs = _inproj(h, emb_ln_g, emb_ln_b, w_in[l].astype(BF16), apply_ln=(l == 0))
        if l == 0:
            h, outs = outs[0], outs[1:]
        qa, ka, va, qn, kn, vn, ga, gb = outs
        lamv = jnp.stack([lam_q1[l], lam_k1[l], lam_q2[l], lam_k2[l]]).astype(F32)
        oa = _diffattn(qa.reshape(B, S, -1), ka.reshape(B, S, -1), va.reshape(B, S, -1),
                       lamv, subln_g[l], lam_init)
        ob = _natten(qn.reshape(B, S, -1), kn.reshape(B, S, -1), vn.reshape(B, S, -1),
                     _na_bias_table(na_rpb[l], rows))
        h1, h1p = _mixout(oa.reshape(T, -1), ob.reshape(T, -1), ga, gb, h, b_gate[l],
                          w_proj_a[l].astype(BF16), w_proj_b[l].astype(BF16), w_out[l].astype(BF16),
                          ln1_g[l], ln1_b[l])
        h = _moe_layer(l, h1, h1p, router_w[l], router_bias[l], exp_w_gate, exp_w_up, exp_w_down,
                       sh_w_gate[l], sh_w_up[l], sh_w_down[l], ln2_g[l], ln2_b[l])
    return h.reshape(B, S, D)
```

```python
import functools
import math

import numpy as np
import jax
import jax.numpy as jnp
from jax import lax
from jax.experimental import pallas as pl
from jax.experimental.pallas import tpu as pltpu
from jax.experimental.pallas import tpu_sc as plsc

F32 = jnp.float32
BF16 = jnp.bfloat16
U32 = jnp.uint32
I32 = jnp.int32

DA_HEADS = 8
DA_HEAD_DIM = 64
NA_HEADS = 16
NA_HEAD_DIM = 32
GRID_W = 64
WIN_R = 8
WIN_C = 16
N_EXPERTS = 256
TOP_K = 8
N_GROUPS = 8
TOPK_GROUPS = 4
ROUTE_SCALE = 2.5
DEPTH = 2
ALPHA = (2 * DEPTH) ** 0.25
LN_EPS = 1e-5
LOG2E = 1.4426950408889634

LANES = 128
NA_HEADS_PER_BLOCK = LANES // NA_HEAD_DIM
VMEM_LIMIT = 56 * 1024 * 1024
SC_CORES = 2
SC_SUBCORES = 16
SC_IDX_CHUNK = 128
SC_ROW_WIN = 64
SC_GATHER_PLANES = 4

TM_PROJ = 512
TM_MIX = 1024
SUB_MIX = 256
TQ_DA = 2048
TQ_SUB_DA = 128
NA_ROWS_PER_ITER = 4
TT_ROUTER = 1024
TD_MOE = 512
BLK_MOE = 512
SUB_MOE = 256
MOE_COMBINE_PARTS = 8
NEG_BIG = -1e30


def _cparams(n_axes, flags=None):
    return pltpu.CompilerParams(
        dimension_semantics=("arbitrary",) * n_axes, vmem_limit_bytes=VMEM_LIMIT, flags=flags)


def _layer_norm(x, g, b):
    mu = jnp.mean(x, axis=-1, keepdims=True)
    xc = x - mu
    var = jnp.mean(xc * xc, axis=-1, keepdims=True)
    return xc * lax.rsqrt(var + LN_EPS) * g + b


def _pack_bf16_pairs(y):
    w = y.shape[1] // 2
    lo = lax.bitcast_convert_type(y[:, :w].astype(BF16).astype(F32), U32)
    hi = lax.bitcast_convert_type(y[:, w:].astype(BF16).astype(F32), U32)
    return (hi & jnp.uint32(0xFFFF0000)) | (lo >> 16)


def _unpack_bf16_pairs(u):
    lo = lax.bitcast_convert_type(u << 16, F32)
    hi = lax.bitcast_convert_type(u & jnp.uint32(0xFFFF0000), F32)
    return lo, hi


def _inproj_kernel(apply_ln, seg_widths, seg_scales, x_ref, g_ref, b_ref, w_ref, *out_refs):
    x = x_ref[...]
    if apply_ln:
        x = _layer_norm(x, g_ref[...], b_ref[...])
        out_refs[0][...] = x
        out_refs = out_refs[1:]
    xb = x.astype(BF16)
    off = 0
    for ref, width, scale in zip(out_refs, seg_widths, seg_scales):
        y = jnp.dot(xb, w_ref[:, off:off + width], preferred_element_type=F32)
        if scale != 1.0:
            y = y * scale
        ref[...] = y.astype(BF16)
        off += width


def _inproj(x, ln_g, ln_b, w_in_bf16, apply_ln):
    T, D = x.shape
    da_w = DA_HEADS * 2 * DA_HEAD_DIM
    na_w = NA_HEADS * NA_HEAD_DIM
    seg_widths = (da_w, da_w, da_w, na_w, na_w, na_w, D, D)
    seg_scales = (DA_HEAD_DIM ** -0.5 * LOG2E, 1.0, 1.0, NA_HEAD_DIM ** -0.5 * LOG2E, 1.0, 1.0, 1.0, 1.0)
    tm = min(TM_PROJ, T)
    n_cols = w_in_bf16.shape[1]
    row = lambda i: (i, 0)
    const = lambda i: (0, 0)
    out_shape = [jax.ShapeDtypeStruct((T, w), BF16) for w in seg_widths]
    out_specs = [pl.BlockSpec((tm, w), row) for w in seg_widths]
    if apply_ln:
        out_shape = [jax.ShapeDtypeStruct((T, D), F32)] + out_shape
        out_specs = [pl.BlockSpec((tm, D), row)] + out_specs
    return pl.pallas_call(
        functools.partial(_inproj_kernel, apply_ln, seg_widths, seg_scales),
        grid=(T // tm,),
        in_specs=[
            pl.BlockSpec((tm, D), row),
            pl.BlockSpec((1, D), const),
            pl.BlockSpec((1, D), const),
            pl.BlockSpec((D, n_cols), const, pipeline_mode=pl.Buffered(1)),
        ],
        out_specs=out_specs,
        out_shape=out_shape,
        compiler_params=_cparams(1),
        name="inproj",
    )(x, ln_g.reshape(1, D), ln_b.reshape(1, D), w_in_bf16)


def _diffattn_kernel(lam_init, slopes_ref, q_ref, k_ref, v_ref, lamv_ref, g_ref, o_ref, bias_ref):
    h = pl.program_id(0)
    qi = pl.program_id(1)
    b = pl.program_id(2)
    tq = q_ref.shape[1]
    S = k_ref.shape[1]
    d = DA_HEAD_DIM

    @pl.when(b == 0)
    def _():
        qpos = qi * tq + lax.broadcasted_iota(I32, (tq, S), 0)
        kpos = lax.broadcasted_iota(I32, (tq, S), 1)
        bias_ref[...] = jnp.abs(qpos - kpos).astype(F32) * (-slopes_ref[h])

    lv = lamv_ref[...]
    lam = (jnp.exp(jnp.sum(lv[0:1] * lv[1:2], axis=-1, keepdims=True))
           - jnp.exp(jnp.sum(lv[2:3] * lv[3:4], axis=-1, keepdims=True)) + lam_init)

    k = k_ref[0]
    v1 = jnp.concatenate([v_ref[0], jnp.ones((S, 2 * d), BF16)], axis=1)
    gain = g_ref[...] * (1.0 - lam_init)
    nt = (((1,), (1,)), ((), ()))
    ts = min(TQ_SUB_DA, tq)
    lane = lax.broadcasted_iota(I32, (ts, 2 * d), 1)

    def scores(j):
        q = q_ref[0, j * ts:(j + 1) * ts, :]
        nb = bias_ref[j * ts:(j + 1) * ts, :]
        zero = jnp.zeros_like(q)
        s1 = lax.dot_general(jnp.where(lane < d, q, zero), k, nt, preferred_element_type=F32) + nb
        s2 = lax.dot_general(jnp.where(lane >= d, q, zero), k, nt, preferred_element_type=F32) + nb
        return s1, s2

    def weights(s1, s2):
        return (jnp.exp2((s1 - jnp.max(s1, axis=-1, keepdims=True)).astype(BF16)),
                jnp.exp2((s2 - jnp.max(s2, axis=-1, keepdims=True)).astype(BF16)))

    def finish(j, e1, e2):
        p = jnp.dot(jnp.concatenate([e1, e2], axis=0), v1, preferred_element_type=F32)
        p1, p2 = p[:ts], p[ts:]
        o = p1[:, :2 * d] / p1[:, 2 * d:] - lam * (p2[:, :2 * d] / p2[:, 2 * d:])
        ms = jnp.mean(o * o, axis=-1, keepdims=True)
        o = o * lax.rsqrt(ms + LN_EPS) * gain
        o_ref[0, j * ts:(j + 1) * ts, :] = o.astype(BF16)

    n_sub = tq // ts
    s_next = scores(0)
    e_cur = weights(*s_next)
    s_next = scores(1) if n_sub > 1 else None
    for j in range(n_sub):
        s_after = scores(j + 2) if j + 2 < n_sub else None
        finish(j, *e_cur)
        if s_next is not None:
            e_cur = weights(*s_next)
        s_next = s_after


def _diffattn(qa, ka, va, lamv, subln_g, lam_init):
    B, S, W = qa.shape
    hw = 2 * DA_HEAD_DIM
    tq = min(TQ_DA, S)
    slopes = jnp.asarray(2.0 ** (-8.0 * np.arange(1, DA_HEADS + 1) / DA_HEADS) * LOG2E, F32)
    grid_spec = pltpu.PrefetchScalarGridSpec(
        num_scalar_prefetch=1,
        grid=(DA_HEADS, S // tq, B),
        in_specs=[
            pl.BlockSpec((1, tq, hw), lambda h, qi, b, sl: (b, qi, h)),
            pl.BlockSpec((1, S, hw), lambda h, qi, b, sl: (b, 0, h)),
            pl.BlockSpec((1, S, hw), lambda h, qi, b, sl: (b, 0, h)),
            pl.BlockSpec((4, DA_HEAD_DIM), lambda h, qi, b, sl: (0, 0)),
            pl.BlockSpec((1, hw), lambda h, qi, b, sl: (0, 0)),
        ],
        out_specs=pl.BlockSpec((1, tq, hw), lambda h, qi, b, sl: (b, qi, h)),
        scratch_shapes=[pltpu.VMEM((tq, S), F32)],
    )
    return pl.pallas_call(
        functools.partial(_diffattn_kernel, lam_init),
        grid_spec=grid_spec,
        out_shape=jax.ShapeDtypeStruct((B, S, W), BF16),
        compiler_params=_cparams(3),
        name="diffattn",
    )(slopes, qa, ka, va, lamv, subln_g.reshape(1, hw))


def _na_bias_table(rpb, rows):
    kh = min(WIN_R, rows)
    qcol = np.arange(GRID_W)
    kcol = np.arange(GRID_W)
    cs = np.clip(qcol - WIN_C // 2, 0, GRID_W - WIN_C)
    col_mask = (kcol[None, :] >= cs[:, None]) & (kcol[None, :] < cs[:, None] + WIN_C)
    col_off = np.clip(kcol[None, :] - qcol[:, None] + WIN_C - 1, 0, 2 * WIN_C - 2)
    v = np.arange(kh)
    j = np.arange(kh)
    row_idx = np.clip(j[None, :] - v[:, None] + WIN_R - 1, 0, 2 * WIN_R - 2)
    sel_r = np.zeros((2 * WIN_R - 1, kh * kh), np.float32)
    sel_r[row_idx.reshape(-1), np.arange(kh * kh)] = 1.0
    sel_c = np.zeros((2 * WIN_C - 1, GRID_W * GRID_W), np.float32)
    sel_c[col_off.reshape(-1), np.arange(GRID_W * GRID_W)] = 1.0
    hi = lax.Precision.HIGHEST
    t = jnp.einsum("hrc,rp->hpc", rpb * LOG2E, jnp.asarray(sel_r), precision=hi)
    t = jnp.einsum("hpc,cq->hpq", t, jnp.asarray(sel_c), precision=hi)
    t = t.reshape(NA_HEADS, kh, kh, GRID_W, GRID_W)
    t = jnp.where(jnp.asarray(col_mask)[None, None, None], t, NEG_BIG)
    t = t.transpose(1, 0, 3, 2, 4)
    nblk = NA_HEADS // NA_HEADS_PER_BLOCK
    return t.reshape(kh, nblk, NA_HEADS_PER_BLOCK * GRID_W, kh * GRID_W)


def _natten_kernel(rows, kh, q_ref, k_ref, v_ref, bias_ref, o_ref):
    hid = lax.broadcasted_iota(I32, (GRID_W, LANES), 1) // NA_HEAD_DIM

    def scores(r):
        rs = jnp.clip(r - WIN_R // 2, 0, rows - kh)
        q0 = pl.multiple_of(r * GRID_W, GRID_W)
        k0 = pl.multiple_of(rs * GRID_W, GRID_W)
        qr = q_ref[0, pl.ds(q0, GRID_W), :]
        zero = jnp.zeros_like(qr)
        qq = jnp.concatenate(
            [jnp.where(hid == hh, qr, zero) for hh in range(NA_HEADS_PER_BLOCK)], axis=0)
        kb = k_ref[0, pl.ds(k0, kh * GRID_W), :]
        s = lax.dot_general(qq, kb, (((1,), (1,)), ((), ())), preferred_element_type=F32)
        return s + bias_ref[r - rs, 0]

    def weights(s):
        e = jnp.exp2(s - jnp.max(s, axis=-1, keepdims=True))
        return e.astype(BF16), 1.0 / jnp.sum(e, axis=-1, keepdims=True)

    def finish(r, e, rl):
        rs = jnp.clip(r - WIN_R // 2, 0, rows - kh)
        q0 = pl.multiple_of(r * GRID_W, GRID_W)
        k0 = pl.multiple_of(rs * GRID_W, GRID_W)
        vb = v_ref[0, pl.ds(k0, kh * GRID_W), :]
        oo = jnp.dot(e, vb, preferred_element_type=F32) * rl
        o = jnp.zeros((GRID_W, LANES), F32)
        for hh in range(NA_HEADS_PER_BLOCK):
            o = o + jnp.where(hid == hh, oo[hh * GRID_W:(hh + 1) * GRID_W], 0.0)
        o_ref[0, pl.ds(q0, GRID_W), :] = o.astype(BF16)

    def row_group(g, carry):
        rr = [g * NA_ROWS_PER_ITER + i for i in range(NA_ROWS_PER_ITER)]
        ss = [scores(r) for r in rr]
        ws = [weights(s) for s in ss]
        for r, (e, rl) in zip(rr, ws):
            finish(r, e, rl)
        return carry

    lax.fori_loop(0, rows // NA_ROWS_PER_ITER, row_group, 0)


def _natten(qn, kn, vn, bias_tab):
    B, S, W = qn.shape
    rows = S // GRID_W
    kh = min(WIN_R, rows)
    nblk = W // LANES
    blk = lambda g, b: (b, 0, g)
    return pl.pallas_call(
        functools.partial(_natten_kernel, rows, kh),
        grid=(nblk, B),
        in_specs=[
            pl.BlockSpec((1, S, LANES), blk),
            pl.BlockSpec((1, S, LANES), blk),
            pl.BlockSpec((1, S, LANES), blk),
            pl.BlockSpec((kh, 1, NA_HEADS_PER_BLOCK * GRID_W, kh * GRID_W), lambda g, b: (0, g, 0, 0)),
        ],
        out_specs=pl.BlockSpec((1, S, LANES), blk),
        out_shape=jax.ShapeDtypeStruct((B, S, W), BF16),
        compiler_params=_cparams(2),
        name="natten",
    )(qn, kn, vn, bias_tab)


def _mixout_kernel(oa_ref, ob_ref, ga_ref, gb_ref, h_ref, bg_ref, wa_ref, wb_ref, wo_ref,
                   lg_ref, lb_ref, h1_ref, h1p_ref):
    tm = h_ref.shape[0]
    ts = min(SUB_MIX, tm)

    def branches(j):
        r = slice(j * ts, (j + 1) * ts)
        return (jnp.dot(oa_ref[r, :], wa_ref[...], preferred_element_type=F32),
                jnp.dot(ob_ref[r, :], wb_ref[...], preferred_element_type=F32))

    def finish(j, ya, yb):
        r = slice(j * ts, (j + 1) * ts)
        g_a = jax.nn.sigmoid(ga_ref[r, :].astype(F32) + bg_ref[0:1, :])
        g_b = jax.nn.sigmoid(gb_ref[r, :].astype(F32) + bg_ref[1:2, :])
        z = (g_a * ya + g_b * yb).astype(BF16)
        m = jnp.dot(z, wo_ref[...], preferred_element_type=F32)
        y = _layer_norm(ALPHA * h_ref[r, :] + m, lg_ref[...], lb_ref[...])
        h1_ref[r, :] = y
        h1p_ref[r, :] = _pack_bf16_pairs(y)

    pending = branches(0)
    for j in range(tm // ts):
        nxt = branches(j + 1) if (j + 1) * ts < tm else None
        finish(j, *pending)
        pending = nxt


def _mixout(oa, ob, ga, gb, h, b_gate, wa, wb, wo, ln_g, ln_b):
    T, D = h.shape
    tm = min(TM_MIX, T)
    row = lambda i: (i, 0)
    const = lambda i: (0, 0)
    res = lambda a: pl.BlockSpec(a.shape, const, pipeline_mode=pl.Buffered(1))
    return pl.pallas_call(
        _mixout_kernel,
        grid=(T // tm,),
        in_specs=[
            pl.BlockSpec((tm, oa.shape[1]), row),
            pl.BlockSpec((tm, ob.shape[1]), row),
            pl.BlockSpec((tm, D), row),
            pl.BlockSpec((tm, D), row),
            pl.BlockSpec((tm, D), row),
            pl.BlockSpec((2, D), const),
            res(wa), res(wb), res(wo),
            pl.BlockSpec((1, D), const),
            pl.BlockSpec((1, D), const),
        ],
        out_specs=[pl.BlockSpec((tm, D), row), pl.BlockSpec((tm, D // 2), row)],
        out_shape=[jax.ShapeDtypeStruct((T, D), F32), jax.ShapeDtypeStruct((T, D // 2), U32)],
        compiler_params=_cparams(1),
        name="mixout",
    )(oa, ob, ga, gb, h, b_gate, wa, wb, wo, ln_g.reshape(1, D), ln_b.reshape(1, D))


def _router_kernel(h_ref, rwt_ref, rb_ref, tri_ref, eidx_ref, gate_ref, rank_ref, cnt_ref, carry_ref):
    i = pl.program_id(0)
    tt = h_ref.shape[0]
    E, G = N_EXPERTS, N_GROUPS
    P = E // G
    neg = -jnp.inf

    @pl.when(i == 0)
    def _():
        carry_ref[...] = jnp.zeros_like(carry_ref)

    hb = h_ref[...].astype(BF16)
    logits = lax.dot_general(rwt_ref[...], hb, (((1,), (1,)), ((), ())), preferred_element_type=F32)
    scores = jax.nn.sigmoid(logits)
    biased = scores + rb_ref[...]
    b3 = biased.reshape(G, P, tt)
    s3 = scores.reshape(G, P, tt)
    pi = lax.broadcasted_iota(I32, (G, P, tt), 1)
    ei = lax.broadcasted_iota(I32, (G, P, tt), 0) * P + pi

    m1 = jnp.max(b3, axis=1, keepdims=True)
    i1 = jnp.min(jnp.where(b3 == m1, pi, P), axis=1, keepdims=True)
    m2 = jnp.max(jnp.where(pi == i1, neg, b3), axis=1, keepdims=True)
    grp = m1 + m2
    gi = lax.broadcasted_iota(I32, (G, 1, tt), 0)
    gsel = jnp.zeros((G, 1, tt), F32)
    for _ in range(TOPK_GROUPS):
        gm = jnp.max(grp, axis=0, keepdims=True)
        gidx = jnp.min(jnp.where(grp == gm, gi, G), axis=0, keepdims=True)
        hit = gi == gidx
        gsel = jnp.where(hit, 1.0, gsel)
        grp = jnp.where(hit, neg, grp)

    cand = jnp.where(gsel > 0.0, b3, neg)
    sel = jnp.zeros((G, P, tt), F32)
    eids, gates = [], []
    for _ in range(TOP_K):
        mk = jnp.max(cand, axis=(0, 1), keepdims=True)
        ik = jnp.min(jnp.where(cand == mk, ei, E), axis=(0, 1), keepdims=True)
        hit = ei == ik
        gates.append(jnp.sum(jnp.where(hit, s3, 0.0), axis=(0, 1), keepdims=True))
        eids.append(ik)
        cand = jnp.where(hit, neg, cand)
        sel = jnp.where(hit, 1.0, sel)

    gsum = gates[0]
    for gk in gates[1:]:
        gsum = gsum + gk
    gscale = ROUTE_SCALE / gsum

    sel2 = sel.reshape(E, tt)
    prefix = jnp.dot(sel2.astype(BF16), tri_ref[...], preferred_element_type=F32)
    base3 = (prefix + carry_ref[...]).reshape(G, P, tt)
    ranks = [jnp.sum(jnp.where(ei == ik, base3, 0.0), axis=(0, 1), keepdims=True) for ik in eids]
    carry_ref[...] = carry_ref[...] + jnp.sum(sel2, axis=1, keepdims=True)
    cnt_ref[...] = carry_ref[...]

    eidx_ref[...] = jnp.concatenate([x.reshape(1, tt) for x in eids], axis=0)
    gate_ref[...] = jnp.concatenate([(g * gscale).reshape(1, tt) for g in gates], axis=0)
    rank_ref[...] = jnp.concatenate([x.reshape(1, tt) for x in ranks], axis=0).astype(I32)


def _router(h1, router_w, router_bias):
    T, D = h1.shape
    E = N_EXPERTS
    tt = min(TT_ROUTER, T)
    rwt = router_w.T.astype(BF16)
    tri = jnp.asarray(np.triu(np.ones((tt, tt), np.float32), k=1), BF16)
    const = lambda i: (0, 0)
    col = lambda i: (0, i)
    return pl.pallas_call(
        _router_kernel,
        grid=(T // tt,),
        in_specs=[
            pl.BlockSpec((tt, D), lambda i: (i, 0)),
            pl.BlockSpec((E, D), const),
            pl.BlockSpec((E, 1), const),
            pl.BlockSpec((tt, tt), const),
        ],
        out_specs=[
            pl.BlockSpec((TOP_K, tt), col),
            pl.BlockSpec((TOP_K, tt), col),
            pl.BlockSpec((TOP_K, tt), col),
            pl.BlockSpec((E, 1), const),
        ],
        out_shape=[
            jax.ShapeDtypeStruct((TOP_K, T), I32),
            jax.ShapeDtypeStruct((TOP_K, T), F32),
            jax.ShapeDtypeStruct((TOP_K, T), I32),
            jax.ShapeDtypeStruct((E, 1), F32),
        ],
        scratch_shapes=[pltpu.VMEM((E, 1), F32)],
        compiler_params=_cparams(1),
        name="router",
    )(h1, rwt, router_bias.reshape(E, 1).astype(F32), tri)


def _slots_kernel(pstart_ref, eidx_ref, rank_ref, slot_ref):
    eidx = eidx_ref[...]

    def add_expert(e, acc):
        return acc + jnp.where(eidx == e, pstart_ref[e], 0)

    slot_ref[...] = lax.fori_loop(0, N_EXPERTS, add_expert, rank_ref[...], unroll=8)


def _slots(pad_start, eidx, rank):
    K, T = eidx.shape
    tt = min(2048, T)
    col = lambda i, ps: (0, i)
    grid_spec = pltpu.PrefetchScalarGridSpec(
        num_scalar_prefetch=1,
        grid=(T // tt,),
        in_specs=[pl.BlockSpec((K, tt), col), pl.BlockSpec((K, tt), col)],
        out_specs=pl.BlockSpec((K, tt), col),
    )
    return pl.pallas_call(
        _slots_kernel,
        grid_spec=grid_spec,
        out_shape=jax.ShapeDtypeStruct((K, T), I32),
        compiler_params=_cparams(1),
        name="slots",
    )(pad_start, eidx, rank)


def _dispatch(slot, h1p, n_slots):
    T, Wp = h1p.shape
    n_workers = SC_CORES * SC_SUBCORES
    per = T // n_workers
    assert per % SC_IDX_CHUNK == 0, (T, n_workers)
    mesh = plsc.VectorSubcoreMesh(core_axis_name="core", subcore_axis_name="subcore")

    @pl.kernel(
        out_type=jax.ShapeDtypeStruct((n_slots, Wp), U32), mesh=mesh,
        scratch_types=[pltpu.VMEM((TOP_K, SC_IDX_CHUNK), I32),
                       pltpu.VMEM((SC_IDX_CHUNK // SC_ROW_WIN, SC_ROW_WIN, Wp), U32),
                       pltpu.SemaphoreType.DMA((SC_IDX_CHUNK // SC_ROW_WIN,)), pltpu.SemaphoreType.DMA])
    def dispatch(x_hbm, s_hbm, o_hbm, idx_v, x_v, sem_r, sem_w):
        base = (lax.axis_index("core") * SC_SUBCORES + lax.axis_index("subcore")) * per
        n_win = SC_IDX_CHUNK // SC_ROW_WIN

        @pl.loop(0, per // SC_IDX_CHUNK)
        def _(c):
            t0 = base + c * SC_IDX_CHUNK
            reads = [pltpu.async_copy(x_hbm.at[pl.ds(t0 + j * SC_ROW_WIN, SC_ROW_WIN)], x_v.at[j], sem_r.at[j])
                     for j in range(n_win)]
            pltpu.sync_copy(s_hbm.at[:, pl.ds(t0, SC_IDX_CHUNK)], idx_v)
            copies = []
            for j in range(n_win):
                reads[j].wait()
                copies += [
                    pltpu.async_copy(x_v.at[j], o_hbm.at[idx_v.at[k, pl.ds(j * SC_ROW_WIN, SC_ROW_WIN)]], sem_w)
                    for k in range(TOP_K)]
            for cp in copies:
                cp.wait()

    return dispatch(h1p, slot)


def _experts_kernel(layer, be_ref, nv_ref, par_ref, nxt_ref, xs_hbm, wg_hbm, wu_hbm, wd_hbm, ys_hbm,
                    xbuf, ybuf, wg_f, wu_f, wd_f, wg_s, wu_s, wd_s, sem_w, sem_x, sem_y):
    nv = nv_ref[0]
    blk = xbuf.shape[1]
    sub = min(SUB_MOE, blk)
    n_sub = blk // sub

    def rows(i):
        return pl.ds(pl.multiple_of(i * blk, blk), blk)

    def x_copy(i, slot):
        return pltpu.make_async_copy(xs_hbm.at[rows(i)], xbuf.at[slot], sem_x.at[slot])

    def y_copy(i, slot):
        return pltpu.make_async_copy(ybuf.at[slot], ys_hbm.at[rows(i)], sem_y.at[slot])

    def weight_copies(e, slot):
        return [pltpu.make_async_copy(src.at[layer, e], dst.at[slot], sem_w.at[slot])
                for src, dst in ((wg_hbm, wg_f), (wu_hbm, wu_f), (wd_hbm, wd_f))]

    n_xbuf = xbuf.shape[0]
    for a in range(n_xbuf - 1):
        @pl.when(a < nv)
        def _():
            x_copy(a, a).start()
    for cp in weight_copies(be_ref[0], par_ref[0]):
        cp.start()

    def block(i, carry):
        slot = lax.rem(i, 2)
        xslot = lax.rem(i, n_xbuf)
        e = be_ref[i]
        wslot = par_ref[i]

        @pl.when(i + (n_xbuf - 1) < nv)
        def _():
            x_copy(i + (n_xbuf - 1), lax.rem(i + (n_xbuf - 1), n_xbuf)).start()

        @pl.when((i == 0) | (e != be_ref[jnp.maximum(i - 1, 0)]))
        def _():
            for cp in weight_copies(e, wslot):
                cp.wait()
            wg_s[...] = wg_f[wslot].astype(BF16)
            wu_s[...] = wu_f[wslot].astype(BF16)
            wd_s[...] = wd_f[wslot].astype(BF16)

            @pl.when(nxt_ref[i] >= 0)
            def _():
                for cp in weight_copies(nxt_ref[i], 1 - wslot):
                    cp.start()

        x_copy(i, xslot).wait()

        @pl.when(i >= 2)
        def _():
            y_copy(i - 2, slot).wait()

        def up(j):
            lo, hi = _unpack_bf16_pairs(xbuf[xslot, j * sub:(j + 1) * sub, :])
            x = jnp.concatenate([lo.astype(BF16), hi.astype(BF16)], axis=1)
            return (jnp.dot(x, wg_s[...], preferred_element_type=F32),
                    jnp.dot(x, wu_s[...], preferred_element_type=F32))

        def down(j, g, u):
            hmid = (g * jax.nn.sigmoid(g) * u).astype(BF16)
            y = jnp.dot(hmid, wd_s[...], preferred_element_type=F32)
            ybuf[slot, j * sub:(j + 1) * sub, :] = _pack_bf16_pairs(y)

        pending = up(0)
        for j in range(n_sub):
            nxt = up(j + 1) if j + 1 < n_sub else None
            down(j, *pending)
            pending = nxt

        y_copy(i, slot).start()
        return carry

    lax.fori_loop(0, nv, block, 0)

    @pl.when(nv >= 2)
    def _():
        y_copy(nv - 2, lax.rem(nv, 2)).wait()

    y_copy(nv - 1, lax.rem(nv - 1, 2)).wait()


def _experts(layer, blk_exp, n_valid, blk_par, blk_next, xs, wg, wu, wd):
    n_slots, Wp = xs.shape
    _, E, D, F = wg.shape

    hbm = pl.BlockSpec(memory_space=pl.ANY)
    grid_spec = pltpu.PrefetchScalarGridSpec(
        num_scalar_prefetch=4,
        grid=(1,),
        in_specs=[hbm, hbm, hbm, hbm],
        out_specs=hbm,
        scratch_shapes=[
            pltpu.VMEM((3, BLK_MOE, Wp), U32), pltpu.VMEM((2, BLK_MOE, Wp), U32),
            pltpu.VMEM((2, D, F), F32), pltpu.VMEM((2, D, F), F32), pltpu.VMEM((2, F, D), F32),
            pltpu.VMEM((D, F), BF16), pltpu.VMEM((D, F), BF16), pltpu.VMEM((F, D), BF16),
            pltpu.SemaphoreType.DMA((2,)), pltpu.SemaphoreType.DMA((3,)), pltpu.SemaphoreType.DMA((2,)),
        ],
    )
    return pl.pallas_call(
        functools.partial(_experts_kernel, layer),
        grid_spec=grid_spec,
        out_shape=jax.ShapeDtypeStruct((n_slots, Wp), U32),
        compiler_params=_cparams(1),
        name="experts",
    )(blk_exp, n_valid, blk_par, blk_next, xs, wg, wu, wd)


def _gather_rows(slot, ys):
    K, T = slot.shape
    Wp = ys.shape[1]
    n_workers = SC_CORES * SC_SUBCORES
    per = T // n_workers
    assert per % SC_IDX_CHUNK == 0 and K % SC_GATHER_PLANES == 0, (T, K)
    win = SC_ROW_WIN // 4
    mesh = plsc.VectorSubcoreMesh(core_axis_name="core", subcore_axis_name="subcore")
    groups = [(j, k0) for j in range(SC_IDX_CHUNK // win) for k0 in range(0, K, SC_GATHER_PLANES)]

    @pl.kernel(
        out_type=jax.ShapeDtypeStruct((K, T, Wp), U32), mesh=mesh,
        scratch_types=[pltpu.VMEM((K, SC_IDX_CHUNK), I32),
                       pltpu.VMEM((2, SC_GATHER_PLANES, win, Wp), U32),
                       pltpu.SemaphoreType.DMA((2,)), pltpu.SemaphoreType.DMA((2,))])
    def gather(y_hbm, s_hbm, g_hbm, idx_v, buf, sem_g, sem_w):
        base = (lax.axis_index("core") * SC_SUBCORES + lax.axis_index("subcore")) * per

        @pl.loop(0, per // SC_IDX_CHUNK)
        def _(c):
            t0 = base + c * SC_IDX_CHUNK
            pltpu.sync_copy(s_hbm.at[:, pl.ds(t0, SC_IDX_CHUNK)], idx_v)

            def start_reads(g):
                j, k0 = groups[g]
                return [pltpu.async_copy(y_hbm.at[idx_v.at[k0 + i, pl.ds(j * win, win)]],
                                         buf.at[g % 2, i], sem_g.at[g % 2])
                        for i in range(SC_GATHER_PLANES)]

            def start_writes(g):
                j, k0 = groups[g]
                return [pltpu.async_copy(buf.at[g % 2, i], g_hbm.at[k0 + i, pl.ds(t0 + j * win, win)],
                                         sem_w.at[g % 2])
                        for i in range(SC_GATHER_PLANES)]

            reads = start_reads(0)
            writes_prev = []
            for g in range(len(groups)):
                for cp in reads:
                    cp.wait()
                writes = start_writes(g)
                for cp in writes_prev:
                    cp.wait()
                if g + 1 < len(groups):
                    reads = start_reads(g + 1)
                writes_prev = writes
            for cp in writes_prev:
                cp.wait()

    return gather(ys, slot)


def _combine_kernel(yg_ref, gate_ref, h1_ref, swg_ref, swu_ref, swd_ref, lg_ref, lb_ref, *rest):
    out_ref = rest[-1]
    h1 = h1_ref[...]
    xb = h1.astype(BF16)
    g = jnp.dot(xb, swg_ref[...], preferred_element_type=F32)
    u = jnp.dot(xb, swu_ref[...], preferred_element_type=F32)
    hmid = (g * jax.nn.sigmoid(g) * u).astype(BF16)
    shared = jnp.dot(hmid, swd_ref[...], preferred_element_type=F32)

    gates = gate_ref[...]
    tc, wp = yg_ref.shape[1], yg_ref.shape[2]
    f_lo = jnp.zeros((tc, wp), F32)
    f_hi = jnp.zeros((tc, wp), F32)
    for k in range(TOP_K):
        lo, hi = _unpack_bf16_pairs(yg_ref[k])
        gk = gates[:, k:k + 1]
        f_lo = f_lo + gk * lo
        f_hi = f_hi + gk * hi
    routed = jnp.concatenate([f_lo, f_hi], axis=1)
    out_ref[...] = _layer_norm(ALPHA * h1 + (routed + shared), lg_ref[...], lb_ref[...])


def _combine(yg, gate_t, h1, swg, swu, swd, ln_g, ln_b, part, prev_out):
    T, D = h1.shape
    K, Tp, Wp = yg.shape
    tc = min(TD_MOE, Tp)
    off = part * (Tp // tc)
    row = lambda i: (i + off, 0)
    const = lambda i: (0, 0)
    res = lambda a: pl.BlockSpec(a.shape, const)
    in_specs = [
        pl.BlockSpec((K, tc, Wp), lambda i: (0, i, 0)),
        pl.BlockSpec((tc, TOP_K), row),
        pl.BlockSpec((tc, D), row),
        res(swg), res(swu), res(swd),
        pl.BlockSpec((1, D), const),
        pl.BlockSpec((1, D), const),
    ]
    args = [yg, gate_t, h1, swg, swu, swd, ln_g.reshape(1, D), ln_b.reshape(1, D)]
    aliases = {}
    if prev_out is not None:
        in_specs.append(pl.BlockSpec(memory_space=pl.ANY))
        args.append(prev_out)
        aliases = {len(args) - 1: 0}
    return pl.pallas_call(
        _combine_kernel,
        grid=(Tp // tc,),
        in_specs=in_specs,
        out_specs=pl.BlockSpec((tc, D), row),
        out_shape=jax.ShapeDtypeStruct((T, D), F32),
        input_output_aliases=aliases,
        compiler_params=_cparams(1),
        name="combine",
    )(*args)


def _moe_layer(layer, h1, h1p, router_w, router_bias, wg, wu, wd, swg, swu, swd, ln_g, ln_b):
    T, D = h1.shape
    E = N_EXPERTS
    M = T * TOP_K
    eidx, gate, rank, cnt = _router(h1, router_w, router_bias)

    counts = cnt[:, 0].astype(I32)
    padded = (counts + BLK_MOE - 1) // BLK_MOE * BLK_MOE
    pad_end = jnp.cumsum(padded)
    pad_start = pad_end - padded
    nb = -(-(M + E * (BLK_MOE - 1)) // BLK_MOE)
    n_slots = nb * BLK_MOE
    slot = _slots(pad_start.astype(I32), eidx, rank)
    n_valid = (pad_end[-1] // BLK_MOE).astype(I32).reshape(1)
    blk_start = jnp.arange(nb, dtype=I32) * BLK_MOE
    blk_exp = jnp.minimum(jnp.sum((pad_end[None, :] <= blk_start[:, None]).astype(I32), axis=1), E - 1)

    eids = jnp.arange(E, dtype=I32)
    nonempty = counts > 0
    order = jnp.cumsum(nonempty.astype(I32)) - 1
    nxt_e = lax.cummin(jnp.where(nonempty, eids, E), reverse=True)
    nxt_e = jnp.concatenate([nxt_e[1:], jnp.full((1,), E, I32)])
    blk_par = (order % 2)[blk_exp].astype(I32)
    blk_next = jnp.where(nxt_e < E, nxt_e, -1)[blk_exp].astype(I32)

    xs = _dispatch(slot, h1p, n_slots)
    ys = _experts(layer, blk_exp, n_valid, blk_par, blk_next, xs, wg, wu, wd)
    n_parts = MOE_COMBINE_PARTS if T % (MOE_COMBINE_PARTS * SC_CORES * SC_SUBCORES * SC_IDX_CHUNK) == 0 else 1
    tp = T // n_parts
    gate_t = gate.T
    shared_w = (swg.astype(BF16), swu.astype(BF16), swd.astype(BF16))
    parts = [_gather_rows(slot[:, p * tp:(p + 1) * tp], ys) for p in range(n_parts)]
    out = None
    for p in range(n_parts):
        out = _combine(parts[p], gate_t, h1, *shared_w, ln_g, ln_b, p, out)
    return out


def kernel(x, emb_ln_g, emb_ln_b, w_in, b_gate, lam_q1, lam_k1, lam_q2, lam_k2, subln_g, w_proj_a, na_rpb, w_proj_b, w_out, ln1_g, ln1_b, router_w, router_bias, exp_w_gate, exp_w_up, exp_w_down, sh_w_gate, sh_w_up, sh_w_down, ln2_g, ln2_b):
    B, S, D = x.shape
    T = B * S
    rows = S // GRID_W
    h = x.reshape(T, D)
    for l in range(DEPTH):
        lam_init = 0.8 - 0.6 * math.exp(-0.3 * l)
        outs = _inproj(h, emb_ln_g, emb_ln_b, w_in[l].astype(BF16), apply_ln=(l == 0))
        if l == 0:
            h, outs = outs[0], outs[1:]
        qa, ka, va, qn, kn, vn, ga, gb = outs
        lamv = jnp.stack([lam_q1[l], lam_k1[l], lam_q2[l], lam_k2[l]]).astype(F32)
        oa = _diffattn(qa.reshape(B, S, -1), ka.reshape(B, S, -1), va.reshape(B, S, -1),
                       lamv, subln_g[l], lam_init)
        ob = _natten(qn.reshape(B, S, -1), kn.reshape(B, S, -1), vn.reshape(B, S, -1),
                     _na_bias_table(na_rpb[l], rows))
        h1, h1p = _mixout(oa.reshape(T, -1), ob.reshape(T, -1), ga, gb, h, b_gate[l],
                          w_proj_a[l].astype(BF16), w_proj_b[l].astype(BF16), w_out[l].astype(BF16),
                          ln1_g[l], ln1_b[l])
        h = _moe_layer(l, h1, h1p, router_w[l], router_bias[l], exp_w_gate, exp_w_up, exp_w_down,
                       sh_w_gate[l], sh_w_up[l], sh_w_down[l], ln2_g[l], ln2_b[l])
    return h.reshape(B, S, D)
```
